```python
import math
import jax, jax.numpy as jnp
from jax import lax
import numpy as np

D_MODEL = 2048
BATCH = 2
SEQ = 4096
DEPTH = 2

SSD_WIDTH = 1024
SSD_HEADDIM = 64
SSD_HEADS = SSD_WIDTH // SSD_HEADDIM
SSD_STATE = 128
SSD_GROUPS = 2
CONV_WIDTH = 4
CHUNK = 128
CONV_DIM = SSD_WIDTH + 2 * SSD_GROUPS * SSD_STATE
ATT_HEADS = 8
ATT_HEADDIM = 128
ATT_WIDTH = ATT_HEADS * ATT_HEADDIM
IDX_HEADS = 16
IDX_DIM = 64
TOPK_MAX = 256
Q_BLOCK = 128
D_MIX = SSD_WIDTH + ATT_WIDTH
D_IN = SSD_WIDTH + CONV_DIM + SSD_HEADS + 3 * ATT_WIDTH + IDX_HEADS * IDX_DIM + IDX_DIM + IDX_HEADS
N_EXPERTS = 32
N_EXPERT_GROUPS = 8
EXPERTS_PER_GROUP = N_EXPERTS // N_EXPERT_GROUPS
TOP_K = 2
D_FF = 512
DISPATCH_BLOCK = 128
PLE_DIM = 256
ALPHA = (2.0 * DEPTH) ** 0.25
BETA = (8.0 * DEPTH) ** -0.25
LN_EPS = 1e-5

kernel_name = 'hybrid_ssd_dsa_grouped_moe_deepnorm'


def _proj_splits():
    widths = [SSD_WIDTH, CONV_DIM, SSD_HEADS, ATT_WIDTH, ATT_WIDTH, ATT_WIDTH,
              IDX_HEADS * IDX_DIM, IDX_DIM, IDX_HEADS]
    return [int(v) for v in np.cumsum(widths)[:-1]]


def layer_norm(x, g, b):
    xf = x.astype(jnp.float32)
    mu = jnp.mean(xf, -1, keepdims=True)
    var = jnp.mean(jnp.square(xf - mu), -1, keepdims=True)
    return ((xf - mu) * lax.rsqrt(var + LN_EPS) * g.astype(jnp.float32) + b.astype(jnp.float32)).astype(x.dtype)


def rms_norm(x, g):
    xf = x.astype(jnp.float32)
    ms = jnp.mean(jnp.square(xf), -1, keepdims=True)
    return (xf * lax.rsqrt(ms + LN_EPS) * g.astype(jnp.float32)).astype(x.dtype)


def causal_depthwise_conv(u, w, b):
    out = lax.conv_general_dilated(u, w[:, None, :], window_strides=(1,),
                                   padding=[(CONV_WIDTH - 1, 0)],
                                   dimension_numbers=('NWC', 'WIO', 'NWC'),
                                   feature_group_count=u.shape[-1])
    return out + b


def ssd_chunked(xh, dt, a, bmat, cmat):
    bsz, seq, nh, hd = xh.shape
    nc = seq // CHUNK
    hpg = nh // SSD_GROUPS
    xdt = (xh * dt[..., None]).reshape(bsz, nc, CHUNK, nh, hd)
    a_cs = jnp.cumsum((dt * a).reshape(bsz, nc, CHUNK, nh), axis=2)
    bc = bmat.reshape(bsz, nc, CHUNK, SSD_GROUPS, SSD_STATE)
    cc = cmat.reshape(bsz, nc, CHUNK, SSD_GROUPS, SSD_STATE)
    causal = jnp.tril(jnp.ones((CHUNK, CHUNK), dtype=bool))
    seg = a_cs[:, :, :, None, :] - a_cs[:, :, None, :, :]
    decay = jnp.exp(jnp.where(causal[None, None, :, :, None], seg, -jnp.inf))
    cb = jnp.repeat(jnp.einsum('bclgn,bcsgn->bclsg', cc, bc), hpg, axis=-1)
    y_diag = jnp.einsum('bclsh,bcshp->bclhp', cb * decay, xdt)
    bh = jnp.repeat(bc, hpg, axis=3)
    ch = jnp.repeat(cc, hpg, axis=3)
    decay_to_end = jnp.exp(a_cs[:, :, -1:, :] - a_cs)
    states = jnp.einsum('bclhn,bclh,bclhp->bchpn', bh, decay_to_end, xdt)
    chunk_decay = jnp.exp(a_cs[:, :, -1, :])

    def step(carry, inp):
        st, dec = inp
        return carry * dec[..., None, None] + st, carry

    init = jnp.zeros((bsz, nh, hd, SSD_STATE), xdt.dtype)
    _, prev = lax.scan(step, init, (jnp.moveaxis(states, 1, 0), jnp.moveaxis(chunk_decay, 1, 0)))
    prev = jnp.moveaxis(prev, 0, 1)
    y_off = jnp.einsum('bclhn,bchpn,bclh->bclhp', ch, prev, jnp.exp(a_cs))
    return (y_diag + y_off).reshape(bsz, seq, nh, hd)


def dsa_attention(q, k, v, iq, ik, iw):
    bsz, seq = q.shape[0], q.shape[1]
    topk = min(TOPK_MAX, seq // 4)
    nb = seq // Q_BLOCK
    key_pos = jnp.arange(seq)
    scale = ATT_HEADDIM ** -0.5
    ikf = ik.astype(jnp.float32)

    def block(i):
        start = i * Q_BLOCK
        qb = lax.dynamic_slice_in_dim(q, start, Q_BLOCK, axis=1)
        iqb = lax.dynamic_slice_in_dim(iq, start, Q_BLOCK, axis=1).astype(jnp.float32)
        iwb = lax.dynamic_slice_in_dim(iw, start, Q_BLOCK, axis=1).astype(jnp.float32)
        q_pos = start + jnp.arange(Q_BLOCK)
        dots = jnp.einsum('bqhd,bsd->bqhs', iqb, ikf) * (IDX_DIM ** -0.5)
        score = jnp.einsum('bqh,bqhs->bqs', iwb * (IDX_HEADS ** -0.5), jax.nn.relu(dots))
        visible = key_pos[None, :] <= q_pos[:, None]
        score = jnp.where(visible[None], score, -jnp.inf)
        _, sel = lax.top_k(score, topk)
        sel_valid = sel <= q_pos[None, :, None]
        kg = jax.vmap(lambda kk, ii: kk[ii])(k, sel)
        vg = jax.vmap(lambda vv, ii: vv[ii])(v, sel)
        logits = jnp.einsum('bqhd,bqkhd->bqhk', qb, kg).astype(jnp.float32) * scale
        logits = jnp.where(sel_valid[:, :, None, :], logits, -jnp.inf)
        prob = jax.nn.softmax(logits, axis=-1).astype(v.dtype)
        return jnp.einsum('bqhk,bqkhd->bqhd', prob, vg)

    outs = lax.map(block, jnp.arange(nb))
    return jnp.moveaxis(outs, 0, 1).reshape(bsz, seq, ATT_HEADS * ATT_HEADDIM)


def hybrid_mixer(h, w_in, conv_w, conv_b, dt_bias, a_log, d_skip, ssm_norm_w,
                 idx_k_norm_g, idx_k_norm_b, w_out):
    bsz, seq, _ = h.shape
    proj = h @ w_in
    z, xbc, dt_raw, q, k, v, iq, ik, iw = jnp.split(proj, _proj_splits(), axis=-1)
    xbc = jax.nn.silu(causal_depthwise_conv(xbc, conv_w, conv_b))
    xs, bm, cm = jnp.split(xbc, [SSD_WIDTH, SSD_WIDTH + SSD_GROUPS * SSD_STATE], axis=-1)
    xh = xs.reshape(bsz, seq, SSD_HEADS, SSD_HEADDIM).astype(jnp.float32)
    bm = bm.reshape(bsz, seq, SSD_GROUPS, SSD_STATE).astype(jnp.float32)
    cm = cm.reshape(bsz, seq, SSD_GROUPS, SSD_STATE).astype(jnp.float32)
    dt = jax.nn.softplus(dt_raw.astype(jnp.float32) + dt_bias.astype(jnp.float32))
    a = -jnp.exp(a_log.astype(jnp.float32))
    y = ssd_chunked(xh, dt, a, bm, cm) + xh * d_skip.astype(jnp.float32)[:, None]
    y = y.reshape(bsz, seq, SSD_WIDTH).astype(h.dtype)
    y_ssd = rms_norm(y * jax.nn.silu(z), ssm_norm_w)
    q = q.reshape(bsz, seq, ATT_HEADS, ATT_HEADDIM)
    k = k.reshape(bsz, seq, ATT_HEADS, ATT_HEADDIM)
    v = v.reshape(bsz, seq, ATT_HEADS, ATT_HEADDIM)
    iq = iq.reshape(bsz, seq, IDX_HEADS, IDX_DIM)
    ik = layer_norm(ik, idx_k_norm_g, idx_k_norm_b)
    y_att = dsa_attention(q, k, v, iq, ik, iw)
    return jnp.concatenate([y_ssd, y_att.astype(h.dtype)], axis=-1) @ w_out


def grouped_moe(h, router_w, router_b, w_gate, w_up, w_down):
    bsz, seq, d = h.shape
    t = bsz * seq
    hf = h.reshape(t, d)
    affinity = jax.nn.sigmoid((hf @ router_w).astype(jnp.float32))
    biased = affinity + router_b.astype(jnp.float32)
    grp = biased.reshape(t, N_EXPERT_GROUPS, EXPERTS_PER_GROUP)
    grp_score = jnp.sum(lax.top_k(grp, 2)[0], axis=-1)
    best_group = jnp.argmax(grp_score, axis=-1)
    in_group = (jnp.arange(N_EXPERTS) // EXPERTS_PER_GROUP)[None, :] == best_group[:, None]
    _, expert_idx = lax.top_k(jnp.where(in_group, biased, -jnp.inf), TOP_K)
    gate = jnp.take_along_axis(affinity, expert_idx, axis=-1)
    gate = gate / jnp.sum(gate, axis=-1, keepdims=True)
    n_assign = t * TOP_K
    flat_e = expert_idx.reshape(n_assign)
    flat_tok = jnp.repeat(jnp.arange(t, dtype=jnp.int32), TOP_K)
    order = jnp.argsort(flat_e)
    sorted_e = flat_e[order]
    counts = jnp.bincount(flat_e, length=N_EXPERTS)
    padded = (counts + DISPATCH_BLOCK - 1) // DISPATCH_BLOCK * DISPATCH_BLOCK
    pad_end = jnp.cumsum(padded)
    pad_start = pad_end - padded
    start = jnp.cumsum(counts) - counts
    dest_sorted = pad_start[sorted_e] + (jnp.arange(n_assign) - start[sorted_e])
    cap = -(-n_assign // DISPATCH_BLOCK) * DISPATCH_BLOCK + N_EXPERTS * DISPATCH_BLOCK
    n_blocks = cap // DISPATCH_BLOCK
    slot_tok = jnp.full((cap,), t, dtype=jnp.int32).at[dest_sorted].set(flat_tok[order])
    block_expert = jnp.minimum(
        jnp.searchsorted(pad_end, jnp.arange(n_blocks) * DISPATCH_BLOCK, side='right'), N_EXPERTS - 1)
    hpad = jnp.concatenate([hf, jnp.zeros((1, d), hf.dtype)], axis=0)
    xb = hpad[slot_tok].reshape(n_blocks, DISPATCH_BLOCK, d)

    def run(args):
        xblk, e = args
        return (jax.nn.silu(xblk @ w_gate[e]) * (xblk @ w_up[e])) @ w_down[e]

    yb = lax.map(run, (xb, block_expert)).reshape(cap, d)
    dest = jnp.zeros((n_assign,), dtype=dest_sorted.dtype).at[order].set(dest_sorted)
    y = yb[dest].reshape(t, TOP_K, d)
    return jnp.einsum('tk,tkd->td', gate.astype(y.dtype), y).reshape(bsz, seq, d)


def setup_inputs(seed: int = 0) -> dict:
    key = jax.random.key(seed)
    ks = jax.random.split(key, 24)
    f32 = jnp.float32

    def nrm(k, shape, scale):
        return jax.random.normal(k, shape, f32) * scale

    dt0 = jnp.exp(jax.random.uniform(ks[5], (DEPTH, SSD_HEADS), f32, math.log(1e-3), math.log(1e-1)))
    return {
        'x': nrm(ks[0], (BATCH, SEQ, D_MODEL), 1.0),
        'p': nrm(ks[1], (DEPTH, BATCH, SEQ, PLE_DIM), 1.0),
        'w_in': nrm(ks[2], (DEPTH, D_MODEL, D_IN), D_MODEL ** -0.5),
        'conv_w': nrm(ks[3], (DEPTH, CONV_WIDTH, CONV_DIM), CONV_WIDTH ** -0.5),
        'conv_b': nrm(ks[4], (DEPTH, CONV_DIM), 0.02),
        'dt_bias': dt0 + jnp.log(-jnp.expm1(-dt0)),
        'a_log': jnp.log(jax.random.uniform(ks[6], (DEPTH, SSD_HEADS), f32, 1.0, 16.0)),
        'd_skip': 1.0 + nrm(ks[7], (DEPTH, SSD_HEADS), 0.1),
        'ssm_norm_w': 1.0 + nrm(ks[8], (DEPTH, SSD_WIDTH), 0.1),
        'idx_k_norm_g': 1.0 + nrm(ks[9], (DEPTH, IDX_DIM), 0.1),
        'idx_k_norm_b': nrm(ks[10], (DEPTH, IDX_DIM), 0.02),
        'w_out': nrm(ks[11], (DEPTH, D_MIX, D_MODEL), BETA * D_MIX ** -0.5),
        'ln1_g': 1.0 + nrm(ks[12], (DEPTH, D_MODEL), 0.05),
        'ln1_b': nrm(ks[13], (DEPTH, D_MODEL), 0.02),
        'router_w': nrm(ks[14], (D_MODEL, N_EXPERTS), D_MODEL ** -0.5),
        'router_b': nrm(ks[15], (N_EXPERTS,), 0.01),
        'w_gate': nrm(ks[16], (DEPTH, N_EXPERTS, D_MODEL, D_FF), D_MODEL ** -0.5),
        'w_up': nrm(ks[17], (DEPTH, N_EXPERTS, D_MODEL, D_FF), D_MODEL ** -0.5),
        'w_down': nrm(ks[18], (DEPTH, N_EXPERTS, D_FF, D_MODEL), BETA * D_FF ** -0.5),
        'ple_w': nrm(ks[19], (DEPTH, PLE_DIM, D_MODEL), BETA * PLE_DIM ** -0.5),
        'ple_gate_w': nrm(ks[20], (DEPTH, D_MODEL, D_MODEL), D_MODEL ** -0.5),
        'ln2_g': 1.0 + nrm(ks[21], (DEPTH, D_MODEL), 0.05),
        'ln2_b': nrm(ks[22], (DEPTH, D_MODEL), 0.02),
    }


def reference(x, p, w_in, conv_w, conv_b, dt_bias, a_log, d_skip, ssm_norm_w,
              idx_k_norm_g, idx_k_norm_b, w_out, ln1_g, ln1_b, router_w, router_b,
              w_gate, w_up, w_down, ple_w, ple_gate_w, ln2_g, ln2_b):
    h = x
    for i in range(DEPTH):
        mix = hybrid_mixer(h, w_in[i], conv_w[i], conv_b[i], dt_bias[i], a_log[i], d_skip[i],
                           ssm_norm_w[i], idx_k_norm_g[i], idx_k_norm_b[i], w_out[i])
        h = layer_norm(ALPHA * h + mix, ln1_g[i], ln1_b[i])
        ffn = grouped_moe(h, router_w, router_b, w_gate[i], w_up[i], w_down[i])
        ple = jax.nn.sigmoid(h @ ple_gate_w[i]) * (p[i] @ ple_w[i])
        h = layer_norm(ALPHA * h + ffn + ple, ln2_g[i], ln2_b[i])
    return h
```

```python
import functools

import jax
import jax.numpy as jnp
from jax import lax
from jax.experimental import pallas as pl
from jax.experimental.pallas import tpu as pltpu

F32 = jnp.float32
BF16 = jnp.bfloat16
I32 = jnp.int32

SSD_WIDTH = 1024
SSD_HEADDIM = 64
SSD_HEADS = 16
SSD_STATE = 128
SSD_GROUPS = 2
CONV_WIDTH = 4
CHUNK = 128
CONV_DIM = SSD_WIDTH + 2 * SSD_GROUPS * SSD_STATE
ATT_HEADS = 8
ATT_HEADDIM = 128
ATT_WIDTH = ATT_HEADS * ATT_HEADDIM
IDX_HEADS = 16
IDX_DIM = 64
TOPK_MAX = 256
Q_BLOCK = 128
N_EXPERTS = 32
N_EXPERT_GROUPS = 8
EXPERTS_PER_GROUP = 4
D_FF = 512
ALPHA = (2.0 * 2) ** 0.25
LN_EPS = 1e-5

OFF_Z = 0
OFF_XBC = OFF_Z + SSD_WIDTH
OFF_DT = OFF_XBC + CONV_DIM
OFF_Q = OFF_DT + SSD_HEADS
OFF_IQ = OFF_Q + 3 * ATT_WIDTH
OFF_IK = OFF_IQ + IDX_HEADS * IDX_DIM
OFF_IW = OFF_IK + IDX_DIM
SM_DT = 0
SM_IK = SSD_HEADS
SM_IW = SM_IK + IDX_DIM
SM_W = 128

V7X_LANES = 128
VMEM_LIMIT = 56 * 1024 * 1024
INT_MIN = -2147483648
KEY_CHUNK = 256
MOE_BLOCK = 256


def _sigmoid(x):
    return 1.0 / (1.0 + jnp.exp(-x))


def _softplus(x):
    return jnp.maximum(x, 0.0) + jnp.log(1.0 + jnp.exp(-jnp.abs(x)))


def _nt_dot(a, b, **kw):
    return lax.dot_general(a, b, (((1,), (1,)), ((), ())), preferred_element_type=F32, **kw)


def _layer_norm_rows(u, g, b):
    mu = jnp.mean(u, axis=-1, keepdims=True)
    d = u - mu
    var = jnp.mean(d * d, axis=-1, keepdims=True)
    return d * lax.rsqrt(var + LN_EPS) * g + b


def _mm_kernel(x_ref, w_ref, o_ref):
    o_ref[...] = jnp.dot(x_ref[...], w_ref[...], preferred_element_type=F32).astype(o_ref.dtype)


def _matmul(x, w, out_dtype, tm, tn):
    m, k = x.shape
    n = w.shape[1]
    return pl.pallas_call(
        _mm_kernel,
        name="in_proj",
        grid=(m // tm, n // tn),
        in_specs=[pl.BlockSpec((tm, k), lambda i, j: (i, 0)),
                  pl.BlockSpec((k, tn), lambda i, j: (0, j))],
        out_specs=pl.BlockSpec((tm, tn), lambda i, j: (i, j)),
        out_shape=jax.ShapeDtypeStruct((m, n), out_dtype),
        compiler_params=pltpu.CompilerParams(
            dimension_semantics=("parallel", "parallel"), vmem_limit_bytes=VMEM_LIMIT),
    )(x, w)


def _ssd_kernel(za_ref, sm_ref, cw_ref, cb_ref, dtb_ref, a_ref, dexp_ref, nw_ref, e_ref,
                o_ref, cbuf, state, ybuf):
    c = pl.program_id(1)
    L = CHUNK

    @pl.when(c == 0)
    def _():
        cbuf[0:8, :] = jnp.zeros((8, CONV_DIM), F32)
        state[...] = jnp.zeros_like(state)

    cbuf[8:8 + L, :] = za_ref[:, SSD_WIDTH:SSD_WIDTH + CONV_DIM]
    acc = cb_ref[...] + cw_ref[0:1, :] * cbuf[5:5 + L, :]
    acc = acc + cw_ref[1:2, :] * cbuf[6:6 + L, :]
    acc = acc + cw_ref[2:3, :] * cbuf[7:7 + L, :]
    acc = acc + cw_ref[3:4, :] * cbuf[8:8 + L, :]
    cbuf[0:8, :] = cbuf[L:L + 8, :]
    xbc = acc * _sigmoid(acc)
    xs = xbc[:, 0:SSD_WIDTH]
    bm = xbc[:, SSD_WIDTH:SSD_WIDTH + SSD_GROUPS * SSD_STATE]
    cm = xbc[:, SSD_WIDTH + SSD_GROUPS * SSD_STATE:CONV_DIM]

    dt = _softplus(sm_ref[...] + dtb_ref[...])
    da = dt * a_ref[...]
    row = lax.broadcasted_iota(I32, (L, L), 0)
    col = lax.broadcasted_iota(I32, (L, L), 1)
    causal = col <= row
    tri = jnp.where(causal, 1.0, 0.0).astype(F32)
    a_cs = jnp.dot(tri, da, preferred_element_type=F32, precision=lax.Precision.HIGHEST)
    a_cs_t = a_cs.T
    expa = jnp.exp(a_cs)
    dte = jnp.exp(a_cs[L - 1:L, :] - a_cs)
    stacked = jnp.concatenate([dt, expa, dte], axis=0)
    expd = jnp.dot(stacked, e_ref[...], preferred_element_type=F32,
                   precision=lax.Precision.HIGHEST)
    dt_x = expd[0:L, :]
    expa_x = expd[L:2 * L, :]
    dte_x = expd[2 * L:3 * L, :]
    xdt = xs * dt_x
    hpg = SSD_HEADS // SSD_GROUPS
    gw = hpg * SSD_HEADDIM
    for g in range(SSD_GROUPS):
        cm_g = cm[:, g * SSD_STATE:(g + 1) * SSD_STATE].astype(BF16)
        bm_gf = bm[:, g * SSD_STATE:(g + 1) * SSD_STATE]
        bm_g = bm_gf.astype(BF16)
        cb = _nt_dot(cm_g, bm_g)
        for hh in range(hpg):
            h = g * hpg + hh
            seg = a_cs[:, h:h + 1] - a_cs_t[h:h + 1, :]
            dec = jnp.exp(jnp.where(causal, seg, -jnp.inf))
            mh = (cb * dec).astype(BF16)
            xh = xdt[:, h * SSD_HEADDIM:(h + 1) * SSD_HEADDIM].astype(BF16)
            ybuf[:, h * SSD_HEADDIM:(h + 1) * SSD_HEADDIM] = jnp.dot(
                mh, xh, preferred_element_type=F32)
        st_g = state[:, g * gw:(g + 1) * gw]
        y_off = jnp.dot(cm_g, st_g.astype(BF16), preferred_element_type=F32)
        ybuf[:, g * gw:(g + 1) * gw] = (ybuf[:, g * gw:(g + 1) * gw]
                                        + y_off * expa_x[:, g * gw:(g + 1) * gw])
        xw = (xdt[:, g * gw:(g + 1) * gw] * dte_x[:, g * gw:(g + 1) * gw]).astype(BF16)
        upd = jnp.dot(bm_gf.T.astype(BF16), xw, preferred_element_type=F32)
        state[:, g * gw:(g + 1) * gw] = st_g * expa_x[L - 1:L, g * gw:(g + 1) * gw] + upd

    y = ybuf[...] + xs * dexp_ref[...]
    z = za_ref[:, 0:SSD_WIDTH]
    yg = y * (z * _sigmoid(z))
    ms = jnp.mean(yg * yg, axis=-1, keepdims=True)
    o_ref[...] = (yg * lax.rsqrt(ms + LN_EPS) * nw_ref[...]).astype(o_ref.dtype)


def _ssd(za, sm, conv_w, conv_b, dt_bias, a_log, d_skip, ssm_norm_w, bsz, seq):
    nc = seq // CHUNK
    t = bsz * seq
    pad = SM_W - SSD_HEADS
    dtb = jnp.pad(dt_bias.astype(F32), (0, pad)).reshape(1, SM_W)
    a = jnp.pad(-jnp.exp(a_log.astype(F32)), (0, pad)).reshape(1, SM_W)
    dexp = jnp.repeat(d_skip.astype(F32), SSD_HEADDIM).reshape(1, SSD_WIDTH)
    e = (jnp.arange(SSD_WIDTH)[None, :] // SSD_HEADDIM == jnp.arange(SM_W)[:, None]).astype(F32)
    const = lambda shape: pl.BlockSpec(shape, lambda b, c: (0, 0))
    return pl.pallas_call(
        _ssd_kernel,
        name="ssd",
        grid=(bsz, nc),
        in_specs=[pl.BlockSpec((CHUNK, SSD_WIDTH + CONV_DIM), lambda b, c: (b * nc + c, 0)),
                  pl.BlockSpec((CHUNK, SM_W), lambda b, c: (b * nc + c, 0)),
                  const((CONV_WIDTH, CONV_DIM)), const((1, CONV_DIM)), const((1, SM_W)),
                  const((1, SM_W)), const((1, SSD_WIDTH)), const((1, SSD_WIDTH)),
                  const((SM_W, SSD_WIDTH))],
        out_specs=pl.BlockSpec((CHUNK, SSD_WIDTH), lambda b, c: (b * nc + c, 0)),
        out_shape=jax.ShapeDtypeStruct((t, SSD_WIDTH), BF16),
        scratch_shapes=[pltpu.VMEM((CHUNK + 8, CONV_DIM), F32),
                        pltpu.VMEM((SSD_STATE, SSD_WIDTH), F32),
                        pltpu.VMEM((CHUNK, SSD_WIDTH), F32)],
        compiler_params=pltpu.CompilerParams(
            dimension_semantics=("parallel", "arbitrary"), vmem_limit_bytes=VMEM_LIMIT),
    )(za, sm, conv_w.astype(F32), conv_b.astype(F32).reshape(1, CONV_DIM), dtb, a, dexp,
      ssm_norm_w.astype(F32).reshape(1, SSD_WIDTH), e)


def _dsa_kernel(q_ref, k_ref, v_ref, iq_ref, smb_ref, smq_ref, g_ref, b_ref, o_ref,
                ikn, keys, *, topk):
    qi = pl.program_id(1)
    QB = Q_BLOCK
    KC = KEY_CHUNK

    @pl.when(qi == 0)
    def _():
        ik = smb_ref[:, SM_IK:SM_IK + IDX_DIM]
        ikn[...] = _layer_norm_rows(ik, g_ref[...], b_ref[...]).astype(BF16)

    nkc = (qi * QB) // KC + 1
    w = smq_ref[:, SM_IW:SM_IW + IDX_HEADS] * (IDX_DIM ** -0.5 * IDX_HEADS ** -0.5)
    qpos = qi * QB + lax.broadcasted_iota(I32, (QB, KC), 0)
    kiota = lax.broadcasted_iota(I32, (QB, KC), 1)

    def score_chunk(c, carry):
        off = pl.multiple_of(c * KC, KC)
        kc = ikn[pl.ds(off, KC), :]
        acc = jnp.zeros((QB, KC), F32)
        for h in range(IDX_HEADS):
            d = _nt_dot(iq_ref[:, h * IDX_DIM:(h + 1) * IDX_DIM], kc)
            acc = acc + w[:, h:h + 1] * jnp.maximum(d, 0.0)
        bits = lax.bitcast_convert_type(acc, I32)
        key = jnp.where(bits < 0, bits ^ 0x7FFFFFFF, bits)
        key = jnp.where(acc == 0.0, 0, key)
        key = jnp.where(off + kiota <= qpos, key, INT_MIN)
        keys[c] = key
        return carry

    lax.fori_loop(0, nkc, score_chunk, 0)

    def bisect(it, t_u):
        bit = jnp.left_shift(jnp.int32(1), 31 - it)
        cand_u = t_u | bit
        cand_s = cand_u ^ INT_MIN

        def count_chunk(c, acc):
            m = jnp.where(keys[c] >= cand_s, 1.0, 0.0)
            for j in range(KC // V7X_LANES):
                acc = acc + m[:, j * V7X_LANES:(j + 1) * V7X_LANES]
            return acc

        acc = lax.fori_loop(0, nkc, count_chunk, jnp.zeros((QB, V7X_LANES), F32))
        cnt = jnp.sum(acc, axis=1, keepdims=True)
        return jnp.where(cnt >= float(topk), cand_u, t_u)

    t_u = lax.fori_loop(0, 32, bisect, jnp.zeros((QB, 1), I32))
    thr = jnp.maximum(t_u ^ INT_MIN, INT_MIN + 1)

    scale = ATT_HEADDIM ** -0.5
    for h in range(ATT_HEADS):
        lo, hi = h * ATT_HEADDIM, (h + 1) * ATT_HEADDIM
        qh = q_ref[:, lo:hi]

        def att_chunk(c, carry, lo=lo, hi=hi, qh=qh):
            m, l, acc = carry
            off = pl.multiple_of(c * KC, KC)
            kc = k_ref[pl.ds(off, KC), lo:hi]
            vc = v_ref[pl.ds(off, KC), lo:hi]
            s = _nt_dot(qh, kc) * scale
            s = jnp.where(keys[c] >= thr, s, -1e30)
            m_new = jnp.maximum(m, jnp.max(s, axis=1, keepdims=True))
            alpha = jnp.exp(m - m_new)
            p = jnp.exp(s - m_new)
            l = alpha * l + jnp.sum(p, axis=1, keepdims=True)
            acc = alpha * acc + jnp.dot(p.astype(BF16), vc, preferred_element_type=F32)
            return m_new, l, acc

        m0 = jnp.full((QB, 1), -1e30, F32)
        l0 = jnp.zeros((QB, 1), F32)
        a0 = jnp.zeros((QB, ATT_HEADDIM), F32)
        _, l, acc = lax.fori_loop(0, nkc, att_chunk, (m0, l0, a0))
        o_ref[:, lo:hi] = (acc / l).astype(o_ref.dtype)


def _dsa(qkvi, sm, g, b, bsz, seq):
    nq = seq // Q_BLOCK
    t = bsz * seq
    topk = min(TOPK_MAX, seq // 4)
    single = pl.Buffered(1)
    return pl.pallas_call(
        functools.partial(_dsa_kernel, topk=topk),
        name="dsa",
        grid=(bsz, nq),
        in_specs=[pl.BlockSpec((Q_BLOCK, ATT_WIDTH), lambda bi, qi: (bi * nq + qi, 0)),
                  pl.BlockSpec((seq, ATT_WIDTH), lambda bi, qi: (bi, 1), pipeline_mode=single),
                  pl.BlockSpec((seq, ATT_WIDTH), lambda bi, qi: (bi, 2), pipeline_mode=single),
                  pl.BlockSpec((Q_BLOCK, IDX_HEADS * IDX_DIM), lambda bi, qi: (bi * nq + qi, 3)),
                  pl.BlockSpec((seq, SM_W), lambda bi, qi: (bi, 0), pipeline_mode=single),
                  pl.BlockSpec((Q_BLOCK, SM_W), lambda bi, qi: (bi * nq + qi, 0)),
                  pl.BlockSpec((1, IDX_DIM), lambda bi, qi: (0, 0)),
                  pl.BlockSpec((1, IDX_DIM), lambda bi, qi: (0, 0))],
        out_specs=pl.BlockSpec((Q_BLOCK, ATT_WIDTH), lambda bi, qi: (bi * nq + qi, 0)),
        out_shape=jax.ShapeDtypeStruct((t, ATT_WIDTH), BF16),
        scratch_shapes=[pltpu.VMEM((seq, IDX_DIM), BF16),
                        pltpu.VMEM((seq // KEY_CHUNK, Q_BLOCK, KEY_CHUNK), I32)],
        compiler_params=pltpu.CompilerParams(
            dimension_semantics=("parallel", "arbitrary"), vmem_limit_bytes=VMEM_LIMIT),
    )(qkvi, qkvi, qkvi, qkvi, sm, sm, g.astype(F32).reshape(1, IDX_DIM),
      b.astype(F32).reshape(1, IDX_DIM))


def _route_rows(logits_t, rb):
    aff = _sigmoid(logits_t)
    biased = aff + rb
    rb_rows = [biased[r:r + 1, :] for r in range(N_EXPERTS)]
    ra_rows = [aff[r:r + 1, :] for r in range(N_EXPERTS)]
    epg = EXPERTS_PER_GROUP
    gs = []
    for g in range(N_EXPERT_GROUPS):
        a, b, c, d = rb_rows[epg * g:epg * g + epg]
        p, q = jnp.maximum(a, b), jnp.minimum(a, b)
        r, s = jnp.maximum(c, d), jnp.minimum(c, d)
        gs.append(jnp.maximum(p, r) + jnp.maximum(jnp.minimum(p, r), jnp.maximum(q, s)))
    best = gs[0]
    bidx = jnp.zeros_like(best, dtype=I32)
    for g in range(1, N_EXPERT_GROUPS):
        better = gs[g] > best
        best = jnp.where(better, gs[g], best)
        bidx = jnp.where(better, g, bidx)
    vb = [rb_rows[j] for j in range(epg)]
    va = [ra_rows[j] for j in range(epg)]
    for g in range(1, N_EXPERT_GROUPS):
        pick = bidx == g
        vb = [jnp.where(pick, rb_rows[epg * g + j], vb[j]) for j in range(epg)]
        va = [jnp.where(pick, ra_rows[epg * g + j], va[j]) for j in range(epg)]
    t1, a1, i1 = vb[0], va[0], jnp.zeros_like(bidx)
    for j in range(1, epg):
        better = vb[j] > t1
        t1 = jnp.where(better, vb[j], t1)
        a1 = jnp.where(better, va[j], a1)
        i1 = jnp.where(better, j, i1)
    t2 = jnp.full_like(t1, -jnp.inf)
    a2 = jnp.zeros_like(a1)
    i2 = jnp.zeros_like(i1)
    for j in range(epg):
        vj = jnp.where(i1 == j, -jnp.inf, vb[j])
        better = vj > t2
        t2 = jnp.where(better, vj, t2)
        a2 = jnp.where(better, va[j], a2)
        i2 = jnp.where(better, j, i2)
    den = a1 + a2
    return bidx * epg + i1, bidx * epg + i2, a1 / den, a2 / den


def _outproj_kernel(ys_ref, ya_ref, w1_ref, w2_ref, h_ref, g_ref, b_ref, rwt_ref, rb_ref,
                    h1_ref, h1b_ref, eidx_ref, gate_ref):
    mix = jnp.dot(ys_ref[...], w1_ref[...], preferred_element_type=F32)
    mix = mix + jnp.dot(ya_ref[...], w2_ref[...], preferred_element_type=F32)
    h1 = _layer_norm_rows(ALPHA * h_ref[...] + mix, g_ref[...], b_ref[...])
    h1_ref[...] = h1
    h1b_ref[...] = h1.astype(BF16)
    logits_t = _nt_dot(rwt_ref[...], h1, precision=lax.Precision.HIGHEST)
    e1, e2, g1, g2 = _route_rows(logits_t, rb_ref[...])
    eidx_ref[0:1, :] = e1
    eidx_ref[1:2, :] = e2
    gate_ref[0:1, :] = g1
    gate_ref[1:2, :] = g2


def _outproj_ln_router(y_ssd, y_att, w_out, h, ln_g, ln_b, router_w, router_b, tm):
    t, d = h.shape
    w = w_out.astype(BF16)
    const = lambda shape: pl.BlockSpec(shape, lambda i: (0, 0))
    return pl.pallas_call(
        _outproj_kernel,
        name="outproj_ln_router",
        grid=(t // tm,),
        in_specs=[pl.BlockSpec((tm, SSD_WIDTH), lambda i: (i, 0)),
                  pl.BlockSpec((tm, ATT_WIDTH), lambda i: (i, 0)),
                  pl.BlockSpec((SSD_WIDTH, d), lambda i: (0, 0)),
                  pl.BlockSpec((ATT_WIDTH, d), lambda i: (1, 0)),
                  pl.BlockSpec((tm, d), lambda i: (i, 0)),
                  const((1, d)), const((1, d)), const((N_EXPERTS, d)), const((N_EXPERTS, 1))],
        out_specs=[pl.BlockSpec((tm, d), lambda i: (i, 0)),
                   pl.BlockSpec((tm, d), lambda i: (i, 0)),
                   pl.BlockSpec((2, tm), lambda i: (0, i)),
                   pl.BlockSpec((2, tm), lambda i: (0, i))],
        out_shape=[jax.ShapeDtypeStruct((t, d), F32), jax.ShapeDtypeStruct((t, d), BF16),
                   jax.ShapeDtypeStruct((2, t), I32), jax.ShapeDtypeStruct((2, t), F32)],
        compiler_params=pltpu.CompilerParams(
            dimension_semantics=("parallel",), vmem_limit_bytes=VMEM_LIMIT),
    )(y_ssd, y_att, w, w, h, ln_g.astype(F32).reshape(1, d), ln_b.astype(F32).reshape(1, d),
      router_w.astype(F32).T, router_b.astype(F32).reshape(N_EXPERTS, 1))


def _dispatch_kernel(dest_ref, h_hbm, xs_in_hbm, xs_hbm, sem, *, tb, t):
    del xs_in_hbm
    base = pl.program_id(0) * tb

    def row_copy(tok, slot):
        return pltpu.make_async_copy(h_hbm.at[pl.ds(tok, 1), :], xs_hbm.at[pl.ds(slot, 1), :], sem)

    def start(r, carry):
        for k in range(2):
            row_copy(base + r, dest_ref[k * t + base + r]).start()
        return carry

    def wait(r, carry):
        for k in range(2):
            row_copy(0, 0).wait()
        return carry

    lax.fori_loop(0, tb, start, 0)
    lax.fori_loop(0, tb, wait, 0)


def _dispatch(dest, h1, cap, tb):
    t, d = h1.shape
    xs0 = jnp.zeros((cap, d), h1.dtype)
    return pl.pallas_call(
        functools.partial(_dispatch_kernel, tb=tb, t=t),
        name="moe_dispatch",
        grid_spec=pltpu.PrefetchScalarGridSpec(
            num_scalar_prefetch=1,
            grid=(t // tb,),
            in_specs=[pl.BlockSpec(memory_space=pl.ANY), pl.BlockSpec(memory_space=pl.ANY)],
            out_specs=pl.BlockSpec(memory_space=pl.ANY),
            scratch_shapes=[pltpu.SemaphoreType.DMA(())]),
        out_shape=jax.ShapeDtypeStruct((cap, d), h1.dtype),
        input_output_aliases={2: 0},
        compiler_params=pltpu.CompilerParams(
            dimension_semantics=("arbitrary",), has_side_effects=True),
    )(dest, h1, xs0)


def _expert_kernel(be_ref, nu_ref, xs_ref, wg_ref, wu_ref, wd_ref, o_ref, wg_s, wu_s, wd_s):
    i = pl.program_id(0)

    @pl.when(i < nu_ref[0])
    def _():
        prev = be_ref[jnp.maximum(i - 1, 0)]

        @pl.when((i == 0) | (be_ref[i] != prev))
        def _():
            wg_s[...] = wg_ref[...].astype(BF16)
            wu_s[...] = wu_ref[...].astype(BF16)
            wd_s[...] = wd_ref[...].astype(BF16)

        x = xs_ref[...].astype(BF16)
        g = jnp.dot(x, wg_s[...], preferred_element_type=F32)
        u = jnp.dot(x, wu_s[...], preferred_element_type=F32)
        a = (g * _sigmoid(g) * u).astype(BF16)
        o_ref[...] = jnp.dot(a, wd_s[...], preferred_element_type=F32)

    @pl.when(i >= nu_ref[0])
    def _():
        o_ref[...] = jnp.zeros_like(o_ref)


def _experts(block_expert, n_used, xs, w_gate, w_up, w_down):
    cap, d = xs.shape
    nb = cap // MOE_BLOCK
    blk = lambda i, be, nu: (jnp.minimum(i, nu[0] - 1), 0)
    oblk = lambda i, be, nu: (i, 0)
    return pl.pallas_call(
        _expert_kernel,
        name="moe_experts",
        grid_spec=pltpu.PrefetchScalarGridSpec(
            num_scalar_prefetch=2,
            grid=(nb,),
            in_specs=[pl.BlockSpec((MOE_BLOCK, d), blk),
                      pl.BlockSpec((None, d, D_FF), lambda i, be, nu: (be[i], 0, 0)),
                      pl.BlockSpec((None, d, D_FF), lambda i, be, nu: (be[i], 0, 0)),
                      pl.BlockSpec((None, D_FF, d), lambda i, be, nu: (be[i], 0, 0))],
            out_specs=pl.BlockSpec((MOE_BLOCK, d), oblk),
            scratch_shapes=[pltpu.VMEM((d, D_FF), BF16), pltpu.VMEM((d, D_FF), BF16),
                            pltpu.VMEM((D_FF, d), BF16)]),
        out_shape=jax.ShapeDtypeStruct((cap, d), F32),
        compiler_params=pltpu.CompilerParams(
            dimension_semantics=("arbitrary",), vmem_limit_bytes=VMEM_LIMIT),
    )(block_expert, n_used, xs, w_gate, w_up, w_down)


def _combine_kernel(dest_ref, yb_hbm, gt_ref, o_ref, buf, sem, *, tc, t):
    base = pl.program_id(0) * tc

    def row_copy(slot, k, r):
        return pltpu.make_async_copy(yb_hbm.at[pl.ds(slot, 1), :], buf.at[k, pl.ds(r, 1), :], sem)

    def start(r, carry):
        for k in range(2):
            row_copy(dest_ref[k * t + base + r], k, r).start()
        return carry

    def wait(r, carry):
        for k in range(2):
            row_copy(0, k, r).wait()
        return carry

    lax.fori_loop(0, tc, start, 0)
    lax.fori_loop(0, tc, wait, 0)
    o_ref[...] = gt_ref[:, 0:1] * buf[0] + gt_ref[:, 1:2] * buf[1]


def _combine(dest, yb, gates_t, tc):
    t = gates_t.shape[0]
    d = yb.shape[1]
    return pl.pallas_call(
        functools.partial(_combine_kernel, tc=tc, t=t),
        name="moe_combine",
        grid_spec=pltpu.PrefetchScalarGridSpec(
            num_scalar_prefetch=1,
            grid=(t // tc,),
            in_specs=[pl.BlockSpec(memory_space=pl.ANY),
                      pl.BlockSpec((tc, 2), lambda i, dest: (i, 0))],
            out_specs=pl.BlockSpec((tc, d), lambda i, dest: (i, 0)),
            scratch_shapes=[pltpu.VMEM((2, tc, d), F32), pltpu.SemaphoreType.DMA(())]),
        out_shape=jax.ShapeDtypeStruct((t, d), F32),
        compiler_params=pltpu.CompilerParams(
            dimension_semantics=("arbitrary",), vmem_limit_bytes=VMEM_LIMIT),
    )(dest, yb, gates_t)


def _moe_plan(eidx, t):
    flat_e = eidx.reshape(2 * t)
    onehot = (flat_e[:, None] == jnp.arange(N_EXPERTS, dtype=I32)[None, :]).astype(I32)
    cs = jnp.cumsum(onehot, axis=0)
    rank = jnp.sum(onehot * cs, axis=1) - 1
    counts = cs[-1]
    padded = (counts + MOE_BLOCK - 1) // MOE_BLOCK * MOE_BLOCK
    pad_end = jnp.cumsum(padded)
    pad_start = pad_end - padded
    dest = (pad_start[flat_e] + rank).astype(I32)
    cap = 2 * t + N_EXPERTS * MOE_BLOCK
    nb = cap // MOE_BLOCK
    block_expert = jnp.minimum(
        jnp.searchsorted(pad_end, jnp.arange(nb, dtype=I32) * MOE_BLOCK, side='right'),
        N_EXPERTS - 1).astype(I32)
    n_used = (pad_end[-1] // MOE_BLOCK).astype(I32).reshape(1)
    return dest, block_expert, n_used, cap


def _ple_kernel(h1_ref, h1b_ref, ffn_ref, p_ref, wg_ref, wp_ref, g_ref, b_ref, o_ref, ob_ref):
    gate = _sigmoid(jnp.dot(h1b_ref[...], wg_ref[...], preferred_element_type=F32))
    pe = jnp.dot(p_ref[...].astype(BF16), wp_ref[...], preferred_element_type=F32)
    u = ALPHA * h1_ref[...] + ffn_ref[...] + gate * pe
    h2 = _layer_norm_rows(u, g_ref[...], b_ref[...])
    o_ref[...] = h2
    ob_ref[...] = h2.astype(BF16)


def _ple_ln(h1, h1b, ffn, p, ple_gate_w, ple_w, ln_g, ln_b, tm):
    t, d = h1.shape
    pd = p.shape[1]
    row = lambda w: pl.BlockSpec((tm, w), lambda i: (i, 0))
    const = lambda shape: pl.BlockSpec(shape, lambda i: (0, 0))
    return pl.pallas_call(
        _ple_kernel,
        name="ple_ln",
        grid=(t // tm,),
        in_specs=[row(d), row(d), row(d), row(pd), const((d, d)), const((pd, d)),
                  const((1, d)), const((1, d))],
        out_specs=[row(d), row(d)],
        out_shape=[jax.ShapeDtypeStruct((t, d), F32), jax.ShapeDtypeStruct((t, d), BF16)],
        compiler_params=pltpu.CompilerParams(
            dimension_semantics=("parallel",), vmem_limit_bytes=VMEM_LIMIT),
    )(h1, h1b, ffn, p, ple_gate_w.astype(BF16), ple_w.astype(BF16),
      ln_g.astype(F32).reshape(1, d), ln_b.astype(F32).reshape(1, d))


def _layer(h, hb, p_i, w_in, conv_w, conv_b, dt_bias, a_log, d_skip, ssm_norm_w,
           idx_k_norm_g, idx_k_norm_b, w_out, ln1_g, ln1_b, router_w, router_b,
           w_gate, w_up, w_down, ple_w, ple_gate_w, ln2_g, ln2_b, bsz, seq):
    t = bsz * seq
    tm = min(1024, t)
    w_a = w_in[:, OFF_Z:OFF_DT].astype(BF16)
    w_b = w_in[:, OFF_Q:OFF_IK].astype(BF16)
    w_c = jnp.concatenate(
        [w_in[:, OFF_DT:OFF_Q], w_in[:, OFF_IK:],
         jnp.zeros((w_in.shape[0], SM_W - SSD_HEADS - IDX_DIM - IDX_HEADS), w_in.dtype)],
        axis=1).astype(BF16)
    za = _matmul(hb, w_a, F32, tm, 512)
    qkvi = _matmul(hb, w_b, BF16, tm, 512)
    sm = _matmul(hb, w_c, F32, tm, SM_W)

    y_ssd = _ssd(za, sm, conv_w, conv_b, dt_bias, a_log, d_skip, ssm_norm_w, bsz, seq)
    y_att = _dsa(qkvi, sm, idx_k_norm_g, idx_k_norm_b, bsz, seq)
    h1, h1b, eidx, gates = _outproj_ln_router(
        y_ssd, y_att, w_out, h, ln1_g, ln1_b, router_w, router_b, min(256, t))

    dest, block_expert, n_used, cap = _moe_plan(eidx, t)
    xs = _dispatch(dest, h1, cap, min(512, t))
    yb = _experts(block_expert, n_used, xs, w_gate, w_up, w_down)
    ffn = _combine(dest, yb, gates.T, min(256, t))
    return _ple_ln(h1, h1b, ffn, p_i, ple_gate_w, ple_w, ln2_g, ln2_b, min(256, t))


def kernel(x, p, w_in, conv_w, conv_b, dt_bias, a_log, d_skip, ssm_norm_w, idx_k_norm_g,
           idx_k_norm_b, w_out, ln1_g, ln1_b, router_w, router_b, w_gate, w_up, w_down,
           ple_w, ple_gate_w, ln2_g, ln2_b):
    bsz, seq, d = x.shape
    t = bsz * seq
    h = x.reshape(t, d).astype(F32)
    hb = h.astype(BF16)
    for i in range(w_in.shape[0]):
        h, hb = _layer(h, hb, p[i].reshape(t, -1), w_in[i], conv_w[i], conv_b[i], dt_bias[i],
                       a_log[i], d_skip[i], ssm_norm_w[i], idx_k_norm_g[i], idx_k_norm_b[i],
                       w_out[i], ln1_g[i], ln1_b[i], router_w, router_b, w_gate[i], w_up[i],
                       w_down[i], ple_w[i], ple_gate_w[i], ln2_g[i], ln2_b[i], bsz, seq)
    return h.reshape(bsz, seq, d).astype(x.dtype)
```

```python
import functools

import jax
import jax.numpy as jnp
from jax import lax
from jax.experimental import pallas as pl
from jax.experimental.pallas import tpu as pltpu

F32 = jnp.float32
BF16 = jnp.bfloat16
I32 = jnp.int32

SSD_WIDTH = 1024
SSD_HEADDIM = 64
SSD_HEADS = 16
SSD_STATE = 128
SSD_GROUPS = 2
CONV_WIDTH = 4
CHUNK = 128
CONV_DIM = SSD_WIDTH + 2 * SSD_GROUPS * SSD_STATE
ATT_HEADS = 8
ATT_HEADDIM = 128
ATT_WIDTH = ATT_HEADS * ATT_HEADDIM
IDX_HEADS = 16
IDX_DIM = 64
TOPK_MAX = 256
Q_BLOCK = 128
N_EXPERTS = 32
N_EXPERT_GROUPS = 8
EXPERTS_PER_GROUP = 4
D_FF = 512
ALPHA = (2.0 * 2) ** 0.25
LN_EPS = 1e-5

OFF_Z = 0
OFF_XBC = OFF_Z + SSD_WIDTH
OFF_DT = OFF_XBC + CONV_DIM
OFF_Q = OFF_DT + SSD_HEADS
OFF_IQ = OFF_Q + 3 * ATT_WIDTH
OFF_IK = OFF_IQ + IDX_HEADS * IDX_DIM
OFF_IW = OFF_IK + IDX_DIM
SM_DT = 0
SM_IK = SSD_HEADS
SM_IW = SM_IK + IDX_DIM
SM_W = 128

V7X_LANES = 128
VMEM_LIMIT = 56 * 1024 * 1024
FLOAT_LOWEST = -3.0e38
SEARCH_PROBES_PER_CHECK = 3
SEARCH_MAX_ITERS = 200
KEY_CHUNK = 256
DSA_QB = 256
MOE_BLOCK = 256


def _sigmoid(x):
    return 1.0 / (1.0 + jnp.exp(-x))


def _softplus(x):
    return jnp.maximum(x, 0.0) + jnp.log(1.0 + jnp.exp(-jnp.abs(x)))


def _nt_dot(a, b, **kw):
    return lax.dot_general(a, b, (((1,), (1,)), ((), ())), preferred_element_type=F32, **kw)


def _layer_norm_rows(u, g, b):
    mu = jnp.mean(u, axis=-1, keepdims=True)
    d = u - mu
    var = jnp.mean(d * d, axis=-1, keepdims=True)
    return d * lax.rsqrt(var + LN_EPS) * g + b


def _mm_kernel(x_ref, w_ref, o_ref):
    o_ref[...] = jnp.dot(x_ref[...], w_ref[...], preferred_element_type=F32).astype(o_ref.dtype)


def _matmul(x, w, out_dtype, tm, tn):
    m, k = x.shape
    n = w.shape[1]
    return pl.pallas_call(
        _mm_kernel,
        name="in_proj",
        grid=(m // tm, n // tn),
        in_specs=[pl.BlockSpec((tm, k), lambda i, j: (i, 0)),
                  pl.BlockSpec((k, tn), lambda i, j: (0, j))],
        out_specs=pl.BlockSpec((tm, tn), lambda i, j: (i, j)),
        out_shape=jax.ShapeDtypeStruct((m, n), out_dtype),
        compiler_params=pltpu.CompilerParams(
            dimension_semantics=("parallel", "parallel"), vmem_limit_bytes=VMEM_LIMIT),
    )(x, w)


def _mm_nt_kernel(wt_ref, x_ref, o_ref):
    o_ref[...] = _nt_dot(wt_ref[...], x_ref[...]).astype(o_ref.dtype)


def _matmul_nt(wt, x, out_dtype, tn, tm):
    n, k = wt.shape
    m = x.shape[0]
    return pl.pallas_call(
        _mm_nt_kernel,
        name="in_proj_t",
        grid=(m // tm, n // tn),
        in_specs=[pl.BlockSpec((tn, k), lambda i, j: (j, 0)),
                  pl.BlockSpec((tm, k), lambda i, j: (i, 0))],
        out_specs=pl.BlockSpec((tn, tm), lambda i, j: (j, i)),
        out_shape=jax.ShapeDtypeStruct((n, m), out_dtype),
        compiler_params=pltpu.CompilerParams(
            dimension_semantics=("parallel", "parallel"), vmem_limit_bytes=VMEM_LIMIT),
    )(wt, x)


def _ssd_kernel(za_ref, sm_ref, cw_ref, cb_ref, dtb_ref, a_ref, dexp_ref, nw_ref, e_ref,
                o_ref, cbuf, state, ybuf):
    c = pl.program_id(1)
    L = CHUNK

    @pl.when(c == 0)
    def _():
        cbuf[0:8, :] = jnp.zeros((8, CONV_DIM), F32)
        state[...] = jnp.zeros_like(state)

    cbuf[8:8 + L, :] = za_ref[:, SSD_WIDTH:SSD_WIDTH + CONV_DIM]
    acc = cb_ref[...] + cw_ref[0:1, :] * cbuf[5:5 + L, :]
    acc = acc + cw_ref[1:2, :] * cbuf[6:6 + L, :]
    acc = acc + cw_ref[2:3, :] * cbuf[7:7 + L, :]
    acc = acc + cw_ref[3:4, :] * cbuf[8:8 + L, :]
    cbuf[0:8, :] = cbuf[L:L + 8, :]
    xbc = acc * _sigmoid(acc)
    xs = xbc[:, 0:SSD_WIDTH]
    bm = xbc[:, SSD_WIDTH:SSD_WIDTH + SSD_GROUPS * SSD_STATE]
    cm = xbc[:, SSD_WIDTH + SSD_GROUPS * SSD_STATE:CONV_DIM]

    dt = _softplus(sm_ref[...] + dtb_ref[...])
    da = dt * a_ref[...]
    row = lax.broadcasted_iota(I32, (L, L), 0)
    col = lax.broadcasted_iota(I32, (L, L), 1)
    causal = col <= row
    tri = jnp.where(causal, 1.0, 0.0).astype(F32)
    a_cs = jnp.dot(tri, da, preferred_element_type=F32, precision=lax.Precision.HIGHEST)
    a_cs_t = a_cs.T
    expa = jnp.exp(a_cs)
    dte = jnp.exp(a_cs[L - 1:L, :] - a_cs)
    stacked = jnp.concatenate([dt, expa, dte], axis=0)
    expd = jnp.dot(stacked, e_ref[...], preferred_element_type=F32,
                   precision=lax.Precision.HIGHEST)
    dt_x = expd[0:L, :]
    expa_x = expd[L:2 * L, :]
    dte_x = expd[2 * L:3 * L, :]
    xdt = xs * dt_x
    hpg = SSD_HEADS // SSD_GROUPS
    gw = hpg * SSD_HEADDIM
    for g in range(SSD_GROUPS):
        cm_g = cm[:, g * SSD_STATE:(g + 1) * SSD_STATE].astype(BF16)
        bm_gf = bm[:, g * SSD_STATE:(g + 1) * SSD_STATE]
        bm_g = bm_gf.astype(BF16)
        cb = _nt_dot(cm_g, bm_g)
        for hh in range(hpg):
            h = g * hpg + hh
            seg = a_cs[:, h:h + 1] - a_cs_t[h:h + 1, :]
            dec = jnp.exp(jnp.where(causal, seg, -jnp.inf))
            mh = (cb * dec).astype(BF16)
            xh = xdt[:, h * SSD_HEADDIM:(h + 1) * SSD_HEADDIM].astype(BF16)
            ybuf[:, h * SSD_HEADDIM:(h + 1) * SSD_HEADDIM] = jnp.dot(
                mh, xh, preferred_element_type=F32)
        st_g = state[:, g * gw:(g + 1) * gw]
        y_off = jnp.dot(cm_g, st_g.astype(BF16), preferred_element_type=F32)
        ybuf[:, g * gw:(g + 1) * gw] = (ybuf[:, g * gw:(g + 1) * gw]
                                        + y_off * expa_x[:, g * gw:(g + 1) * gw])
        xw = (xdt[:, g * gw:(g + 1) * gw] * dte_x[:, g * gw:(g + 1) * gw]).astype(BF16)
        upd = jnp.dot(bm_gf.T.astype(BF16), xw, preferred_element_type=F32)
        state[:, g * gw:(g + 1) * gw] = st_g * expa_x[L - 1:L, g * gw:(g + 1) * gw] + upd

    y = ybuf[...] + xs * dexp_ref[...]
    z = za_ref[:, 0:SSD_WIDTH]
    yg = y * (z * _sigmoid(z))
    ms = jnp.mean(yg * yg, axis=-1, keepdims=True)
    o_ref[...] = (yg * lax.rsqrt(ms + LN_EPS) * nw_ref[...]).astype(o_ref.dtype)


def _ssd(za, sm, conv_w, conv_b, dt_bias, a_log, d_skip, ssm_norm_w, bsz, seq):
    nc = seq // CHUNK
    t = bsz * seq
    pad = SM_W - SSD_HEADS
    dtb = jnp.pad(dt_bias.astype(F32), (0, pad)).reshape(1, SM_W)
    a = jnp.pad(-jnp.exp(a_log.astype(F32)), (0, pad)).reshape(1, SM_W)
    dexp = jnp.repeat(d_skip.astype(F32), SSD_HEADDIM).reshape(1, SSD_WIDTH)
    e = (jnp.arange(SSD_WIDTH)[None, :] // SSD_HEADDIM == jnp.arange(SM_W)[:, None]).astype(F32)
    const = lambda shape: pl.BlockSpec(shape, lambda b, c: (0, 0))
    return pl.pallas_call(
        _ssd_kernel,
        name="ssd",
        grid=(bsz, nc),
        in_specs=[pl.BlockSpec((CHUNK, SSD_WIDTH + CONV_DIM), lambda b, c: (b * nc + c, 0)),
                  pl.BlockSpec((CHUNK, SM_W), lambda b, c: (b * nc + c, 0)),
                  const((CONV_WIDTH, CONV_DIM)), const((1, CONV_DIM)), const((1, SM_W)),
                  const((1, SM_W)), const((1, SSD_WIDTH)), const((1, SSD_WIDTH)),
                  const((SM_W, SSD_WIDTH))],
        out_specs=pl.BlockSpec((CHUNK, SSD_WIDTH), lambda b, c: (b * nc + c, 0)),
        out_shape=jax.ShapeDtypeStruct((t, SSD_WIDTH), BF16),
        scratch_shapes=[pltpu.VMEM((CHUNK + 8, CONV_DIM), F32),
                        pltpu.VMEM((SSD_STATE, SSD_WIDTH), F32),
                        pltpu.VMEM((CHUNK, SSD_WIDTH), F32)],
        compiler_params=pltpu.CompilerParams(
            dimension_semantics=("parallel", "arbitrary"), vmem_limit_bytes=VMEM_LIMIT),
    )(za, sm, conv_w.astype(F32), conv_b.astype(F32).reshape(1, CONV_DIM), dtb, a, dexp,
      ssm_norm_w.astype(F32).reshape(1, SSD_WIDTH), e)


def _dsa_kernel(qt_ref, iqt_ref, vt_ref, k_ref, smb_ref, smt_ref, g_ref, b_ref, o_ref,
                ikn, keys, acc_s, s_s, p_s, bias_s, *, topk):
    qi = pl.program_id(1)
    QB = o_ref.shape[0]
    KC = KEY_CHUNK

    @pl.when(qi == 0)
    def _():
        ik = smb_ref[:, SM_IK:SM_IK + IDX_DIM]
        ikn[...] = _layer_norm_rows(ik, g_ref[...], b_ref[...]).astype(BF16)

    nkc = (qi * QB + QB - 1) // KC + 1
    w = smt_ref[SM_IW:SM_IW + IDX_HEADS, :] * (IDX_DIM ** -0.5 * IDX_HEADS ** -0.5)
    qpos = qi * QB + lax.broadcasted_iota(I32, (KC, QB), 1)
    kiota = lax.broadcasted_iota(I32, (KC, QB), 0)

    def score_chunk(c, carry):
        rmax, rmin = carry
        off = pl.multiple_of(c * KC, KC)
        kc = ikn[pl.ds(off, KC), :]
        acc = jnp.zeros((KC, QB), F32)
        for h in range(IDX_HEADS):
            d = jnp.dot(kc, iqt_ref[h * IDX_DIM:(h + 1) * IDX_DIM, :], preferred_element_type=F32)
            acc = acc + w[h:h + 1, :] * jnp.maximum(d, 0.0)
        vis = off + kiota <= qpos
        sv = jnp.where(vis, acc, -jnp.inf)
        keys[c] = sv
        rmax = jnp.maximum(rmax, jnp.max(sv, axis=0, keepdims=True))
        rmin = jnp.minimum(rmin, jnp.min(jnp.where(vis, acc, jnp.inf), axis=0, keepdims=True))
        return rmax, rmin

    rmax, rmin = lax.fori_loop(
        0, nkc, score_chunk,
        (jnp.full((1, QB), -jnp.inf, F32), jnp.full((1, QB), jnp.inf, F32)))

    kf = float(topk)

    def count_ge(mid):
        def count_chunk(c, acc):
            m = jnp.where(keys[c] >= mid, 1.0, 0.0)
            return acc + jnp.sum(m.reshape(KC // 8, 8, QB), axis=0)

        acc = lax.fori_loop(0, nkc, count_chunk, jnp.zeros((8, QB), F32))
        return jnp.sum(acc, axis=0, keepdims=True)

    nvis = (qi * QB + lax.broadcasted_iota(I32, (1, QB), 1) + 1).astype(F32)
    small = nvis <= kf

    def search_cond(st):
        return (st[0] > 0.0) & (st[1] < SEARCH_MAX_ITERS)

    def search_body(st):
        _, it, lo, hi, clo = st
        for _ in range(SEARCH_PROBES_PER_CHECK):
            mid = 0.5 * lo + 0.5 * hi
            splittable = (mid > lo) & (mid < hi)
            c = count_ge(mid)
            up = splittable & (c >= kf)
            down = splittable & (c < kf)
            lo = jnp.where(up, mid, lo)
            clo = jnp.where(up, c, clo)
            hi = jnp.where(down, mid, hi)
        active = jnp.logical_not(small) & splittable & (clo != kf)
        return jnp.sum(jnp.where(active, 1.0, 0.0)), it + 1, lo, hi, clo

    hi0 = rmax + jnp.abs(rmax) * 1e-6 + 1e-37
    st = lax.while_loop(search_cond, search_body,
                        (jnp.float32(1.0), jnp.int32(0), rmin, hi0, nvis))
    thr = jnp.where(small, FLOAT_LOWEST, st[2])

    scale = ATT_HEADDIM ** -0.5
    acc_s[...] = jnp.zeros_like(acc_s)

    def att_chunk(c, carry):
        ms, ls = carry
        off = pl.multiple_of(c * KC, KC)
        bias_s[...] = jnp.where(keys[c] >= thr, 0.0, -1e30)
        for h in range(ATT_HEADS):
            lo, hi = h * ATT_HEADDIM, (h + 1) * ATT_HEADDIM
            s_s[h] = jnp.dot(k_ref[pl.ds(off, KC), lo:hi], qt_ref[lo:hi, :],
                             preferred_element_type=F32)
        new_ms, new_ls, alphas = [], [], []
        for h in range(ATT_HEADS):
            s = s_s[h] * scale + bias_s[...]
            m_new = jnp.maximum(ms[h], jnp.max(s, axis=0, keepdims=True))
            alpha = jnp.exp(ms[h] - m_new)
            p = jnp.exp(s - m_new)
            new_ls.append(alpha * ls[h] + jnp.sum(p, axis=0, keepdims=True))
            new_ms.append(m_new)
            alphas.append(alpha)
            p_s[h] = p.astype(BF16)
        for h in range(ATT_HEADS):
            lo, hi = h * ATT_HEADDIM, (h + 1) * ATT_HEADDIM
            pv = jnp.dot(vt_ref[lo:hi, pl.ds(off, KC)], p_s[h],
                         preferred_element_type=F32)
            acc_s[h] = alphas[h] * acc_s[h] + pv
        return tuple(new_ms), tuple(new_ls)

    m0 = tuple(jnp.full((1, QB), -1e30, F32) for _ in range(ATT_HEADS))
    l0 = tuple(jnp.zeros((1, QB), F32) for _ in range(ATT_HEADS))
    _, ls = lax.fori_loop(0, nkc, att_chunk, (m0, l0))
    for h in range(ATT_HEADS):
        o_ref[:, h * ATT_HEADDIM:(h + 1) * ATT_HEADDIM] = (acc_s[h] / ls[h]).T.astype(o_ref.dtype)


def _dsa(qivt, kk, sm, smt, g, b, bsz, seq):
    qb = min(DSA_QB, seq)
    nq = seq // qb
    t = bsz * seq
    topk = min(TOPK_MAX, seq // 4)
    single = pl.Buffered(1)
    return pl.pallas_call(
        functools.partial(_dsa_kernel, topk=topk),
        name="dsa",
        grid=(bsz, nq),
        in_specs=[pl.BlockSpec((ATT_WIDTH, qb), lambda bi, qi: (0, bi * nq + qi)),
                  pl.BlockSpec((IDX_HEADS * IDX_DIM, qb), lambda bi, qi: (1, bi * nq + qi)),
                  pl.BlockSpec((ATT_WIDTH, seq), lambda bi, qi: (2, bi), pipeline_mode=single),
                  pl.BlockSpec((seq, ATT_WIDTH), lambda bi, qi: (bi, 0), pipeline_mode=single),
                  pl.BlockSpec((seq, SM_W), lambda bi, qi: (bi, 0), pipeline_mode=single),
                  pl.BlockSpec((SM_W, qb), lambda bi, qi: (0, bi * nq + qi)),
                  pl.BlockSpec((1, IDX_DIM), lambda bi, qi: (0, 0)),
                  pl.BlockSpec((1, IDX_DIM), lambda bi, qi: (0, 0))],
        out_specs=pl.BlockSpec((qb, ATT_WIDTH), lambda bi, qi: (bi * nq + qi, 0)),
        out_shape=jax.ShapeDtypeStruct((t, ATT_WIDTH), BF16),
        scratch_shapes=[pltpu.VMEM((seq, IDX_DIM), BF16),
                        pltpu.VMEM((seq // KEY_CHUNK, KEY_CHUNK, qb), F32),
                        pltpu.VMEM((ATT_HEADS, ATT_HEADDIM, qb), F32),
                        pltpu.VMEM((ATT_HEADS, KEY_CHUNK, qb), F32),
                        pltpu.VMEM((ATT_HEADS, KEY_CHUNK, qb), BF16),
                        pltpu.VMEM((KEY_CHUNK, qb), F32)],
        compiler_params=pltpu.CompilerParams(
            dimension_semantics=("parallel", "arbitrary"), vmem_limit_bytes=VMEM_LIMIT),
    )(qivt, qivt, qivt, kk, sm, smt, g.astype(F32).reshape(1, IDX_DIM),
      b.astype(F32).reshape(1, IDX_DIM))


def _route_rows(logits_t, rb):
    aff = _sigmoid(logits_t)
    biased = aff + rb
    rb_rows = [biased[r:r + 1, :] for r in range(N_EXPERTS)]
    ra_rows = [aff[r:r + 1, :] for r in range(N_EXPERTS)]
    epg = EXPERTS_PER_GROUP
    gs = []
    for g in range(N_EXPERT_GROUPS):
        a, b, c, d = rb_rows[epg * g:epg * g + epg]
        p, q = jnp.maximum(a, b), jnp.minimum(a, b)
        r, s = jnp.maximum(c, d), jnp.minimum(c, d)
        gs.append(jnp.maximum(p, r) + jnp.maximum(jnp.minimum(p, r), jnp.maximum(q, s)))
    best = gs[0]
    bidx = jnp.zeros_like(best, dtype=I32)
    for g in range(1, N_EXPERT_GROUPS):
        better = gs[g] > best
        best = jnp.where(better, gs[g], best)
        bidx = jnp.where(better, g, bidx)
    vb = [rb_rows[j] for j in range(epg)]
    va = [ra_rows[j] for j in range(epg)]
    for g in range(1, N_EXPERT_GROUPS):
        pick = bidx == g
        vb = [jnp.where(pick, rb_rows[epg * g + j], vb[j]) for j in range(epg)]
        va = [jnp.where(pick, ra_rows[epg * g + j], va[j]) for j in range(epg)]
    t1, a1, i1 = vb[0], va[0], jnp.zeros_like(bidx)
    for j in range(1, epg):
        better = vb[j] > t1
        t1 = jnp.where(better, vb[j], t1)
        a1 = jnp.where(better, va[j], a1)
        i1 = jnp.where(better, j, i1)
    t2 = jnp.full_like(t1, -jnp.inf)
    a2 = jnp.zeros_like(a1)
    i2 = jnp.zeros_like(i1)
    for j in range(epg):
        vj = jnp.where(i1 == j, -jnp.inf, vb[j])
        better = vj > t2
        t2 = jnp.where(better, vj, t2)
        a2 = jnp.where(better, va[j], a2)
        i2 = jnp.where(better, j, i2)
    den = a1 + a2
    return bidx * epg + i1, bidx * epg + i2, a1 / den, a2 / den


def _outproj_kernel(ys_ref, ya_ref, w1_ref, w2_ref, h_ref, g_ref, b_ref, rwt_ref, rb_ref,
                    h1_ref, h1b_ref, eidx_ref, gate_ref):
    mix = jnp.dot(ys_ref[...], w1_ref[...], preferred_element_type=F32)
    mix = mix + jnp.dot(ya_ref[...], w2_ref[...], preferred_element_type=F32)
    h1 = _layer_norm_rows(ALPHA * h_ref[...] + mix, g_ref[...], b_ref[...])
    h1_ref[...] = h1
    h1b_ref[...] = h1.astype(BF16)
    logits_t = _nt_dot(rwt_ref[...], h1, precision=lax.Precision.HIGHEST)
    e1, e2, g1, g2 = _route_rows(logits_t, rb_ref[...])
    eidx_ref[0:1, :] = e1
    eidx_ref[1:2, :] = e2
    gate_ref[0:1, :] = g1
    gate_ref[1:2, :] = g2


def _outproj_ln_router(y_ssd, y_att, w_out, h, ln_g, ln_b, router_w, router_b, tm):
    t, d = h.shape
    w = w_out.astype(BF16)
    const = lambda shape: pl.BlockSpec(shape, lambda i: (0, 0))
    return pl.pallas_call(
        _outproj_kernel,
        name="outproj_ln_router",
        grid=(t // tm,),
        in_specs=[pl.BlockSpec((tm, SSD_WIDTH), lambda i: (i, 0)),
                  pl.BlockSpec((tm, ATT_WIDTH), lambda i: (i, 0)),
                  pl.BlockSpec((SSD_WIDTH, d), lambda i: (0, 0)),
                  pl.BlockSpec((ATT_WIDTH, d), lambda i: (1, 0)),
                  pl.BlockSpec((tm, d), lambda i: (i, 0)),
                  const((1, d)), const((1, d)), const((N_EXPERTS, d)), const((N_EXPERTS, 1))],
        out_specs=[pl.BlockSpec((tm, d), lambda i: (i, 0)),
                   pl.BlockSpec((tm, d), lambda i: (i, 0)),
                   pl.BlockSpec((2, tm), lambda i: (0, i)),
                   pl.BlockSpec((2, tm), lambda i: (0, i))],
        out_shape=[jax.ShapeDtypeStruct((t, d), F32), jax.ShapeDtypeStruct((t, d), BF16),
                   jax.ShapeDtypeStruct((2, t), I32), jax.ShapeDtypeStruct((2, t), F32)],
        compiler_params=pltpu.CompilerParams(
            dimension_semantics=("parallel",), vmem_limit_bytes=VMEM_LIMIT),
    )(y_ssd, y_att, w, w, h, ln_g.astype(F32).reshape(1, d), ln_b.astype(F32).reshape(1, d),
      router_w.astype(F32).T, router_b.astype(F32).reshape(N_EXPERTS, 1))


def _dispatch_kernel(dest_ref, h_ref, xs_in_hbm, xs_hbm, sem, *, tb, t):
    del xs_in_hbm
    base = pl.program_id(0) * tb

    def row_copy(r, slot):
        return pltpu.make_async_copy(h_ref.at[pl.ds(r, 1), :], xs_hbm.at[pl.ds(slot, 1), :], sem)

    def start(r, carry):
        for k in range(2):
            row_copy(r, dest_ref[k * t + base + r]).start()
        return carry

    def wait(r, carry):
        for k in range(2):
            row_copy(0, 0).wait()
        return carry

    lax.fori_loop(0, tb, start, 0)
    lax.fori_loop(0, tb, wait, 0)


def _dispatch(dest, h1, cap, tb):
    t, d = h1.shape
    xs0 = jnp.zeros((cap, d), h1.dtype)
    return pl.pallas_call(
        functools.partial(_dispatch_kernel, tb=tb, t=t),
        name="moe_dispatch",
        grid_spec=pltpu.PrefetchScalarGridSpec(
            num_scalar_prefetch=1,
            grid=(t // tb,),
            in_specs=[pl.BlockSpec((tb, d), lambda i, dest: (i, 0)),
                      pl.BlockSpec(memory_space=pl.ANY)],
            out_specs=pl.BlockSpec(memory_space=pl.ANY),
            scratch_shapes=[pltpu.SemaphoreType.DMA(())]),
        out_shape=jax.ShapeDtypeStruct((cap, d), h1.dtype),
        input_output_aliases={2: 0},
        compiler_params=pltpu.CompilerParams(
            dimension_semantics=("arbitrary",), has_side_effects=True),
    )(dest, h1, xs0)


def _expert_kernel(be_ref, nu_ref, xs_ref, wg_ref, wu_ref, wd_ref, o_ref, wg_s, wu_s, wd_s):
    i = pl.program_id(0)

    @pl.when(i < nu_ref[0])
    def _():
        prev = be_ref[jnp.maximum(i - 1, 0)]

        @pl.when((i == 0) | (be_ref[i] != prev))
        def _():
            wg_s[...] = wg_ref[...].astype(BF16)
            wu_s[...] = wu_ref[...].astype(BF16)
            wd_s[...] = wd_ref[...].astype(BF16)

        x = xs_ref[...].astype(BF16)
        g = jnp.dot(x, wg_s[...], preferred_element_type=F32)
        u = jnp.dot(x, wu_s[...], preferred_element_type=F32)
        a = (g * _sigmoid(g) * u).astype(BF16)
        o_ref[...] = jnp.dot(a, wd_s[...], preferred_element_type=F32)

    @pl.when(i >= nu_ref[0])
    def _():
        o_ref[...] = jnp.zeros_like(o_ref)


def _experts(block_expert, n_used, xs, w_gate, w_up, w_down):
    cap, d = xs.shape
    nb = cap // MOE_BLOCK
    blk = lambda i, be, nu: (jnp.minimum(i, nu[0] - 1), 0)
    oblk = lambda i, be, nu: (i, 0)
    return pl.pallas_call(
        _expert_kernel,
        name="moe_experts",
        grid_spec=pltpu.PrefetchScalarGridSpec(
            num_scalar_prefetch=2,
            grid=(nb,),
            in_specs=[pl.BlockSpec((MOE_BLOCK, d), blk),
                      pl.BlockSpec((None, d, D_FF), lambda i, be, nu: (be[i], 0, 0)),
                      pl.BlockSpec((None, d, D_FF), lambda i, be, nu: (be[i], 0, 0)),
                      pl.BlockSpec((None, D_FF, d), lambda i, be, nu: (be[i], 0, 0))],
            out_specs=pl.BlockSpec((MOE_BLOCK, d), oblk),
            scratch_shapes=[pltpu.VMEM((d, D_FF), BF16), pltpu.VMEM((d, D_FF), BF16),
                            pltpu.VMEM((D_FF, d), BF16)]),
        out_shape=jax.ShapeDtypeStruct((cap, d), F32),
        compiler_params=pltpu.CompilerParams(
            dimension_semantics=("arbitrary",), vmem_limit_bytes=VMEM_LIMIT),
    )(block_expert, n_used, xs, w_gate, w_up, w_down)


def _combine_kernel(dest_ref, yb_hbm, gt_ref, o_ref, buf, sem, *, tc, t):
    base = pl.program_id(0) * tc

    def row_copy(slot, k, r):
        return pltpu.make_async_copy(yb_hbm.at[pl.ds(slot, 1), :], buf.at[k, pl.ds(r, 1), :], sem)

    def start(r, carry):
        for k in range(2):
            row_copy(dest_ref[k * t + base + r], k, r).start()
        return carry

    def wait(r, carry):
        for k in range(2):
            row_copy(0, k, r).wait()
        return carry

    lax.fori_loop(0, tc, start, 0)
    lax.fori_loop(0, tc, wait, 0)
    o_ref[...] = gt_ref[:, 0:1] * buf[0] + gt_ref[:, 1:2] * buf[1]


def _combine(dest, yb, gates_t, tc):
    t = gates_t.shape[0]
    d = yb.shape[1]
    return pl.pallas_call(
        functools.partial(_combine_kernel, tc=tc, t=t),
        name="moe_combine",
        grid_spec=pltpu.PrefetchScalarGridSpec(
            num_scalar_prefetch=1,
            grid=(t // tc,),
            in_specs=[pl.BlockSpec(memory_space=pl.ANY),
                      pl.BlockSpec((tc, 2), lambda i, dest: (i, 0))],
            out_specs=pl.BlockSpec((tc, d), lambda i, dest: (i, 0)),
            scratch_shapes=[pltpu.VMEM((2, tc, d), F32), pltpu.SemaphoreType.DMA(())]),
        out_shape=jax.ShapeDtypeStruct((t, d), F32),
        compiler_params=pltpu.CompilerParams(
            dimension_semantics=("arbitrary",), vmem_limit_bytes=VMEM_LIMIT),
    )(dest, yb, gates_t)


def _moe_plan(eidx, t):
    flat_e = eidx.reshape(2 * t)
    onehot = (flat_e[:, None] == jnp.arange(N_EXPERTS, dtype=I32)[None, :]).astype(I32)
    cs = jnp.cumsum(onehot, axis=0)
    rank = jnp.sum(onehot * cs, axis=1) - 1
    counts = cs[-1]
    padded = (counts + MOE_BLOCK - 1) // MOE_BLOCK * MOE_BLOCK
    pad_end = jnp.cumsum(padded)
    pad_start = pad_end - padded
    dest = (pad_start[flat_e] + rank).astype(I32)
    cap = 2 * t + N_EXPERTS * MOE_BLOCK
    nb = cap // MOE_BLOCK
    block_start = jnp.arange(nb, dtype=I32) * MOE_BLOCK
    block_expert = jnp.minimum(
        jnp.sum((pad_end[None, :] <= block_start[:, None]).astype(I32), axis=1), N_EXPERTS - 1)
    n_used = (pad_end[-1] // MOE_BLOCK).astype(I32).reshape(1)
    return dest, block_expert, n_used, cap


def _ple_kernel(h1_ref, h1b_ref, ffn_ref, p_ref, wg_ref, wp_ref, g_ref, b_ref, o_ref, ob_ref):
    gate = _sigmoid(jnp.dot(h1b_ref[...], wg_ref[...], preferred_element_type=F32))
    pe = jnp.dot(p_ref[...].astype(BF16), wp_ref[...], preferred_element_type=F32)
    u = ALPHA * h1_ref[...] + ffn_ref[...] + gate * pe
    h2 = _layer_norm_rows(u, g_ref[...], b_ref[...])
    o_ref[...] = h2
    ob_ref[...] = h2.astype(BF16)


def _ple_ln(h1, h1b, ffn, p, ple_gate_w, ple_w, ln_g, ln_b, tm):
    t, d = h1.shape
    pd = p.shape[1]
    row = lambda w: pl.BlockSpec((tm, w), lambda i: (i, 0))
    const = lambda shape: pl.BlockSpec(shape, lambda i: (0, 0))
    return pl.pallas_call(
        _ple_kernel,
        name="ple_ln",
        grid=(t // tm,),
        in_specs=[row(d), row(d), row(d), row(pd), const((d, d)), const((pd, d)),
                  const((1, d)), const((1, d))],
        out_specs=[row(d), row(d)],
        out_shape=[jax.ShapeDtypeStruct((t, d), F32), jax.ShapeDtypeStruct((t, d), BF16)],
        compiler_params=pltpu.CompilerParams(
            dimension_semantics=("parallel",), vmem_limit_bytes=VMEM_LIMIT),
    )(h1, h1b, ffn, p, ple_gate_w.astype(BF16), ple_w.astype(BF16),
      ln_g.astype(F32).reshape(1, d), ln_b.astype(F32).reshape(1, d))


def _layer(h, hb, p_i, w_in, conv_w, conv_b, dt_bias, a_log, d_skip, ssm_norm_w,
           idx_k_norm_g, idx_k_norm_b, w_out, ln1_g, ln1_b, router_w, router_b,
           w_gate, w_up, w_down, ple_w, ple_gate_w, ln2_g, ln2_b, bsz, seq):
    t = bsz * seq
    tm = min(1024, t)
    w_a = w_in[:, OFF_Z:OFF_DT].astype(BF16)
    w_k = w_in[:, OFF_Q + ATT_WIDTH:OFF_Q + 2 * ATT_WIDTH].astype(BF16)
    w_c = jnp.concatenate(
        [w_in[:, OFF_DT:OFF_Q], w_in[:, OFF_IK:],
         jnp.zeros((w_in.shape[0], SM_W - SSD_HEADS - IDX_DIM - IDX_HEADS), w_in.dtype)],
        axis=1).astype(BF16)
    w_t = jnp.concatenate(
        [w_in[:, OFF_Q:OFF_Q + ATT_WIDTH], w_in[:, OFF_IQ:OFF_IK],
         w_in[:, OFF_Q + 2 * ATT_WIDTH:OFF_IQ]], axis=1).astype(BF16).T
    za = _matmul(hb, w_a, F32, tm, 512)
    kk = _matmul(hb, w_k, BF16, tm, 512)
    sm = _matmul(hb, w_c, F32, tm, SM_W)
    qivt = _matmul_nt(w_t, hb, BF16, 512, tm)
    smt = _matmul_nt(w_c.T, hb, F32, SM_W, tm)

    y_ssd = _ssd(za, sm, conv_w, conv_b, dt_bias, a_log, d_skip, ssm_norm_w, bsz, seq)
    y_att = _dsa(qivt, kk, sm, smt, idx_k_norm_g, idx_k_norm_b, bsz, seq)
    h1, h1b, eidx, gates = _outproj_ln_router(
        y_ssd, y_att, w_out, h, ln1_g, ln1_b, router_w, router_b, min(256, t))

    dest, block_expert, n_used, cap = _moe_plan(eidx, t)
    xs = _dispatch(dest, h1, cap, min(512, t))
    yb = _experts(block_expert, n_used, xs, w_gate, w_up, w_down)
    ffn = _combine(dest, yb, gates.T, min(256, t))
    return _ple_ln(h1, h1b, ffn, p_i, ple_gate_w, ple_w, ln2_g, ln2_b, min(256, t))


def kernel(x, p, w_in, conv_w, conv_b, dt_bias, a_log, d_skip, ssm_norm_w, idx_k_norm_g,
           idx_k_norm_b, w_out, ln1_g, ln1_b, router_w, router_b, w_gate, w_up, w_down,
           ple_w, ple_gate_w, ln2_g, ln2_b):
    bsz, seq, d = x.shape
    t = bsz * seq
    h = x.reshape(t, d).astype(F32)
    hb = h.astype(BF16)
    for i in range(w_in.shape[0]):
        h, hb = _layer(h, hb, p[i].reshape(t, -1), w_in[i], conv_w[i], conv_b[i], dt_bias[i],
                       a_log[i], d_skip[i], ssm_norm_w[i], idx_k_norm_g[i], idx_k_norm_b[i],
                       w_out[i], ln1_g[i], ln1_b[i], router_w, router_b, w_gate[i], w_up[i],
                       w_down[i], ple_w[i], ple_gate_w[i], ln2_g[i], ln2_b[i], bsz, seq)
    return h.reshape(bsz, seq, d).astype(x.dtype)
```

```python
import functools

import jax
import jax.numpy as jnp
from jax import lax
from jax.experimental import pallas as pl
from jax.experimental.pallas import tpu as pltpu

F32 = jnp.float32
BF16 = jnp.bfloat16
I32 = jnp.int32

SSD_WIDTH = 1024
SSD_HEADDIM = 64
SSD_HEADS = 16
SSD_STATE = 128
SSD_GROUPS = 2
CONV_WIDTH = 4
CHUNK = 128
CONV_DIM = SSD_WIDTH + 2 * SSD_GROUPS * SSD_STATE
ATT_HEADS = 8
ATT_HEADDIM = 128
ATT_WIDTH = ATT_HEADS * ATT_HEADDIM
IDX_HEADS = 16
IDX_DIM = 64
TOPK_MAX = 256
Q_BLOCK = 128
N_EXPERTS = 32
N_EXPERT_GROUPS = 8
EXPERTS_PER_GROUP = 4
D_FF = 512
ALPHA = (2.0 * 2) ** 0.25
LN_EPS = 1e-5

OFF_Z = 0
OFF_XBC = OFF_Z + SSD_WIDTH
OFF_DT = OFF_XBC + CONV_DIM
OFF_Q = OFF_DT + SSD_HEADS
OFF_IQ = OFF_Q + 3 * ATT_WIDTH
OFF_IK = OFF_IQ + IDX_HEADS * IDX_DIM
OFF_IW = OFF_IK + IDX_DIM
SM_DT = 0
SM_IK = SSD_HEADS
SM_IW = SM_IK + IDX_DIM
SM_W = 128

V7X_LANES = 128
VMEM_LIMIT = 56 * 1024 * 1024
FLOAT_LOWEST = -3.0e38
SEARCH_PROBES_PER_CHECK = 3
SEARCH_MAX_ITERS = 200
KEY_CHUNK = 256
DSA_QB = 256
MOE_BLOCK = 256


def _sigmoid(x):
    return 1.0 / (1.0 + jnp.exp(-x))


def _softplus(x):
    return jnp.maximum(x, 0.0) + jnp.log(1.0 + jnp.exp(-jnp.abs(x)))


def _nt_dot(a, b, **kw):
    return lax.dot_general(a, b, (((1,), (1,)), ((), ())), preferred_element_type=F32, **kw)


def _layer_norm_rows(u, g, b):
    mu = jnp.mean(u, axis=-1, keepdims=True)
    d = u - mu
    var = jnp.mean(d * d, axis=-1, keepdims=True)
    return d * lax.rsqrt(var + LN_EPS) * g + b


PROJ_ROWS = 256


def _proj_kernel(x_ref, wm_ref, *rest, shift, transposed):
    if shift:
        we_ref, o_ref, w_s = rest
    else:
        o_ref, w_s = rest
    k, tn = wm_ref.shape

    @pl.when(pl.program_id(1) == 0)
    def _():
        def stage(r, carry):
            r0 = pl.multiple_of(r * PROJ_ROWS, PROJ_ROWS)
            wm = wm_ref[pl.ds(r0, PROJ_ROWS), :]
            if shift:
                wide = jnp.concatenate([wm, we_ref[pl.ds(r0, PROJ_ROWS), :]], axis=1)
                wm = wide[:, shift:shift + tn]
            if transposed:
                w_s[:, pl.ds(r0, PROJ_ROWS)] = wm.T.astype(BF16)
            else:
                w_s[pl.ds(r0, PROJ_ROWS), :] = wm.astype(BF16)
            return carry

        lax.fori_loop(0, k // PROJ_ROWS, stage, 0)

    if transposed:
        o_ref[...] = _nt_dot(w_s[...], x_ref[...]).astype(o_ref.dtype)
    else:
        o_ref[...] = jnp.dot(x_ref[...], w_s[...], preferred_element_type=F32).astype(o_ref.dtype)


def _proj(x, w_in, layer, c0, width, out_dtype, transposed, tm, tn=512):
    m, k = x.shape
    shift = c0 % V7X_LANES
    base = c0 - shift
    assert base % tn == 0 and width % tn == 0 and m % tm == 0
    b0 = base // tn
    in_specs = [pl.BlockSpec((tm, k), lambda j, i: (i, 0)),
                pl.BlockSpec((None, k, tn), lambda j, i: (layer, 0, b0 + j))]
    args = [x, w_in]
    if shift:
        in_specs.append(pl.BlockSpec(
            (None, k, V7X_LANES), lambda j, i: (layer, 0, (base + (j + 1) * tn) // V7X_LANES)))
        args.append(w_in)
    if transposed:
        out_spec = pl.BlockSpec((tn, tm), lambda j, i: (j, i))
        out_shape = jax.ShapeDtypeStruct((width, m), out_dtype)
        scratch = pltpu.VMEM((tn, k), BF16)
    else:
        out_spec = pl.BlockSpec((tm, tn), lambda j, i: (i, j))
        out_shape = jax.ShapeDtypeStruct((m, width), out_dtype)
        scratch = pltpu.VMEM((k, tn), BF16)
    return pl.pallas_call(
        functools.partial(_proj_kernel, shift=shift, transposed=transposed),
        name="in_proj_t" if transposed else "in_proj",
        grid=(width // tn, m // tm),
        in_specs=in_specs,
        out_specs=out_spec,
        out_shape=out_shape,
        scratch_shapes=[scratch],
        compiler_params=pltpu.CompilerParams(
            dimension_semantics=("parallel", "arbitrary"), vmem_limit_bytes=VMEM_LIMIT),
    )(*args)


def _proj_small_kernel(x_ref, wa_ref, wb_ref, o_ref, ot_ref, w_s, wt_s):
    @pl.when(pl.program_id(0) == 0)
    def _():
        lane = lax.broadcasted_iota(I32, wa_ref.shape, 1)
        w = jnp.where(lane < SM_IK, wa_ref[...],
                      jnp.where(lane < SM_IW + IDX_HEADS, wb_ref[...], 0.0))
        w_s[...] = w.astype(BF16)
        wt_s[...] = w.T.astype(BF16)

    o_ref[...] = jnp.dot(x_ref[...], w_s[...], preferred_element_type=F32)
    ot_ref[...] = _nt_dot(wt_s[...], x_ref[...])


def _proj_small(x, w_in, layer, tm):
    m, k = x.shape
    assert OFF_DT % V7X_LANES == 0 and (OFF_IK - SM_IK) % V7X_LANES == 0 and OFF_IW - OFF_IK == IDX_DIM
    ba, bb = OFF_DT // V7X_LANES, (OFF_IK - SM_IK) // V7X_LANES
    return pl.pallas_call(
        _proj_small_kernel,
        name="in_proj_small",
        grid=(m // tm,),
        in_specs=[pl.BlockSpec((tm, k), lambda i: (i, 0)),
                  pl.BlockSpec((None, k, SM_W), lambda i: (layer, 0, ba)),
                  pl.BlockSpec((None, k, SM_W), lambda i: (layer, 0, bb))],
        out_specs=[pl.BlockSpec((tm, SM_W), lambda i: (i, 0)),
                   pl.BlockSpec((SM_W, tm), lambda i: (0, i))],
        out_shape=[jax.ShapeDtypeStruct((m, SM_W), F32), jax.ShapeDtypeStruct((SM_W, m), F32)],
        scratch_shapes=[pltpu.VMEM((k, SM_W), BF16), pltpu.VMEM((SM_W, k), BF16)],
        compiler_params=pltpu.CompilerParams(
            dimension_semantics=("arbitrary",), vmem_limit_bytes=VMEM_LIMIT),
    )(x, w_in, w_in)


def _ssd_kernel(za_ref, sm_ref, cw_ref, cb_ref, dtb_ref, a_ref, dexp_ref, nw_ref, e_ref,
                o_ref, cbuf, state, ybuf):
    c = pl.program_id(1)
    L = CHUNK

    @pl.when(c == 0)
    def _():
        cbuf[0:8, :] = jnp.zeros((8, CONV_DIM), F32)
        state[...] = jnp.zeros_like(state)

    cbuf[8:8 + L, :] = za_ref[:, SSD_WIDTH:SSD_WIDTH + CONV_DIM]
    acc = cb_ref[...] + cw_ref[0:1, :] * cbuf[5:5 + L, :]
    acc = acc + cw_ref[1:2, :] * cbuf[6:6 + L, :]
    acc = acc + cw_ref[2:3, :] * cbuf[7:7 + L, :]
    acc = acc + cw_ref[3:4, :] * cbuf[8:8 + L, :]
    cbuf[0:8, :] = cbuf[L:L + 8, :]
    xbc = acc * _sigmoid(acc)
    xs = xbc[:, 0:SSD_WIDTH]
    bm = xbc[:, SSD_WIDTH:SSD_WIDTH + SSD_GROUPS * SSD_STATE]
    cm = xbc[:, SSD_WIDTH + SSD_GROUPS * SSD_STATE:CONV_DIM]

    dt = _softplus(sm_ref[...] + dtb_ref[...])
    da = dt * a_ref[...]
    row = lax.broadcasted_iota(I32, (L, L), 0)
    col = lax.broadcasted_iota(I32, (L, L), 1)
    causal = col <= row
    tri = jnp.where(causal, 1.0, 0.0).astype(F32)
    a_cs = jnp.dot(tri, da, preferred_element_type=F32, precision=lax.Precision.HIGHEST)
    a_cs_t = a_cs.T
    expa = jnp.exp(a_cs)
    dte = jnp.exp(a_cs[L - 1:L, :] - a_cs)
    stacked = jnp.concatenate([dt, expa, dte], axis=0)
    expd = jnp.dot(stacked, e_ref[...], preferred_element_type=F32,
                   precision=lax.Precision.HIGHEST)
    dt_x = expd[0:L, :]
    expa_x = expd[L:2 * L, :]
    dte_x = expd[2 * L:3 * L, :]
    xdt = xs * dt_x
    hpg = SSD_HEADS // SSD_GROUPS
    gw = hpg * SSD_HEADDIM
    for g in range(SSD_GROUPS):
        cm_g = cm[:, g * SSD_STATE:(g + 1) * SSD_STATE].astype(BF16)
        bm_gf = bm[:, g * SSD_STATE:(g + 1) * SSD_STATE]
        bm_g = bm_gf.astype(BF16)
        cb = _nt_dot(cm_g, bm_g)
        for hh in range(hpg):
            h = g * hpg + hh
            seg = a_cs[:, h:h + 1] - a_cs_t[h:h + 1, :]
            dec = jnp.exp(jnp.where(causal, seg, -jnp.inf))
            mh = (cb * dec).astype(BF16)
            xh = xdt[:, h * SSD_HEADDIM:(h + 1) * SSD_HEADDIM].astype(BF16)
            ybuf[:, h * SSD_HEADDIM:(h + 1) * SSD_HEADDIM] = jnp.dot(
                mh, xh, preferred_element_type=F32)
        st_g = state[:, g * gw:(g + 1) * gw]
        y_off = jnp.dot(cm_g, st_g.astype(BF16), preferred_element_type=F32)
        ybuf[:, g * gw:(g + 1) * gw] = (ybuf[:, g * gw:(g + 1) * gw]
                                        + y_off * expa_x[:, g * gw:(g + 1) * gw])
        xw = (xdt[:, g * gw:(g + 1) * gw] * dte_x[:, g * gw:(g + 1) * gw]).astype(BF16)
        upd = jnp.dot(bm_gf.T.astype(BF16), xw, preferred_element_type=F32)
        state[:, g * gw:(g + 1) * gw] = st_g * expa_x[L - 1:L, g * gw:(g + 1) * gw] + upd

    y = ybuf[...] + xs * dexp_ref[...]
    z = za_ref[:, 0:SSD_WIDTH]
    yg = y * (z * _sigmoid(z))
    ms = jnp.mean(yg * yg, axis=-1, keepdims=True)
    o_ref[...] = (yg * lax.rsqrt(ms + LN_EPS) * nw_ref[...]).astype(o_ref.dtype)


def _ssd(za, sm, conv_w, conv_b, dt_bias, a_log, d_skip, ssm_norm_w, bsz, seq):
    nc = seq // CHUNK
    t = bsz * seq
    pad = SM_W - SSD_HEADS
    dtb = jnp.pad(dt_bias.astype(F32), (0, pad)).reshape(1, SM_W)
    a = jnp.pad(-jnp.exp(a_log.astype(F32)), (0, pad)).reshape(1, SM_W)
    dexp = jnp.repeat(d_skip.astype(F32), SSD_HEADDIM).reshape(1, SSD_WIDTH)
    e = (jnp.arange(SSD_WIDTH)[None, :] // SSD_HEADDIM == jnp.arange(SM_W)[:, None]).astype(F32)
    const = lambda shape: pl.BlockSpec(shape, lambda b, c: (0, 0))
    return pl.pallas_call(
        _ssd_kernel,
        name="ssd",
        grid=(bsz, nc),
        in_specs=[pl.BlockSpec((CHUNK, SSD_WIDTH + CONV_DIM), lambda b, c: (b * nc + c, 0)),
                  pl.BlockSpec((CHUNK, SM_W), lambda b, c: (b * nc + c, 0)),
                  const((CONV_WIDTH, CONV_DIM)), const((1, CONV_DIM)), const((1, SM_W)),
                  const((1, SM_W)), const((1, SSD_WIDTH)), const((1, SSD_WIDTH)),
                  const((SM_W, SSD_WIDTH))],
        out_specs=pl.BlockSpec((CHUNK, SSD_WIDTH), lambda b, c: (b * nc + c, 0)),
        out_shape=jax.ShapeDtypeStruct((t, SSD_WIDTH), BF16),
        scratch_shapes=[pltpu.VMEM((CHUNK + 8, CONV_DIM), F32),
                        pltpu.VMEM((SSD_STATE, SSD_WIDTH), F32),
                        pltpu.VMEM((CHUNK, SSD_WIDTH), F32)],
        compiler_params=pltpu.CompilerParams(
            dimension_semantics=("parallel", "arbitrary"), vmem_limit_bytes=VMEM_LIMIT),
    )(za, sm, conv_w.astype(F32), conv_b.astype(F32).reshape(1, CONV_DIM), dtb, a, dexp,
      ssm_norm_w.astype(F32).reshape(1, SSD_WIDTH), e)


def _dsa_kernel(qt_ref, iqt_ref, vt_ref, k_ref, smb_ref, smt_ref, g_ref, b_ref, o_ref,
                ikn, keys, acc_s, s_s, p_s, bias_s, *, topk):
    qi = pl.program_id(1)
    QB = o_ref.shape[0]
    KC = KEY_CHUNK

    @pl.when(qi == 0)
    def _():
        ik = smb_ref[:, SM_IK:SM_IK + IDX_DIM]
        ikn[...] = _layer_norm_rows(ik, g_ref[...], b_ref[...]).astype(BF16)

    nkc = (qi * QB + QB - 1) // KC + 1
    w = smt_ref[SM_IW:SM_IW + IDX_HEADS, :] * (IDX_DIM ** -0.5 * IDX_HEADS ** -0.5)
    qpos = qi * QB + lax.broadcasted_iota(I32, (KC, QB), 1)
    kiota = lax.broadcasted_iota(I32, (KC, QB), 0)

    def score_chunk(c, carry):
        rmax, rmin = carry
        off = pl.multiple_of(c * KC, KC)
        kc = ikn[pl.ds(off, KC), :]
        acc = jnp.zeros((KC, QB), F32)
        for h in range(IDX_HEADS):
            d = jnp.dot(kc, iqt_ref[h * IDX_DIM:(h + 1) * IDX_DIM, :], preferred_element_type=F32)
            acc = acc + w[h:h + 1, :] * jnp.maximum(d, 0.0)
        vis = off + kiota <= qpos
        sv = jnp.where(vis, acc, -jnp.inf)
        keys[c] = sv
        rmax = jnp.maximum(rmax, jnp.max(sv, axis=0, keepdims=True))
        rmin = jnp.minimum(rmin, jnp.min(jnp.where(vis, acc, jnp.inf), axis=0, keepdims=True))
        return rmax, rmin

    rmax, rmin = lax.fori_loop(
        0, nkc, score_chunk,
        (jnp.full((1, QB), -jnp.inf, F32), jnp.full((1, QB), jnp.inf, F32)))

    kf = float(topk)

    def count_ge(mid):
        def count_chunk(c, acc):
            m = jnp.where(keys[c] >= mid, 1.0, 0.0)
            return acc + jnp.sum(m.reshape(KC // 8, 8, QB), axis=0)

        acc = lax.fori_loop(0, nkc, count_chunk, jnp.zeros((8, QB), F32))
        return jnp.sum(acc, axis=0, keepdims=True)

    nvis = (qi * QB + lax.broadcasted_iota(I32, (1, QB), 1) + 1).astype(F32)
    small = nvis <= kf

    def search_cond(st):
        return (st[0] > 0.0) & (st[1] < SEARCH_MAX_ITERS)

    def search_body(st):
        _, it, lo, hi, clo = st
        for _ in range(SEARCH_PROBES_PER_CHECK):
            mid = 0.5 * lo + 0.5 * hi
            splittable = (mid > lo) & (mid < hi)
            c = count_ge(mid)
            up = splittable & (c >= kf)
            down = splittable & (c < kf)
            lo = jnp.where(up, mid, lo)
            clo = jnp.where(up, c, clo)
            hi = jnp.where(down, mid, hi)
        active = jnp.logical_not(small) & splittable & (clo != kf)
        return jnp.sum(jnp.where(active, 1.0, 0.0)), it + 1, lo, hi, clo

    hi0 = rmax + jnp.abs(rmax) * 1e-6 + 1e-37
    st = lax.while_loop(search_cond, search_body,
                        (jnp.float32(1.0), jnp.int32(0), rmin, hi0, nvis))
    thr = jnp.where(small, FLOAT_LOWEST, st[2])

    scale = ATT_HEADDIM ** -0.5
    acc_s[...] = jnp.zeros_like(acc_s)

    def att_chunk(c, carry):
        ms, ls = carry
        off = pl.multiple_of(c * KC, KC)
        bias_s[...] = jnp.where(keys[c] >= thr, 0.0, -1e30)
        for h in range(ATT_HEADS):
            lo, hi = h * ATT_HEADDIM, (h + 1) * ATT_HEADDIM
            s_s[h] = jnp.dot(k_ref[pl.ds(off, KC), lo:hi], qt_ref[lo:hi, :],
                             preferred_element_type=F32)
        new_ms, new_ls, alphas = [], [], []
        for h in range(ATT_HEADS):
            s = s_s[h] * scale + bias_s[...]
            m_new = jnp.maximum(ms[h], jnp.max(s, axis=0, keepdims=True))
            alpha = jnp.exp(ms[h] - m_new)
            p = jnp.exp(s - m_new)
            new_ls.append(alpha * ls[h] + jnp.sum(p, axis=0, keepdims=True))
            new_ms.append(m_new)
            alphas.append(alpha)
            p_s[h] = p.astype(BF16)
        for h in range(ATT_HEADS):
            lo, hi = h * ATT_HEADDIM, (h + 1) * ATT_HEADDIM
            pv = jnp.dot(vt_ref[lo:hi, pl.ds(off, KC)], p_s[h],
                         preferred_element_type=F32)
            acc_s[h] = alphas[h] * acc_s[h] + pv
        return tuple(new_ms), tuple(new_ls)

    m0 = tuple(jnp.full((1, QB), -1e30, F32) for _ in range(ATT_HEADS))
    l0 = tuple(jnp.zeros((1, QB), F32) for _ in range(ATT_HEADS))
    _, ls = lax.fori_loop(0, nkc, att_chunk, (m0, l0))
    for h in range(ATT_HEADS):
        o_ref[:, h * ATT_HEADDIM:(h + 1) * ATT_HEADDIM] = (acc_s[h] / ls[h]).T.astype(o_ref.dtype)


def _dsa(qt, iqt, vt, kk, sm, smt, g, b, bsz, seq):
    qb = min(DSA_QB, seq)
    nq = seq // qb
    t = bsz * seq
    topk = min(TOPK_MAX, seq // 4)
    single = pl.Buffered(1)
    return pl.pallas_call(
        functools.partial(_dsa_kernel, topk=topk),
        name="dsa",
        grid=(bsz, nq),
        in_specs=[pl.BlockSpec((ATT_WIDTH, qb), lambda bi, qi: (0, bi * nq + qi)),
                  pl.BlockSpec((IDX_HEADS * IDX_DIM, qb), lambda bi, qi: (0, bi * nq + qi)),
                  pl.BlockSpec((ATT_WIDTH, seq), lambda bi, qi: (0, bi), pipeline_mode=single),
                  pl.BlockSpec((seq, ATT_WIDTH), lambda bi, qi: (bi, 0), pipeline_mode=single),
                  pl.BlockSpec((seq, SM_W), lambda bi, qi: (bi, 0), pipeline_mode=single),
                  pl.BlockSpec((SM_W, qb), lambda bi, qi: (0, bi * nq + qi)),
                  pl.BlockSpec((1, IDX_DIM), lambda bi, qi: (0, 0)),
                  pl.BlockSpec((1, IDX_DIM), lambda bi, qi: (0, 0))],
        out_specs=pl.BlockSpec((qb, ATT_WIDTH), lambda bi, qi: (bi * nq + qi, 0)),
        out_shape=jax.ShapeDtypeStruct((t, ATT_WIDTH), BF16),
        scratch_shapes=[pltpu.VMEM((seq, IDX_DIM), BF16),
                        pltpu.VMEM((seq // KEY_CHUNK, KEY_CHUNK, qb), F32),
                        pltpu.VMEM((ATT_HEADS, ATT_HEADDIM, qb), F32),
                        pltpu.VMEM((ATT_HEADS, KEY_CHUNK, qb), F32),
                        pltpu.VMEM((ATT_HEADS, KEY_CHUNK, qb), BF16),
                        pltpu.VMEM((KEY_CHUNK, qb), F32)],
        compiler_params=pltpu.CompilerParams(
            dimension_semantics=("parallel", "arbitrary"), vmem_limit_bytes=VMEM_LIMIT),
    )(qt, iqt, vt, kk, sm, smt, g.astype(F32).reshape(1, IDX_DIM),
      b.astype(F32).reshape(1, IDX_DIM))


def _route_rows(logits_t, rb):
    aff = _sigmoid(logits_t)
    biased = aff + rb
    rb_rows = [biased[r:r + 1, :] for r in range(N_EXPERTS)]
    ra_rows = [aff[r:r + 1, :] for r in range(N_EXPERTS)]
    epg = EXPERTS_PER_GROUP
    gs = []
    for g in range(N_EXPERT_GROUPS):
        a, b, c, d = rb_rows[epg * g:epg * g + epg]
        p, q = jnp.maximum(a, b), jnp.minimum(a, b)
        r, s = jnp.maximum(c, d), jnp.minimum(c, d)
        gs.append(jnp.maximum(p, r) + jnp.maximum(jnp.minimum(p, r), jnp.maximum(q, s)))
    best = gs[0]
    bidx = jnp.zeros_like(best, dtype=I32)
    for g in range(1, N_EXPERT_GROUPS):
        better = gs[g] > best
        best = jnp.where(better, gs[g], best)
        bidx = jnp.where(better, g, bidx)
    vb = [rb_rows[j] for j in range(epg)]
    va = [ra_rows[j] for j in range(epg)]
    for g in range(1, N_EXPERT_GROUPS):
        pick = bidx == g
        vb = [jnp.where(pick, rb_rows[epg * g + j], vb[j]) for j in range(epg)]
        va = [jnp.where(pick, ra_rows[epg * g + j], va[j]) for j in range(epg)]
    t1, a1, i1 = vb[0], va[0], jnp.zeros_like(bidx)
    for j in range(1, epg):
        better = vb[j] > t1
        t1 = jnp.where(better, vb[j], t1)
        a1 = jnp.where(better, va[j], a1)
        i1 = jnp.where(better, j, i1)
    t2 = jnp.full_like(t1, -jnp.inf)
    a2 = jnp.zeros_like(a1)
    i2 = jnp.zeros_like(i1)
    for j in range(epg):
        vj = jnp.where(i1 == j, -jnp.inf, vb[j])
        better = vj > t2
        t2 = jnp.where(better, vj, t2)
        a2 = jnp.where(better, va[j], a2)
        i2 = jnp.where(better, j, i2)
    den = a1 + a2
    return bidx * epg + i1, bidx * epg + i2, a1 / den, a2 / den


def _outproj_kernel(ys_ref, ya_ref, w1_ref, w2_ref, h_ref, g_ref, b_ref, rwh_ref, rwl_ref, rb_ref,
                    h1_ref, h1b_ref, eidx_ref, gate_ref):
    mix = jnp.dot(ys_ref[...], w1_ref[...], preferred_element_type=F32)
    mix = mix + jnp.dot(ya_ref[...], w2_ref[...], preferred_element_type=F32)
    h1 = _layer_norm_rows(ALPHA * h_ref[...] + mix, g_ref[...], b_ref[...])
    h1_ref[...] = h1
    h1_hi = h1.astype(BF16)
    h1b_ref[...] = h1_hi
    h1_lo = (h1 - h1_hi.astype(F32)).astype(BF16)
    logits = jnp.dot(h1_hi, rwh_ref[...], preferred_element_type=F32)
    logits = logits + jnp.dot(h1_hi, rwl_ref[...], preferred_element_type=F32)
    logits = logits + jnp.dot(h1_lo, rwh_ref[...], preferred_element_type=F32)
    logits_t = logits.T[0:N_EXPERTS, :]
    e1, e2, g1, g2 = _route_rows(logits_t, rb_ref[...])
    eidx_ref[0:1, :] = e1
    eidx_ref[1:2, :] = e2
    gate_ref[0:1, :] = g1
    gate_ref[1:2, :] = g2


def _outproj_ln_router(y_ssd, y_att, w_out, h, ln_g, ln_b, router_w, router_b, tm):
    t, d = h.shape
    w = w_out.astype(BF16)
    rw = jnp.pad(router_w.astype(F32), ((0, 0), (0, V7X_LANES - N_EXPERTS)))
    rw_hi = rw.astype(BF16)
    rw_lo = (rw - rw_hi.astype(F32)).astype(BF16)
    const = lambda shape: pl.BlockSpec(shape, lambda i: (0, 0))
    return pl.pallas_call(
        _outproj_kernel,
        name="outproj_ln_router",
        grid=(t // tm,),
        in_specs=[pl.BlockSpec((tm, SSD_WIDTH), lambda i: (i, 0)),
                  pl.BlockSpec((tm, ATT_WIDTH), lambda i: (i, 0)),
                  pl.BlockSpec((SSD_WIDTH, d), lambda i: (0, 0)),
                  pl.BlockSpec((ATT_WIDTH, d), lambda i: (1, 0)),
                  pl.BlockSpec((tm, d), lambda i: (i, 0)),
                  const((1, d)), const((1, d)), const((d, V7X_LANES)), const((d, V7X_LANES)),
                  const((N_EXPERTS, 1))],
        out_specs=[pl.BlockSpec((tm, d), lambda i: (i, 0)),
                   pl.BlockSpec((tm, d), lambda i: (i, 0)),
                   pl.BlockSpec((2, tm), lambda i: (0, i)),
                   pl.BlockSpec((2, tm), lambda i: (0, i))],
        out_shape=[jax.ShapeDtypeStruct((t, d), F32), jax.ShapeDtypeStruct((t, d), BF16),
                   jax.ShapeDtypeStruct((2, t), I32), jax.ShapeDtypeStruct((2, t), F32)],
        compiler_params=pltpu.CompilerParams(
            dimension_semantics=("parallel",), vmem_limit_bytes=VMEM_LIMIT),
    )(y_ssd, y_att, w, w, h, ln_g.astype(F32).reshape(1, d), ln_b.astype(F32).reshape(1, d),
      rw_hi, rw_lo, router_b.astype(F32).reshape(N_EXPERTS, 1))


def _expert_kernel(be_ref, nu_ref, st_ref, h_hbm, wg_ref, wu_ref, wd_ref, o_ref,
                   xbuf, wg_s, wu_s, wd_s, sem):
    i = pl.program_id(0)
    nb = pl.num_programs(0)
    slot = lax.rem(i, 2)
    nxt_base = jnp.minimum(i + 1, nb - 1) * MOE_BLOCK

    def row_copy(tok, r, s):
        return pltpu.make_async_copy(h_hbm.at[pl.ds(tok, 1), :], xbuf.at[s, pl.ds(r, 1), :],
                                     sem.at[s])

    def issue_rolled(base, s):
        def body(r, carry):
            row_copy(st_ref[base + r], r, s).start()
            return carry
        lax.fori_loop(0, MOE_BLOCK, body, 0)

    def wait_rows(s):
        pltpu.make_async_copy(h_hbm.at[pl.ds(0, MOE_BLOCK), :], xbuf.at[s], sem.at[s]).wait()

    @pl.when(i == 0)
    def _():
        issue_rolled(0, 0)

    wait_rows(slot)

    @pl.when(i < nu_ref[0])
    def _():
        prev = be_ref[jnp.maximum(i - 1, 0)]

        @pl.when((i == 0) | (be_ref[i] != prev))
        def _():
            wg_s[...] = wg_ref[...].astype(BF16)
            wu_s[...] = wu_ref[...].astype(BF16)
            wd_s[...] = wd_ref[...].astype(BF16)

        for r in range(MOE_BLOCK):
            row_copy(st_ref[nxt_base + r], r, 1 - slot).start()
        x = xbuf[slot].astype(BF16)
        g = jnp.dot(x, wg_s[...], preferred_element_type=F32)
        u = jnp.dot(x, wu_s[...], preferred_element_type=F32)
        a = (g * _sigmoid(g) * u).astype(BF16)
        o_ref[...] = jnp.dot(a, wd_s[...], preferred_element_type=F32)

    @pl.when(i >= nu_ref[0])
    def _():
        issue_rolled(nxt_base, 1 - slot)
        o_ref[...] = jnp.zeros_like(o_ref)

    @pl.when(i == nb - 1)
    def _():
        wait_rows(1 - slot)


def _experts(block_expert, n_used, slot_tok, h1, w_gate, w_up, w_down):
    t, d = h1.shape
    cap = slot_tok.shape[0]
    nb = cap // MOE_BLOCK
    return pl.pallas_call(
        _expert_kernel,
        name="moe_experts",
        grid_spec=pltpu.PrefetchScalarGridSpec(
            num_scalar_prefetch=3,
            grid=(nb,),
            in_specs=[pl.BlockSpec(memory_space=pl.ANY),
                      pl.BlockSpec((None, d, D_FF), lambda i, be, nu, st: (be[i], 0, 0)),
                      pl.BlockSpec((None, d, D_FF), lambda i, be, nu, st: (be[i], 0, 0)),
                      pl.BlockSpec((None, D_FF, d), lambda i, be, nu, st: (be[i], 0, 0))],
            out_specs=pl.BlockSpec((MOE_BLOCK, d), lambda i, be, nu, st: (i, 0)),
            scratch_shapes=[pltpu.VMEM((2, MOE_BLOCK, d), F32),
                            pltpu.VMEM((d, D_FF), BF16), pltpu.VMEM((d, D_FF), BF16),
                            pltpu.VMEM((D_FF, d), BF16), pltpu.SemaphoreType.DMA((2,))]),
        out_shape=jax.ShapeDtypeStruct((cap, d), F32),
        compiler_params=pltpu.CompilerParams(
            dimension_semantics=("arbitrary",), vmem_limit_bytes=VMEM_LIMIT),
    )(block_expert, n_used, slot_tok, h1, w_gate, w_up, w_down)


def _moe_plan(eidx, t):
    flat_e = eidx.reshape(2 * t)
    onehot = (flat_e[:, None] == jnp.arange(N_EXPERTS, dtype=I32)[None, :]).astype(I32)
    cs = jnp.cumsum(onehot, axis=0)
    rank = jnp.sum(onehot * cs, axis=1) - 1
    counts = cs[-1]
    padded = (counts + MOE_BLOCK - 1) // MOE_BLOCK * MOE_BLOCK
    pad_end = jnp.cumsum(padded)
    pad_start = pad_end - padded
    dest = (pad_start[flat_e] + rank).astype(I32)
    cap = 2 * t + N_EXPERTS * MOE_BLOCK
    nb = cap // MOE_BLOCK
    block_start = jnp.arange(nb, dtype=I32) * MOE_BLOCK
    block_expert = jnp.minimum(
        jnp.sum((pad_end[None, :] <= block_start[:, None]).astype(I32), axis=1), N_EXPERTS - 1)
    n_used = (pad_end[-1] // MOE_BLOCK).astype(I32).reshape(1)
    tok = jnp.tile(jnp.arange(t, dtype=I32), 2)
    slot_tok = jnp.zeros((cap,), I32).at[dest].set(tok, unique_indices=True)
    return dest, block_expert, n_used, slot_tok


def _ple_kernel(dest_ref, h1_ref, h1b_ref, yb_hbm, gt_ref, p_ref, wg_ref, wp_ref, g_ref, b_ref,
                o_ref, ob_ref, gbuf, sem):
    i = pl.program_id(0)
    n = pl.num_programs(0)
    tm = h1_ref.shape[0]
    t = n * tm
    slot = lax.rem(i, 2)
    nxt_base = jnp.minimum(i + 1, n - 1) * tm

    def row_copy(src_row, k, r, s):
        return pltpu.make_async_copy(yb_hbm.at[pl.ds(src_row, 1), :],
                                     gbuf.at[s, pl.ds(k * tm + r, 1), :], sem.at[s])

    def wait_tile(s):
        pltpu.make_async_copy(yb_hbm.at[pl.ds(0, 2 * tm), :], gbuf.at[s], sem.at[s]).wait()

    @pl.when(i == 0)
    def _():
        def body(r, carry):
            for k in range(2):
                row_copy(dest_ref[k * t + r], k, r, 0).start()
            return carry
        lax.fori_loop(0, tm, body, 0)

    wait_tile(slot)
    for r in range(tm):
        for k in range(2):
            row_copy(dest_ref[k * t + nxt_base + r], k, r, 1 - slot).start()
    ffn = gt_ref[:, 0:1] * gbuf[slot, 0:tm, :] + gt_ref[:, 1:2] * gbuf[slot, tm:2 * tm, :]
    gate = _sigmoid(jnp.dot(h1b_ref[...], wg_ref[...], preferred_element_type=F32))
    pe = jnp.dot(p_ref[...].astype(BF16), wp_ref[...], preferred_element_type=F32)
    u = ALPHA * h1_ref[...] + ffn + gate * pe
    h2 = _layer_norm_rows(u, g_ref[...], b_ref[...])
    o_ref[...] = h2
    ob_ref[...] = h2.astype(BF16)

    @pl.when(i == n - 1)
    def _():
        wait_tile(1 - slot)


def _ple_ln(dest, h1, h1b, yb, gates_t, p, ple_gate_w, ple_w, ln_g, ln_b, tm):
    t, d = h1.shape
    pd = p.shape[1]
    row = lambda w: pl.BlockSpec((tm, w), lambda i, dest: (i, 0))
    const = lambda shape: pl.BlockSpec(shape, lambda i, dest: (0, 0), pipeline_mode=pl.Buffered(1))
    return pl.pallas_call(
        _ple_kernel,
        name="ple_ln",
        grid_spec=pltpu.PrefetchScalarGridSpec(
            num_scalar_prefetch=1,
            grid=(t // tm,),
            in_specs=[row(d), row(d), pl.BlockSpec(memory_space=pl.ANY), row(2), row(pd),
                      const((d, d)), const((pd, d)), const((1, d)), const((1, d))],
            out_specs=[row(d), row(d)],
            scratch_shapes=[pltpu.VMEM((2, 2 * tm, d), F32), pltpu.SemaphoreType.DMA((2,))]),
        out_shape=[jax.ShapeDtypeStruct((t, d), F32), jax.ShapeDtypeStruct((t, d), BF16)],
        compiler_params=pltpu.CompilerParams(
            dimension_semantics=("arbitrary",), vmem_limit_bytes=VMEM_LIMIT),
    )(dest, h1, h1b, yb, gates_t, p, ple_gate_w.astype(BF16), ple_w.astype(BF16),
      ln_g.astype(F32).reshape(1, d), ln_b.astype(F32).reshape(1, d))


def _layer(h, hb, p_i, w_in, layer, conv_w, conv_b, dt_bias, a_log, d_skip, ssm_norm_w,
           idx_k_norm_g, idx_k_norm_b, w_out, ln1_g, ln1_b, router_w, router_b,
           w_gate, w_up, w_down, ple_w, ple_gate_w, ln2_g, ln2_b, bsz, seq):
    t = bsz * seq
    tm = min(1024, t)
    za = _proj(hb, w_in, layer, OFF_Z, OFF_DT - OFF_Z, F32, False, tm)
    kk = _proj(hb, w_in, layer, OFF_Q + ATT_WIDTH, ATT_WIDTH, BF16, False, tm)
    qt = _proj(hb, w_in, layer, OFF_Q, ATT_WIDTH, BF16, True, tm)
    vt = _proj(hb, w_in, layer, OFF_Q + 2 * ATT_WIDTH, ATT_WIDTH, BF16, True, tm)
    iqt = _proj(hb, w_in, layer, OFF_IQ, IDX_HEADS * IDX_DIM, BF16, True, tm)
    sm, smt = _proj_small(hb, w_in, layer, tm)

    y_ssd = _ssd(za, sm, conv_w, conv_b, dt_bias, a_log, d_skip, ssm_norm_w, bsz, seq)
    y_att = _dsa(qt, iqt, vt, kk, sm, smt, idx_k_norm_g, idx_k_norm_b, bsz, seq)
    h1, h1b, eidx, gates = _outproj_ln_router(
        y_ssd, y_att, w_out, h, ln1_g, ln1_b, router_w, router_b, min(256, t))

    dest, block_expert, n_used, slot_tok = _moe_plan(eidx, t)
    yb = _experts(block_expert, n_used, slot_tok, h1, w_gate, w_up, w_down)
    return _ple_ln(dest, h1, h1b, yb, gates.T, p_i, ple_gate_w, ple_w, ln2_g, ln2_b, min(256, t))


def kernel(x, p, w_in, conv_w, conv_b, dt_bias, a_log, d_skip, ssm_norm_w, idx_k_norm_g,
           idx_k_norm_b, w_out, ln1_g, ln1_b, router_w, router_b, w_gate, w_up, w_down,
           ple_w, ple_gate_w, ln2_g, ln2_b):
    bsz, seq, d = x.shape
    t = bsz * seq
    h = x.reshape(t, d).astype(F32)
    hb = h.astype(BF16)
    for i in range(w_in.shape[0]):
        h, hb = _layer(h, hb, p[i].reshape(t, -1), w_in, i, conv_w[i], conv_b[i], dt_bias[i],
                       a_log[i], d_skip[i], ssm_norm_w[i], idx_k_norm_g[i], idx_k_norm_b[i],
                       w_out[i], ln1_g[i], ln1_b[i], router_w, router_b, w_gate[i], w_up[i],
                       w_down[i], ple_w[i], ple_gate_w[i], ln2_g[i], ln2_b[i], bsz, seq)
    return h.reshape(bsz, seq, d).astype(x.dtype)
```

```python
import functools

import jax
import jax.numpy as jnp
from jax import lax
from jax.experimental import pallas as pl
from jax.experimental.pallas import tpu as pltpu

F32 = jnp.float32
BF16 = jnp.bfloat16
I32 = jnp.int32

SSD_WIDTH = 1024
SSD_HEADDIM = 64
SSD_HEADS = 16
SSD_STATE = 128
SSD_GROUPS = 2
CONV_WIDTH = 4
CHUNK = 128
CONV_DIM = SSD_WIDTH + 2 * SSD_GROUPS * SSD_STATE
ATT_HEADS = 8
ATT_HEADDIM = 128
ATT_WIDTH = ATT_HEADS * ATT_HEADDIM
IDX_HEADS = 16
IDX_DIM = 64
TOPK_MAX = 256
Q_BLOCK = 128
N_EXPERTS = 32
N_EXPERT_GROUPS = 8
EXPERTS_PER_GROUP = 4
D_FF = 512
ALPHA = (2.0 * 2) ** 0.25
LN_EPS = 1e-5

OFF_Z = 0
OFF_XBC = OFF_Z + SSD_WIDTH
OFF_DT = OFF_XBC + CONV_DIM
OFF_Q = OFF_DT + SSD_HEADS
OFF_IQ = OFF_Q + 3 * ATT_WIDTH
OFF_IK = OFF_IQ + IDX_HEADS * IDX_DIM
OFF_IW = OFF_IK + IDX_DIM
SM_DT = 0
SM_IK = SSD_HEADS
SM_IW = SM_IK + IDX_DIM
SM_W = 128

V7X_LANES = 128
VMEM_LIMIT = 56 * 1024 * 1024
FLOAT_LOWEST = -3.0e38
SEARCH_PROBES_PER_CHECK = 3
SEARCH_MAX_ITERS = 200
KEY_CHUNK = 256
DSA_QB = 256
DSA_KV_SPLIT = 4
MOE_BLOCK = 256
MOE_W_SPLIT = 4


def _sigmoid(x):
    return 1.0 / (1.0 + jnp.exp(-x))


def _softplus(x):
    return jnp.maximum(x, 0.0) + jnp.log(1.0 + jnp.exp(-jnp.abs(x)))


def _nt_dot(a, b, **kw):
    return lax.dot_general(a, b, (((1,), (1,)), ((), ())), preferred_element_type=F32, **kw)


def _layer_norm_rows(u, g, b):
    mu = jnp.mean(u, axis=-1, keepdims=True)
    d = u - mu
    var = jnp.mean(d * d, axis=-1, keepdims=True)
    return d * lax.rsqrt(var + LN_EPS) * g + b


PROJ_ROWS = 256


def _proj_kernel(x_ref, wt_ref, o_ref, w_s, *, transposed):
    tn, k = wt_ref.shape

    @pl.when(pl.program_id(1) == 0)
    def _():
        if transposed:
            w_s[...] = wt_ref[...].astype(BF16)
        else:
            def stage(r, carry):
                r0 = pl.multiple_of(r * PROJ_ROWS, PROJ_ROWS)
                w_s[pl.ds(r0, PROJ_ROWS), :] = wt_ref[:, pl.ds(r0, PROJ_ROWS)].T.astype(BF16)
                return carry

            lax.fori_loop(0, k // PROJ_ROWS, stage, 0)

    if transposed:
        o_ref[...] = _nt_dot(w_s[...], x_ref[...]).astype(o_ref.dtype)
    else:
        o_ref[...] = jnp.dot(x_ref[...], w_s[...], preferred_element_type=F32).astype(o_ref.dtype)


def _proj(x, w_in_t, row0, c0, width, out_dtype, transposed, tm, tn=512):
    m, k = x.shape
    assert (row0 + c0) % 8 == 0 and width % tn == 0 and m % tm == 0
    in_specs = [pl.BlockSpec((tm, k), lambda j, i: (i, 0)),
                pl.BlockSpec((pl.Element(tn), pl.Element(k)),
                             lambda j, i: (pl.multiple_of(row0 + c0 + j * tn, 8), 0))]
    if transposed:
        out_spec = pl.BlockSpec((tn, tm), lambda j, i: (j, i))
        out_shape = jax.ShapeDtypeStruct((width, m), out_dtype)
        scratch = pltpu.VMEM((tn, k), BF16)
    else:
        out_spec = pl.BlockSpec((tm, tn), lambda j, i: (i, j))
        out_shape = jax.ShapeDtypeStruct((m, width), out_dtype)
        scratch = pltpu.VMEM((k, tn), BF16)
    return pl.pallas_call(
        functools.partial(_proj_kernel, transposed=transposed),
        name="in_proj_t" if transposed else "in_proj",
        grid=(width // tn, m // tm),
        in_specs=in_specs,
        out_specs=out_spec,
        out_shape=out_shape,
        scratch_shapes=[scratch],
        compiler_params=pltpu.CompilerParams(
            dimension_semantics=("parallel", "arbitrary"), vmem_limit_bytes=VMEM_LIMIT),
    )(x, w_in_t)


def _proj_small_kernel(x_ref, wa_ref, wb_ref, o_ref, ot_ref, w_s, wt_s):
    @pl.when(pl.program_id(0) == 0)
    def _():
        k = wa_ref.shape[1]
        tail = SM_W - IDX_DIM - IDX_HEADS
        wt = jnp.concatenate(
            [wa_ref[0:SSD_HEADS, :], wb_ref[tail:SM_W, :],
             jnp.zeros((SM_W - SSD_HEADS - IDX_DIM - IDX_HEADS, k), F32)], axis=0)
        wt_s[...] = wt.astype(BF16)
        w_s[...] = wt.T.astype(BF16)

    o_ref[...] = jnp.dot(x_ref[...], w_s[...], preferred_element_type=F32)
    ot_ref[...] = _nt_dot(wt_s[...], x_ref[...])


def _proj_small(x, w_in_t, row0, tm):
    m, k = x.shape
    d_in = OFF_IW + IDX_HEADS
    assert OFF_IW - OFF_IK == IDX_DIM and (row0 + OFF_DT) % 8 == 0 and (row0 + d_in) % 8 == 0
    return pl.pallas_call(
        _proj_small_kernel,
        name="in_proj_small",
        grid=(m // tm,),
        in_specs=[pl.BlockSpec((tm, k), lambda i: (i, 0)),
                  pl.BlockSpec((pl.Element(SM_W), pl.Element(k)), lambda i: (row0 + OFF_DT, 0)),
                  pl.BlockSpec((pl.Element(SM_W), pl.Element(k)),
                               lambda i: (row0 + d_in - SM_W, 0))],
        out_specs=[pl.BlockSpec((tm, SM_W), lambda i: (i, 0)),
                   pl.BlockSpec((SM_W, tm), lambda i: (0, i))],
        out_shape=[jax.ShapeDtypeStruct((m, SM_W), F32), jax.ShapeDtypeStruct((SM_W, m), F32)],
        scratch_shapes=[pltpu.VMEM((k, SM_W), BF16), pltpu.VMEM((SM_W, k), BF16)],
        compiler_params=pltpu.CompilerParams(
            dimension_semantics=("arbitrary",), vmem_limit_bytes=VMEM_LIMIT),
    )(x, w_in_t, w_in_t)


def _ssd_kernel(za_ref, sm_ref, cw_ref, cb_ref, dtb_ref, a_ref, dexp_ref, nw_ref, e_ref,
                o_ref, cbuf, state, ybuf):
    c = pl.program_id(1)
    L = CHUNK

    @pl.when(c == 0)
    def _():
        cbuf[0:8, :] = jnp.zeros((8, CONV_DIM), F32)
        state[...] = jnp.zeros_like(state)

    cbuf[8:8 + L, :] = za_ref[:, SSD_WIDTH:SSD_WIDTH + CONV_DIM]
    acc = cb_ref[...] + cw_ref[0:1, :] * cbuf[5:5 + L, :]
    acc = acc + cw_ref[1:2, :] * cbuf[6:6 + L, :]
    acc = acc + cw_ref[2:3, :] * cbuf[7:7 + L, :]
    acc = acc + cw_ref[3:4, :] * cbuf[8:8 + L, :]
    cbuf[0:8, :] = cbuf[L:L + 8, :]
    xbc = acc * _sigmoid(acc)
    xs = xbc[:, 0:SSD_WIDTH]
    bm = xbc[:, SSD_WIDTH:SSD_WIDTH + SSD_GROUPS * SSD_STATE]
    cm = xbc[:, SSD_WIDTH + SSD_GROUPS * SSD_STATE:CONV_DIM]

    dt = _softplus(sm_ref[...] + dtb_ref[...])
    da = dt * a_ref[...]
    row = lax.broadcasted_iota(I32, (L, L), 0)
    col = lax.broadcasted_iota(I32, (L, L), 1)
    causal = col <= row
    tri = jnp.where(causal, 1.0, 0.0).astype(F32)
    a_cs = jnp.dot(tri, da, preferred_element_type=F32, precision=lax.Precision.HIGHEST)
    a_cs_t = a_cs.T
    expa = jnp.exp(a_cs)
    dte = jnp.exp(a_cs[L - 1:L, :] - a_cs)
    stacked = jnp.concatenate([dt, expa, dte], axis=0)
    expd = jnp.dot(stacked, e_ref[...], preferred_element_type=F32,
                   precision=lax.Precision.HIGHEST)
    dt_x = expd[0:L, :]
    expa_x = expd[L:2 * L, :]
    dte_x = expd[2 * L:3 * L, :]
    xdt = xs * dt_x
    hpg = SSD_HEADS // SSD_GROUPS
    gw = hpg * SSD_HEADDIM
    for g in range(SSD_GROUPS):
        cm_g = cm[:, g * SSD_STATE:(g + 1) * SSD_STATE].astype(BF16)
        bm_gf = bm[:, g * SSD_STATE:(g + 1) * SSD_STATE]
        bm_g = bm_gf.astype(BF16)
        cb = _nt_dot(cm_g, bm_g)
        for hh in range(hpg):
            h = g * hpg + hh
            seg = a_cs[:, h:h + 1] - a_cs_t[h:h + 1, :]
            dec = jnp.exp(jnp.where(causal, seg, -jnp.inf))
            mh = (cb * dec).astype(BF16)
            xh = xdt[:, h * SSD_HEADDIM:(h + 1) * SSD_HEADDIM].astype(BF16)
            ybuf[:, h * SSD_HEADDIM:(h + 1) * SSD_HEADDIM] = jnp.dot(
                mh, xh, preferred_element_type=F32)
        st_g = state[:, g * gw:(g + 1) * gw]
        y_off = jnp.dot(cm_g, st_g.astype(BF16), preferred_element_type=F32)
        ybuf[:, g * gw:(g + 1) * gw] = (ybuf[:, g * gw:(g + 1) * gw]
                                        + y_off * expa_x[:, g * gw:(g + 1) * gw])
        xw = (xdt[:, g * gw:(g + 1) * gw] * dte_x[:, g * gw:(g + 1) * gw]).astype(BF16)
        upd = jnp.dot(bm_gf.T.astype(BF16), xw, preferred_element_type=F32)
        state[:, g * gw:(g + 1) * gw] = st_g * expa_x[L - 1:L, g * gw:(g + 1) * gw] + upd

    y = ybuf[...] + xs * dexp_ref[...]
    z = za_ref[:, 0:SSD_WIDTH]
    yg = y * (z * _sigmoid(z))
    ms = jnp.mean(yg * yg, axis=-1, keepdims=True)
    o_ref[...] = (yg * lax.rsqrt(ms + LN_EPS) * nw_ref[...]).astype(o_ref.dtype)


def _ssd(za, sm, conv_w, conv_b, dt_bias, a_log, d_skip, ssm_norm_w, bsz, seq):
    nc = seq // CHUNK
    t = bsz * seq
    pad = SM_W - SSD_HEADS
    dtb = jnp.pad(dt_bias.astype(F32), (0, pad)).reshape(1, SM_W)
    a = jnp.pad(-jnp.exp(a_log.astype(F32)), (0, pad)).reshape(1, SM_W)
    dexp = jnp.repeat(d_skip.astype(F32), SSD_HEADDIM).reshape(1, SSD_WIDTH)
    e = (jnp.arange(SSD_WIDTH)[None, :] // SSD_HEADDIM == jnp.arange(SM_W)[:, None]).astype(F32)
    const = lambda shape: pl.BlockSpec(shape, lambda b, c: (0, 0))
    return pl.pallas_call(
        _ssd_kernel,
        name="ssd",
        grid=(bsz, nc),
        in_specs=[pl.BlockSpec((CHUNK, SSD_WIDTH + CONV_DIM), lambda b, c: (b * nc + c, 0)),
                  pl.BlockSpec((CHUNK, SM_W), lambda b, c: (b * nc + c, 0)),
                  const((CONV_WIDTH, CONV_DIM)), const((1, CONV_DIM)), const((1, SM_W)),
                  const((1, SM_W)), const((1, SSD_WIDTH)), const((1, SSD_WIDTH)),
                  const((SM_W, SSD_WIDTH))],
        out_specs=pl.BlockSpec((CHUNK, SSD_WIDTH), lambda b, c: (b * nc + c, 0)),
        out_shape=jax.ShapeDtypeStruct((t, SSD_WIDTH), BF16),
        scratch_shapes=[pltpu.VMEM((CHUNK + 8, CONV_DIM), F32),
                        pltpu.VMEM((SSD_STATE, SSD_WIDTH), F32),
                        pltpu.VMEM((CHUNK, SSD_WIDTH), F32)],
        compiler_params=pltpu.CompilerParams(
            dimension_semantics=("parallel", "arbitrary"), vmem_limit_bytes=VMEM_LIMIT),
    )(za, sm, conv_w.astype(F32), conv_b.astype(F32).reshape(1, CONV_DIM), dtb, a, dexp,
      ssm_norm_w.astype(F32).reshape(1, SSD_WIDTH), e)


def _dsa_kernel(qt_ref, iqt_ref, *rest, topk):
    ns = DSA_KV_SPLIT
    vt_refs, k_refs = rest[0:ns], rest[ns:2 * ns]
    smb_ref, smt_ref, g_ref, b_ref, o_ref, ikn, keys, acc_s, s_s, p_s, bias_s = rest[2 * ns:]
    slab = ATT_WIDTH // ns
    qi = pl.program_id(1)
    QB = o_ref.shape[0]
    KC = KEY_CHUNK

    @pl.when(qi == 0)
    def _():
        ik = smb_ref[:, SM_IK:SM_IK + IDX_DIM]
        ikn[...] = _layer_norm_rows(ik, g_ref[...], b_ref[...]).astype(BF16)

    nkc = (qi * QB + QB - 1) // KC + 1
    w = smt_ref[SM_IW:SM_IW + IDX_HEADS, :] * (IDX_DIM ** -0.5 * IDX_HEADS ** -0.5)
    qpos = qi * QB + lax.broadcasted_iota(I32, (KC, QB), 1)
    kiota = lax.broadcasted_iota(I32, (KC, QB), 0)

    def score_chunk(c, carry):
        rmax, rmin = carry
        off = pl.multiple_of(c * KC, KC)
        kc = ikn[pl.ds(off, KC), :]
        acc = jnp.zeros((KC, QB), F32)
        for h in range(IDX_HEADS):
            d = jnp.dot(kc, iqt_ref[h * IDX_DIM:(h + 1) * IDX_DIM, :], preferred_element_type=F32)
            acc = acc + w[h:h + 1, :] * jnp.maximum(d, 0.0)
        vis = off + kiota <= qpos
        sv = jnp.where(vis, acc, -jnp.inf)
        keys[c] = sv
        rmax = jnp.maximum(rmax, jnp.max(sv, axis=0, keepdims=True))
        rmin = jnp.minimum(rmin, jnp.min(jnp.where(vis, acc, jnp.inf), axis=0, keepdims=True))
        return rmax, rmin

    rmax, rmin = lax.fori_loop(
        0, nkc, score_chunk,
        (jnp.full((1, QB), -jnp.inf, F32), jnp.full((1, QB), jnp.inf, F32)))

    kf = float(topk)

    def count_ge(mid):
        def count_chunk(c, acc):
            m = jnp.where(keys[c] >= mid, 1.0, 0.0)
            return acc + jnp.sum(m.reshape(KC // 8, 8, QB), axis=0)

        acc = lax.fori_loop(0, nkc, count_chunk, jnp.zeros((8, QB), F32))
        return jnp.sum(acc, axis=0, keepdims=True)

    nvis = (qi * QB + lax.broadcasted_iota(I32, (1, QB), 1) + 1).astype(F32)
    small = nvis <= kf

    def search_cond(st):
        return (st[0] > 0.0) & (st[1] < SEARCH_MAX_ITERS)

    def search_body(st):
        _, it, lo, hi, clo = st
        for _ in range(SEARCH_PROBES_PER_CHECK):
            mid = 0.5 * lo + 0.5 * hi
            splittable = (mid > lo) & (mid < hi)
            c = count_ge(mid)
            up = splittable & (c >= kf)
            down = splittable & (c < kf)
            lo = jnp.where(up, mid, lo)
            clo = jnp.where(up, c, clo)
            hi = jnp.where(down, mid, hi)
        active = jnp.logical_not(small) & splittable & (clo != kf)
        return jnp.sum(jnp.where(active, 1.0, 0.0)), it + 1, lo, hi, clo

    hi0 = rmax + jnp.abs(rmax) * 1e-6 + 1e-37
    st = lax.while_loop(search_cond, search_body,
                        (jnp.float32(1.0), jnp.int32(0), rmin, hi0, nvis))
    thr = jnp.where(small, FLOAT_LOWEST, st[2])

    scale = ATT_HEADDIM ** -0.5
    acc_s[...] = jnp.zeros_like(acc_s)

    def att_chunk(c, carry):
        ms, ls = carry
        off = pl.multiple_of(c * KC, KC)
        bias_s[...] = jnp.where(keys[c] >= thr, 0.0, -1e30)
        for h in range(ATT_HEADS):
            lo, hi = h * ATT_HEADDIM, (h + 1) * ATT_HEADDIM
            s_s[h] = jnp.dot(k_refs[lo // slab][pl.ds(off, KC), lo % slab:lo % slab + ATT_HEADDIM],
                             qt_ref[lo:hi, :],
                             preferred_element_type=F32)
        new_ms, new_ls, alphas = [], [], []
        for h in range(ATT_HEADS):
            s = s_s[h] * scale + bias_s[...]
            m_new = jnp.maximum(ms[h], jnp.max(s, axis=0, keepdims=True))
            alpha = jnp.exp(ms[h] - m_new)
            p = jnp.exp(s - m_new)
            new_ls.append(alpha * ls[h] + jnp.sum(p, axis=0, keepdims=True))
            new_ms.append(m_new)
            alphas.append(alpha)
            p_s[h] = p.astype(BF16)
        for h in range(ATT_HEADS):
            lo, hi = h * ATT_HEADDIM, (h + 1) * ATT_HEADDIM
            pv = jnp.dot(vt_refs[lo // slab][lo % slab:lo % slab + ATT_HEADDIM, pl.ds(off, KC)],
                         p_s[h],
                         preferred_element_type=F32)
            acc_s[h] = alphas[h] * acc_s[h] + pv
        return tuple(new_ms), tuple(new_ls)

    m0 = tuple(jnp.full((1, QB), -1e30, F32) for _ in range(ATT_HEADS))
    l0 = tuple(jnp.zeros((1, QB), F32) for _ in range(ATT_HEADS))
    _, ls = lax.fori_loop(0, nkc, att_chunk, (m0, l0))
    for h in range(ATT_HEADS):
        o_ref[:, h * ATT_HEADDIM:(h + 1) * ATT_HEADDIM] = (acc_s[h] / ls[h]).T.astype(o_ref.dtype)


def _dsa(qt, iqt, vt, kk, sm, smt, g, b, bsz, seq):
    qb = min(DSA_QB, seq)
    nq = seq // qb
    t = bsz * seq
    topk = min(TOPK_MAX, seq // 4)
    single = pl.Buffered(1)
    ns = DSA_KV_SPLIT
    slab = ATT_WIDTH // ns
    vt_specs = [pl.BlockSpec((slab, seq), lambda bi, qi, q=q: (q, bi), pipeline_mode=single)
                for q in range(ns)]
    k_specs = [pl.BlockSpec((seq, slab), lambda bi, qi, q=q: (bi, q), pipeline_mode=single)
               for q in range(ns)]
    return pl.pallas_call(
        functools.partial(_dsa_kernel, topk=topk),
        name="dsa",
        grid=(bsz, nq),
        in_specs=[pl.BlockSpec((ATT_WIDTH, qb), lambda bi, qi: (0, bi * nq + qi)),
                  pl.BlockSpec((IDX_HEADS * IDX_DIM, qb), lambda bi, qi: (0, bi * nq + qi)),
                  *vt_specs, *k_specs,
                  pl.BlockSpec((seq, SM_W), lambda bi, qi: (bi, 0), pipeline_mode=single),
                  pl.BlockSpec((SM_W, qb), lambda bi, qi: (0, bi * nq + qi)),
                  pl.BlockSpec((1, IDX_DIM), lambda bi, qi: (0, 0)),
                  pl.BlockSpec((1, IDX_DIM), lambda bi, qi: (0, 0))],
        out_specs=pl.BlockSpec((qb, ATT_WIDTH), lambda bi, qi: (bi * nq + qi, 0)),
        out_shape=jax.ShapeDtypeStruct((t, ATT_WIDTH), BF16),
        scratch_shapes=[pltpu.VMEM((seq, IDX_DIM), BF16),
                        pltpu.VMEM((seq // KEY_CHUNK, KEY_CHUNK, qb), F32),
                        pltpu.VMEM((ATT_HEADS, ATT_HEADDIM, qb), F32),
                        pltpu.VMEM((ATT_HEADS, KEY_CHUNK, qb), F32),
                        pltpu.VMEM((ATT_HEADS, KEY_CHUNK, qb), BF16),
                        pltpu.VMEM((KEY_CHUNK, qb), F32)],
        compiler_params=pltpu.CompilerParams(
            dimension_semantics=("parallel", "arbitrary"), vmem_limit_bytes=VMEM_LIMIT),
    )(qt, iqt, *([vt] * ns), *([kk] * ns), sm, smt, g.astype(F32).reshape(1, IDX_DIM),
      b.astype(F32).reshape(1, IDX_DIM))


def _route_rows(logits_t, rb):
    aff = _sigmoid(logits_t)
    biased = aff + rb
    rb_rows = [biased[r:r + 1, :] for r in range(N_EXPERTS)]
    ra_rows = [aff[r:r + 1, :] for r in range(N_EXPERTS)]
    epg = EXPERTS_PER_GROUP
    gs = []
    for g in range(N_EXPERT_GROUPS):
        a, b, c, d = rb_rows[epg * g:epg * g + epg]
        p, q = jnp.maximum(a, b), jnp.minimum(a, b)
        r, s = jnp.maximum(c, d), jnp.minimum(c, d)
        gs.append(jnp.maximum(p, r) + jnp.maximum(jnp.minimum(p, r), jnp.maximum(q, s)))
    best = gs[0]
    bidx = jnp.zeros_like(best, dtype=I32)
    for g in range(1, N_EXPERT_GROUPS):
        better = gs[g] > best
        best = jnp.where(better, gs[g], best)
        bidx = jnp.where(better, g, bidx)
    vb = [rb_rows[j] for j in range(epg)]
    va = [ra_rows[j] for j in range(epg)]
    for g in range(1, N_EXPERT_GROUPS):
        pick = bidx == g
        vb = [jnp.where(pick, rb_rows[epg * g + j], vb[j]) for j in range(epg)]
        va = [jnp.where(pick, ra_rows[epg * g + j], va[j]) for j in range(epg)]
    t1, a1, i1 = vb[0], va[0], jnp.zeros_like(bidx)
    for j in range(1, epg):
        better = vb[j] > t1
        t1 = jnp.where(better, vb[j], t1)
        a1 = jnp.where(better, va[j], a1)
        i1 = jnp.where(better, j, i1)
    t2 = jnp.full_like(t1, -jnp.inf)
    a2 = jnp.zeros_like(a1)
    i2 = jnp.zeros_like(i1)
    for j in range(epg):
        vj = jnp.where(i1 == j, -jnp.inf, vb[j])
        better = vj > t2
        t2 = jnp.where(better, vj, t2)
        a2 = jnp.where(better, va[j], a2)
        i2 = jnp.where(better, j, i2)
    den = a1 + a2
    return bidx * epg + i1, bidx * epg + i2, a1 / den, a2 / den


def _outproj_kernel(ys_ref, ya_ref, w1_ref, w2_ref, h_ref, g_ref, b_ref, rwh_ref, rwl_ref, rb_ref,
                    h1_ref, h1b_ref, eidx_ref, gate_ref):
    mix = jnp.dot(ys_ref[...], w1_ref[...], preferred_element_type=F32)
    mix = mix + jnp.dot(ya_ref[...], w2_ref[...], preferred_element_type=F32)
    h1 = _layer_norm_rows(ALPHA * h_ref[...] + mix, g_ref[...], b_ref[...])
    h1_ref[...] = h1
    h1_hi = h1.astype(BF16)
    h1b_ref[...] = h1_hi
    h1_lo = (h1 - h1_hi.astype(F32)).astype(BF16)
    logits = jnp.dot(h1_hi, rwh_ref[...], preferred_element_type=F32)
    logits = logits + jnp.dot(h1_hi, rwl_ref[...], preferred_element_type=F32)
    logits = logits + jnp.dot(h1_lo, rwh_ref[...], preferred_element_type=F32)
    logits_t = logits.T[0:N_EXPERTS, :]
    e1, e2, g1, g2 = _route_rows(logits_t, rb_ref[...])
    eidx_ref[0:1, :] = e1
    eidx_ref[1:2, :] = e2
    gate_ref[0:1, :] = g1
    gate_ref[1:2, :] = g2


def _outproj_ln_router(y_ssd, y_att, w_out, h, ln_g, ln_b, router_w, router_b, tm):
    t, d = h.shape
    w = w_out.astype(BF16)
    rw = jnp.pad(router_w.astype(F32), ((0, 0), (0, V7X_LANES - N_EXPERTS)))
    rw_hi = rw.astype(BF16)
    rw_lo = (rw - rw_hi.astype(F32)).astype(BF16)
    const = lambda shape: pl.BlockSpec(shape, lambda i: (0, 0))
    return pl.pallas_call(
        _outproj_kernel,
        name="outproj_ln_router",
        grid=(t // tm,),
        in_specs=[pl.BlockSpec((tm, SSD_WIDTH), lambda i: (i, 0)),
                  pl.BlockSpec((tm, ATT_WIDTH), lambda i: (i, 0)),
                  pl.BlockSpec((SSD_WIDTH, d), lambda i: (0, 0)),
                  pl.BlockSpec((ATT_WIDTH, d), lambda i: (1, 0)),
                  pl.BlockSpec((tm, d), lambda i: (i, 0)),
                  const((1, d)), const((1, d)), const((d, V7X_LANES)), const((d, V7X_LANES)),
                  const((N_EXPERTS, 1))],
        out_specs=[pl.BlockSpec((tm, d), lambda i: (i, 0)),
                   pl.BlockSpec((tm, d), lambda i: (i, 0)),
                   pl.BlockSpec((2, tm), lambda i: (0, i)),
                   pl.BlockSpec((2, tm), lambda i: (0, i))],
        out_shape=[jax.ShapeDtypeStruct((t, d), F32), jax.ShapeDtypeStruct((t, d), BF16),
                   jax.ShapeDtypeStruct((2, t), I32), jax.ShapeDtypeStruct((2, t), F32)],
        compiler_params=pltpu.CompilerParams(
            dimension_semantics=("parallel",), vmem_limit_bytes=VMEM_LIMIT),
    )(y_ssd, y_att, w, w, h, ln_g.astype(F32).reshape(1, d), ln_b.astype(F32).reshape(1, d),
      rw_hi, rw_lo, router_b.astype(F32).reshape(N_EXPERTS, 1))


def _expert_kernel(be_ref, nu_ref, st_ref, h_hbm, *rest):
    ns = MOE_W_SPLIT
    wg_refs, wu_refs, wd_refs = rest[0:ns], rest[ns:2 * ns], rest[2 * ns:3 * ns]
    o_ref, xbuf, wg_s, wu_s, wd_s, sem = rest[3 * ns:]
    i = pl.program_id(0)
    nb = pl.num_programs(0)
    slot = lax.rem(i, 2)
    nxt_base = jnp.minimum(i + 1, nb - 1) * MOE_BLOCK

    def row_copy(tok, r, s):
        return pltpu.make_async_copy(h_hbm.at[pl.ds(tok, 1), :], xbuf.at[s, pl.ds(r, 1), :],
                                     sem.at[s])

    def issue_rolled(base, s):
        def body(r, carry):
            row_copy(st_ref[base + r], r, s).start()
            return carry
        lax.fori_loop(0, MOE_BLOCK, body, 0)

    def wait_rows(s):
        pltpu.make_async_copy(h_hbm.at[pl.ds(0, MOE_BLOCK), :], xbuf.at[s], sem.at[s]).wait()

    @pl.when(i == 0)
    def _():
        issue_rolled(0, 0)

    wait_rows(slot)

    @pl.when(i < nu_ref[0])
    def _():
        prev = be_ref[jnp.maximum(i - 1, 0)]

        @pl.when((i == 0) | (be_ref[i] != prev))
        def _():
            for refs, dst in ((wg_refs, wg_s), (wu_refs, wu_s), (wd_refs, wd_s)):
                rows = dst.shape[0] // ns
                for q in range(ns):
                    dst[q * rows:(q + 1) * rows, :] = refs[q][...].astype(BF16)

        for r in range(MOE_BLOCK):
            row_copy(st_ref[nxt_base + r], r, 1 - slot).start()
        x = xbuf[slot].astype(BF16)
        g = jnp.dot(x, wg_s[...], preferred_element_type=F32)
        u = jnp.dot(x, wu_s[...], preferred_element_type=F32)
        a = (g * _sigmoid(g) * u).astype(BF16)
        o_ref[...] = jnp.dot(a, wd_s[...], preferred_element_type=F32)

    @pl.when(i >= nu_ref[0])
    def _():
        issue_rolled(nxt_base, 1 - slot)
        o_ref[...] = jnp.zeros_like(o_ref)

    @pl.when(i == nb - 1)
    def _():
        wait_rows(1 - slot)


def _experts(block_expert, n_used, slot_tok, h1, w_gate, w_up, w_down, layer):
    t, d = h1.shape
    cap = slot_tok.shape[0]
    nb = cap // MOE_BLOCK
    ns = MOE_W_SPLIT

    def slabs(rows, cols):
        return [pl.BlockSpec((None, None, rows // ns, cols),
                             lambda i, be, nu, st, q=q: (layer, be[i], q, 0)) for q in range(ns)]

    return pl.pallas_call(
        _expert_kernel,
        name="moe_experts",
        grid_spec=pltpu.PrefetchScalarGridSpec(
            num_scalar_prefetch=3,
            grid=(nb,),
            in_specs=([pl.BlockSpec(memory_space=pl.ANY)] + slabs(d, D_FF) + slabs(d, D_FF)
                      + slabs(D_FF, d)),
            out_specs=pl.BlockSpec((MOE_BLOCK, d), lambda i, be, nu, st: (i, 0)),
            scratch_shapes=[pltpu.VMEM((2, MOE_BLOCK, d), F32),
                            pltpu.VMEM((d, D_FF), BF16), pltpu.VMEM((d, D_FF), BF16),
                            pltpu.VMEM((D_FF, d), BF16), pltpu.SemaphoreType.DMA((2,))]),
        out_shape=jax.ShapeDtypeStruct((cap, d), F32),
        compiler_params=pltpu.CompilerParams(
            dimension_semantics=("arbitrary",), vmem_limit_bytes=VMEM_LIMIT),
    )(block_expert, n_used, slot_tok, h1, *([w_gate] * ns + [w_up] * ns + [w_down] * ns))


def _moe_plan(eidx, t):
    flat_e = eidx.reshape(2 * t)
    onehot = (flat_e[:, None] == jnp.arange(N_EXPERTS, dtype=I32)[None, :]).astype(I32)
    cs = jnp.cumsum(onehot, axis=0)
    rank = jnp.sum(onehot * cs, axis=1) - 1
    counts = cs[-1]
    padded = (counts + MOE_BLOCK - 1) // MOE_BLOCK * MOE_BLOCK
    pad_end = jnp.cumsum(padded)
    pad_start = pad_end - padded
    dest = (pad_start[flat_e] + rank).astype(I32)
    cap = 2 * t + N_EXPERTS * MOE_BLOCK
    nb = cap // MOE_BLOCK
    block_start = jnp.arange(nb, dtype=I32) * MOE_BLOCK
    block_expert = jnp.minimum(
        jnp.sum((pad_end[None, :] <= block_start[:, None]).astype(I32), axis=1), N_EXPERTS - 1)
    n_used = (pad_end[-1] // MOE_BLOCK).astype(I32).reshape(1)
    tok = jnp.tile(jnp.arange(t, dtype=I32), 2)
    slot_tok = jnp.zeros((cap,), I32).at[dest].set(tok, unique_indices=True)
    return dest, block_expert, n_used, slot_tok


def _ple_kernel(dest_ref, h1_ref, h1b_ref, yb_hbm, gt_ref, p_ref, wg_ref, wp_ref, g_ref, b_ref,
                o_ref, ob_ref, gbuf, sem):
    i = pl.program_id(0)
    n = pl.num_programs(0)
    tm = h1_ref.shape[0]
    t = n * tm
    slot = lax.rem(i, 2)
    nxt_base = jnp.minimum(i + 1, n - 1) * tm

    def row_copy(src_row, k, r, s):
        return pltpu.make_async_copy(yb_hbm.at[pl.ds(src_row, 1), :],
                                     gbuf.at[s, pl.ds(k * tm + r, 1), :], sem.at[s])

    def wait_tile(s):
        pltpu.make_async_copy(yb_hbm.at[pl.ds(0, 2 * tm), :], gbuf.at[s], sem.at[s]).wait()

    @pl.when(i == 0)
    def _():
        def body(r, carry):
            for k in range(2):
                row_copy(dest_ref[k * t + r], k, r, 0).start()
            return carry
        lax.fori_loop(0, tm, body, 0)

    wait_tile(slot)
    for r in range(tm):
        for k in range(2):
            row_copy(dest_ref[k * t + nxt_base + r], k, r, 1 - slot).start()
    ffn = gt_ref[:, 0:1] * gbuf[slot, 0:tm, :] + gt_ref[:, 1:2] * gbuf[slot, tm:2 * tm, :]
    gate = _sigmoid(jnp.dot(h1b_ref[...], wg_ref[...], preferred_element_type=F32))
    pe = jnp.dot(p_ref[...].astype(BF16), wp_ref[...], preferred_element_type=F32)
    u = ALPHA * h1_ref[...] + ffn + gate * pe
    h2 = _layer_norm_rows(u, g_ref[...], b_ref[...])
    o_ref[...] = h2
    ob_ref[...] = h2.astype(BF16)

    @pl.when(i == n - 1)
    def _():
        wait_tile(1 - slot)


def _ple_ln(dest, h1, h1b, yb, gates_t, p, ple_gate_w, ple_w, ln_g, ln_b, tm):
    t, d = h1.shape
    pd = p.shape[1]
    row = lambda w: pl.BlockSpec((tm, w), lambda i, dest: (i, 0))
    const = lambda shape: pl.BlockSpec(shape, lambda i, dest: (0, 0), pipeline_mode=pl.Buffered(1))
    return pl.pallas_call(
        _ple_kernel,
        name="ple_ln",
        grid_spec=pltpu.PrefetchScalarGridSpec(
            num_scalar_prefetch=1,
            grid=(t // tm,),
            in_specs=[row(d), row(d), pl.BlockSpec(memory_space=pl.ANY), row(2), row(pd),
                      const((d, d)), const((pd, d)), const((1, d)), const((1, d))],
            out_specs=[row(d), row(d)],
            scratch_shapes=[pltpu.VMEM((2, 2 * tm, d), F32), pltpu.SemaphoreType.DMA((2,))]),
        out_shape=[jax.ShapeDtypeStruct((t, d), F32), jax.ShapeDtypeStruct((t, d), BF16)],
        compiler_params=pltpu.CompilerParams(
            dimension_semantics=("arbitrary",), vmem_limit_bytes=VMEM_LIMIT),
    )(dest, h1, h1b, yb, gates_t, p, ple_gate_w.astype(BF16), ple_w.astype(BF16),
      ln_g.astype(F32).reshape(1, d), ln_b.astype(F32).reshape(1, d))


def _layer(h, hb, p_i, w_in, layer, conv_w, conv_b, dt_bias, a_log, d_skip, ssm_norm_w,
           idx_k_norm_g, idx_k_norm_b, w_out, ln1_g, ln1_b, router_w, router_b,
           w_gate, w_up, w_down, ple_w, ple_gate_w, ln2_g, ln2_b, bsz, seq):
    t = bsz * seq
    tm = min(1024, t)
    row0 = layer * (OFF_IW + IDX_HEADS)
    za = _proj(hb, w_in, row0, OFF_Z, OFF_DT - OFF_Z, F32, False, tm)
    kk = _proj(hb, w_in, row0, OFF_Q + ATT_WIDTH, ATT_WIDTH, BF16, False, tm)
    qt = _proj(hb, w_in, row0, OFF_Q, ATT_WIDTH, BF16, True, tm)
    vt = _proj(hb, w_in, row0, OFF_Q + 2 * ATT_WIDTH, ATT_WIDTH, BF16, True, tm)
    iqt = _proj(hb, w_in, row0, OFF_IQ, IDX_HEADS * IDX_DIM, BF16, True, tm)
    sm, smt = _proj_small(hb, w_in, row0, tm)

    y_ssd = _ssd(za, sm, conv_w, conv_b, dt_bias, a_log, d_skip, ssm_norm_w, bsz, seq)
    y_att = _dsa(qt, iqt, vt, kk, sm, smt, idx_k_norm_g, idx_k_norm_b, bsz, seq)
    h1, h1b, eidx, gates = _outproj_ln_router(
        y_ssd, y_att, w_out, h, ln1_g, ln1_b, router_w, router_b, min(256, t))

    dest, block_expert, n_used, slot_tok = _moe_plan(eidx, t)
    yb = _experts(block_expert, n_used, slot_tok, h1, w_gate, w_up, w_down, layer)
    return _ple_ln(dest, h1, h1b, yb, gates.T, p_i, ple_gate_w, ple_w, ln2_g, ln2_b, min(256, t))


def kernel(x, p, w_in, conv_w, conv_b, dt_bias, a_log, d_skip, ssm_norm_w, idx_k_norm_g,
           idx_k_norm_b, w_out, ln1_g, ln1_b, router_w, router_b, w_gate, w_up, w_down,
           ple_w, ple_gate_w, ln2_g, ln2_b):
    bsz, seq, d = x.shape
    t = bsz * seq
    h = x.reshape(t, d).astype(F32)
    hb = h.astype(BF16)
    w_in_t = jnp.swapaxes(w_in, 1, 2).reshape(-1, d)
    for i in range(w_in.shape[0]):
        h, hb = _layer(h, hb, p[i].reshape(t, -1), w_in_t, i, conv_w[i], conv_b[i], dt_bias[i],
                       a_log[i], d_skip[i], ssm_norm_w[i], idx_k_norm_g[i], idx_k_norm_b[i],
                       w_out[i], ln1_g[i], ln1_b[i], router_w, router_b, w_gate, w_up,
                       w_down, ple_w[i], ple_gate_w[i], ln2_g[i], ln2_b[i], bsz, seq)
    return h.reshape(bsz, seq, d).astype(x.dtype)
```

```python
import functools

import jax
import jax.numpy as jnp
from jax import lax
from jax.experimental import pallas as pl
from jax.experimental.pallas import tpu as pltpu

F32 = jnp.float32
BF16 = jnp.bfloat16
I32 = jnp.int32

SSD_WIDTH = 1024
SSD_HEADDIM = 64
SSD_HEADS = 16
SSD_STATE = 128
SSD_GROUPS = 2
CONV_WIDTH = 4
CHUNK = 128
CONV_DIM = SSD_WIDTH + 2 * SSD_GROUPS * SSD_STATE
ATT_HEADS = 8
ATT_HEADDIM = 128
ATT_WIDTH = ATT_HEADS * ATT_HEADDIM
IDX_HEADS = 16
IDX_DIM = 64
TOPK_MAX = 256
Q_BLOCK = 128
N_EXPERTS = 32
N_EXPERT_GROUPS = 8
EXPERTS_PER_GROUP = 4
D_FF = 512
ALPHA = (2.0 * 2) ** 0.25
LN_EPS = 1e-5

OFF_Z = 0
OFF_XBC = OFF_Z + SSD_WIDTH
OFF_DT = OFF_XBC + CONV_DIM
OFF_Q = OFF_DT + SSD_HEADS
OFF_IQ = OFF_Q + 3 * ATT_WIDTH
OFF_IK = OFF_IQ + IDX_HEADS * IDX_DIM
OFF_IW = OFF_IK + IDX_DIM
SM_DT = 0
SM_IK = SSD_HEADS
SM_IW = SM_IK + IDX_DIM
SM_W = 128

V7X_LANES = 128
VMEM_LIMIT = 56 * 1024 * 1024
FLOAT_LOWEST = -3.0e38
SEARCH_PROBES_PER_CHECK = 3
SEARCH_MAX_ITERS = 200
KEY_CHUNK = 256
DSA_QB = 256
DSA_KV_SPLIT = 4
MOE_BLOCK = 256
MOE_W_SPLIT = 4


def _sigmoid(x):
    return 1.0 / (1.0 + jnp.exp(-x))


def _softplus(x):
    return jnp.maximum(x, 0.0) + jnp.log(1.0 + jnp.exp(-jnp.abs(x)))


def _nt_dot(a, b, **kw):
    return lax.dot_general(a, b, (((1,), (1,)), ((), ())), preferred_element_type=F32, **kw)


def _store_token_rows(ref, val):
    n, d = val.shape
    parts = d // V7X_LANES
    for j in range(parts):
        ref[pl.ds(j, n, stride=parts), :] = val[:, j * V7X_LANES:(j + 1) * V7X_LANES]


def _load_token_rows(ref, first, n, parts):
    return jnp.concatenate(
        [ref[pl.ds(first * parts + j, n, stride=parts), :] for j in range(parts)], axis=1)


def _layer_norm_rows(u, g, b):
    mu = jnp.mean(u, axis=-1, keepdims=True)
    d = u - mu
    var = jnp.mean(d * d, axis=-1, keepdims=True)
    return d * lax.rsqrt(var + LN_EPS) * g + b


PROJ_ROWS = 256


def _proj_kernel(x_ref, wt_ref, o_ref, w_s, *, transposed):
    tn, k = wt_ref.shape

    @pl.when(pl.program_id(1) == 0)
    def _():
        if transposed:
            w_s[...] = wt_ref[...].astype(BF16)
        else:
            def stage(r, carry):
                r0 = pl.multiple_of(r * PROJ_ROWS, PROJ_ROWS)
                w_s[pl.ds(r0, PROJ_ROWS), :] = wt_ref[:, pl.ds(r0, PROJ_ROWS)].T.astype(BF16)
                return carry

            lax.fori_loop(0, k // PROJ_ROWS, stage, 0)

    if transposed:
        o_ref[...] = _nt_dot(w_s[...], x_ref[...]).astype(o_ref.dtype)
    else:
        o_ref[...] = jnp.dot(x_ref[...], w_s[...], preferred_element_type=F32).astype(o_ref.dtype)


def _proj(x, w_in_t, row0, c0, width, out_dtype, transposed, tm, tn=512):
    m, k = x.shape
    assert (row0 + c0) % 8 == 0 and width % tn == 0 and m % tm == 0
    in_specs = [pl.BlockSpec((tm, k), lambda j, i: (i, 0)),
                pl.BlockSpec((pl.Element(tn), pl.Element(k)),
                             lambda j, i: (pl.multiple_of(row0 + c0 + j * tn, 8), 0))]
    if transposed:
        out_spec = pl.BlockSpec((tn, tm), lambda j, i: (j, i))
        out_shape = jax.ShapeDtypeStruct((width, m), out_dtype)
        scratch = pltpu.VMEM((tn, k), BF16)
    else:
        out_spec = pl.BlockSpec((tm, tn), lambda j, i: (i, j))
        out_shape = jax.ShapeDtypeStruct((m, width), out_dtype)
        scratch = pltpu.VMEM((k, tn), BF16)
    return pl.pallas_call(
        functools.partial(_proj_kernel, transposed=transposed),
        name="in_proj_t" if transposed else "in_proj",
        grid=(width // tn, m // tm),
        in_specs=in_specs,
        out_specs=out_spec,
        out_shape=out_shape,
        scratch_shapes=[scratch],
        compiler_params=pltpu.CompilerParams(
            dimension_semantics=("parallel", "arbitrary"), vmem_limit_bytes=VMEM_LIMIT),
    )(x, w_in_t)


def _proj_small_kernel(x_ref, wa_ref, wb_ref, o_ref, ot_ref, w_s, wt_s):
    @pl.when(pl.program_id(0) == 0)
    def _():
        k = wa_ref.shape[1]
        tail = SM_W - IDX_DIM - IDX_HEADS
        wt = jnp.concatenate(
            [wa_ref[0:SSD_HEADS, :], wb_ref[tail:SM_W, :],
             jnp.zeros((SM_W - SSD_HEADS - IDX_DIM - IDX_HEADS, k), F32)], axis=0)
        wt_s[...] = wt.astype(BF16)
        w_s[...] = wt.T.astype(BF16)

    o_ref[...] = jnp.dot(x_ref[...], w_s[...], preferred_element_type=F32)
    ot_ref[...] = _nt_dot(wt_s[...], x_ref[...])


def _proj_small(x, w_in_t, row0, tm):
    m, k = x.shape
    d_in = OFF_IW + IDX_HEADS
    assert OFF_IW - OFF_IK == IDX_DIM and (row0 + OFF_DT) % 8 == 0 and (row0 + d_in) % 8 == 0
    return pl.pallas_call(
        _proj_small_kernel,
        name="in_proj_small",
        grid=(m // tm,),
        in_specs=[pl.BlockSpec((tm, k), lambda i: (i, 0)),
                  pl.BlockSpec((pl.Element(SM_W), pl.Element(k)), lambda i: (row0 + OFF_DT, 0)),
                  pl.BlockSpec((pl.Element(SM_W), pl.Element(k)),
                               lambda i: (row0 + d_in - SM_W, 0))],
        out_specs=[pl.BlockSpec((tm, SM_W), lambda i: (i, 0)),
                   pl.BlockSpec((SM_W, tm), lambda i: (0, i))],
        out_shape=[jax.ShapeDtypeStruct((m, SM_W), F32), jax.ShapeDtypeStruct((SM_W, m), F32)],
        scratch_shapes=[pltpu.VMEM((k, SM_W), BF16), pltpu.VMEM((SM_W, k), BF16)],
        compiler_params=pltpu.CompilerParams(
            dimension_semantics=("arbitrary",), vmem_limit_bytes=VMEM_LIMIT),
    )(x, w_in_t, w_in_t)


def _ssd_kernel(za_ref, sm_ref, cw_ref, cb_ref, dtb_ref, a_ref, dexp_ref, nw_ref, e_ref,
                o_ref, cbuf, state, ybuf):
    c = pl.program_id(1)
    L = CHUNK

    @pl.when(c == 0)
    def _():
        cbuf[0:8, :] = jnp.zeros((8, CONV_DIM), F32)
        state[...] = jnp.zeros_like(state)

    cbuf[8:8 + L, :] = za_ref[:, SSD_WIDTH:SSD_WIDTH + CONV_DIM]
    acc = cb_ref[...] + cw_ref[0:1, :] * cbuf[5:5 + L, :]
    acc = acc + cw_ref[1:2, :] * cbuf[6:6 + L, :]
    acc = acc + cw_ref[2:3, :] * cbuf[7:7 + L, :]
    acc = acc + cw_ref[3:4, :] * cbuf[8:8 + L, :]
    cbuf[0:8, :] = cbuf[L:L + 8, :]
    xbc = acc * _sigmoid(acc)
    xs = xbc[:, 0:SSD_WIDTH]
    bm = xbc[:, SSD_WIDTH:SSD_WIDTH + SSD_GROUPS * SSD_STATE]
    cm = xbc[:, SSD_WIDTH + SSD_GROUPS * SSD_STATE:CONV_DIM]

    dt = _softplus(sm_ref[...] + dtb_ref[...])
    da = dt * a_ref[...]
    row = lax.broadcasted_iota(I32, (L, L), 0)
    col = lax.broadcasted_iota(I32, (L, L), 1)
    causal = col <= row
    tri = jnp.where(causal, 1.0, 0.0).astype(F32)
    a_cs = jnp.dot(tri, da, preferred_element_type=F32, precision=lax.Precision.HIGHEST)
    a_cs_t = a_cs.T
    expa = jnp.exp(a_cs)
    dte = jnp.exp(a_cs[L - 1:L, :] - a_cs)
    stacked = jnp.concatenate([dt, expa, dte], axis=0)
    expd = jnp.dot(stacked, e_ref[...], preferred_element_type=F32,
                   precision=lax.Precision.HIGHEST)
    dt_x = expd[0:L, :]
    expa_x = expd[L:2 * L, :]
    dte_x = expd[2 * L:3 * L, :]
    xdt = xs * dt_x
    hpg = SSD_HEADS // SSD_GROUPS
    gw = hpg * SSD_HEADDIM
    for g in range(SSD_GROUPS):
        cm_g = cm[:, g * SSD_STATE:(g + 1) * SSD_STATE].astype(BF16)
        bm_gf = bm[:, g * SSD_STATE:(g + 1) * SSD_STATE]
        bm_g = bm_gf.astype(BF16)
        cb = _nt_dot(cm_g, bm_g)
        for hh in range(hpg):
            h = g * hpg + hh
            seg = a_cs[:, h:h + 1] - a_cs_t[h:h + 1, :]
            dec = jnp.exp(jnp.where(causal, seg, -jnp.inf))
            mh = (cb * dec).astype(BF16)
            xh = xdt[:, h * SSD_HEADDIM:(h + 1) * SSD_HEADDIM].astype(BF16)
            ybuf[:, h * SSD_HEADDIM:(h + 1) * SSD_HEADDIM] = jnp.dot(
                mh, xh, preferred_element_type=F32)
        st_g = state[:, g * gw:(g + 1) * gw]
        y_off = jnp.dot(cm_g, st_g.astype(BF16), preferred_element_type=F32)
        ybuf[:, g * gw:(g + 1) * gw] = (ybuf[:, g * gw:(g + 1) * gw]
                                        + y_off * expa_x[:, g * gw:(g + 1) * gw])
        xw = (xdt[:, g * gw:(g + 1) * gw] * dte_x[:, g * gw:(g + 1) * gw]).astype(BF16)
        upd = jnp.dot(bm_gf.T.astype(BF16), xw, preferred_element_type=F32)
        state[:, g * gw:(g + 1) * gw] = st_g * expa_x[L - 1:L, g * gw:(g + 1) * gw] + upd

    y = ybuf[...] + xs * dexp_ref[...]
    z = za_ref[:, 0:SSD_WIDTH]
    yg = y * (z * _sigmoid(z))
    ms = jnp.mean(yg * yg, axis=-1, keepdims=True)
    o_ref[...] = (yg * lax.rsqrt(ms + LN_EPS) * nw_ref[...]).astype(o_ref.dtype)


def _ssd(za, sm, conv_w, conv_b, dt_bias, a_log, d_skip, ssm_norm_w, bsz, seq):
    nc = seq // CHUNK
    t = bsz * seq
    pad = SM_W - SSD_HEADS
    dtb = jnp.pad(dt_bias.astype(F32), (0, pad)).reshape(1, SM_W)
    a = jnp.pad(-jnp.exp(a_log.astype(F32)), (0, pad)).reshape(1, SM_W)
    dexp = jnp.repeat(d_skip.astype(F32), SSD_HEADDIM).reshape(1, SSD_WIDTH)
    e = (jnp.arange(SSD_WIDTH)[None, :] // SSD_HEADDIM == jnp.arange(SM_W)[:, None]).astype(F32)
    const = lambda shape: pl.BlockSpec(shape, lambda b, c: (0, 0))
    return pl.pallas_call(
        _ssd_kernel,
        name="ssd",
        grid=(bsz, nc),
        in_specs=[pl.BlockSpec((CHUNK, SSD_WIDTH + CONV_DIM), lambda b, c: (b * nc + c, 0)),
                  pl.BlockSpec((CHUNK, SM_W), lambda b, c: (b * nc + c, 0)),
                  const((CONV_WIDTH, CONV_DIM)), const((1, CONV_DIM)), const((1, SM_W)),
                  const((1, SM_W)), const((1, SSD_WIDTH)), const((1, SSD_WIDTH)),
                  const((SM_W, SSD_WIDTH))],
        out_specs=pl.BlockSpec((CHUNK, SSD_WIDTH), lambda b, c: (b * nc + c, 0)),
        out_shape=jax.ShapeDtypeStruct((t, SSD_WIDTH), BF16),
        scratch_shapes=[pltpu.VMEM((CHUNK + 8, CONV_DIM), F32),
                        pltpu.VMEM((SSD_STATE, SSD_WIDTH), F32),
                        pltpu.VMEM((CHUNK, SSD_WIDTH), F32)],
        compiler_params=pltpu.CompilerParams(
            dimension_semantics=("parallel", "arbitrary"), vmem_limit_bytes=VMEM_LIMIT),
    )(za, sm, conv_w.astype(F32), conv_b.astype(F32).reshape(1, CONV_DIM), dtb, a, dexp,
      ssm_norm_w.astype(F32).reshape(1, SSD_WIDTH), e)


def _dsa_kernel(qt_ref, iqt_ref, *rest, topk):
    ns = DSA_KV_SPLIT
    vt_refs, k_refs = rest[0:ns], rest[ns:2 * ns]
    smb_ref, smt_ref, g_ref, b_ref, o_ref, ikn, keys, acc_s, s_s, p_s, bias_s = rest[2 * ns:]
    slab = ATT_WIDTH // ns
    qi = pl.program_id(1)
    QB = o_ref.shape[0]
    KC = KEY_CHUNK

    @pl.when(qi == 0)
    def _():
        ik = smb_ref[:, SM_IK:SM_IK + IDX_DIM]
        ikn[...] = _layer_norm_rows(ik, g_ref[...], b_ref[...]).astype(BF16)

    nkc = (qi * QB + QB - 1) // KC + 1
    w = smt_ref[SM_IW:SM_IW + IDX_HEADS, :] * (IDX_DIM ** -0.5 * IDX_HEADS ** -0.5)
    qpos = qi * QB + lax.broadcasted_iota(I32, (KC, QB), 1)
    kiota = lax.broadcasted_iota(I32, (KC, QB), 0)

    def score_chunk(c, carry):
        rmax, rmin = carry
        off = pl.multiple_of(c * KC, KC)
        kc = ikn[pl.ds(off, KC), :]
        acc = jnp.zeros((KC, QB), F32)
        for h in range(IDX_HEADS):
            d = jnp.dot(kc, iqt_ref[h * IDX_DIM:(h + 1) * IDX_DIM, :], preferred_element_type=F32)
            acc = acc + w[h:h + 1, :] * jnp.maximum(d, 0.0)
        vis = off + kiota <= qpos
        sv = jnp.where(vis, acc, -jnp.inf)
        keys[c] = sv
        rmax = jnp.maximum(rmax, jnp.max(sv, axis=0, keepdims=True))
        rmin = jnp.minimum(rmin, jnp.min(jnp.where(vis, acc, jnp.inf), axis=0, keepdims=True))
        return rmax, rmin

    rmax, rmin = lax.fori_loop(
        0, nkc, score_chunk,
        (jnp.full((1, QB), -jnp.inf, F32), jnp.full((1, QB), jnp.inf, F32)))

    kf = float(topk)

    def count_ge(mid):
        def count_chunk(c, acc):
            m = jnp.where(keys[c] >= mid, 1.0, 0.0)
            return acc + jnp.sum(m.reshape(KC // 8, 8, QB), axis=0)

        acc = lax.fori_loop(0, nkc, count_chunk, jnp.zeros((8, QB), F32))
        return jnp.sum(acc, axis=0, keepdims=True)

    nvis = (qi * QB + lax.broadcasted_iota(I32, (1, QB), 1) + 1).astype(F32)
    small = nvis <= kf

    def search_cond(st):
        return (st[0] > 0.0) & (st[1] < SEARCH_MAX_ITERS)

    def search_body(st):
        _, it, lo, hi, clo = st
        for _ in range(SEARCH_PROBES_PER_CHECK):
            mid = 0.5 * lo + 0.5 * hi
            splittable = (mid > lo) & (mid < hi)
            c = count_ge(mid)
            up = splittable & (c >= kf)
            down = splittable & (c < kf)
            lo = jnp.where(up, mid, lo)
            clo = jnp.where(up, c, clo)
            hi = jnp.where(down, mid, hi)
        active = jnp.logical_not(small) & splittable & (clo != kf)
        return jnp.sum(jnp.where(active, 1.0, 0.0)), it + 1, lo, hi, clo

    hi0 = rmax + jnp.abs(rmax) * 1e-6 + 1e-37
    st = lax.while_loop(search_cond, search_body,
                        (jnp.float32(1.0), jnp.int32(0), rmin, hi0, nvis))
    thr = jnp.where(small, FLOAT_LOWEST, st[2])

    scale = ATT_HEADDIM ** -0.5
    acc_s[...] = jnp.zeros_like(acc_s)

    def att_chunk(c, carry):
        ms, ls = carry
        off = pl.multiple_of(c * KC, KC)
        bias_s[...] = jnp.where(keys[c] >= thr, 0.0, -1e30)
        for h in range(ATT_HEADS):
            lo, hi = h * ATT_HEADDIM, (h + 1) * ATT_HEADDIM
            s_s[h] = jnp.dot(k_refs[lo // slab][pl.ds(off, KC), lo % slab:lo % slab + ATT_HEADDIM],
                             qt_ref[lo:hi, :],
                             preferred_element_type=F32)
        new_ms, new_ls, alphas = [], [], []
        for h in range(ATT_HEADS):
            s = s_s[h] * scale + bias_s[...]
            m_new = jnp.maximum(ms[h], jnp.max(s, axis=0, keepdims=True))
            alpha = jnp.exp(ms[h] - m_new)
            p = jnp.exp(s - m_new)
            new_ls.append(alpha * ls[h] + jnp.sum(p, axis=0, keepdims=True))
            new_ms.append(m_new)
            alphas.append(alpha)
            p_s[h] = p.astype(BF16)
        for h in range(ATT_HEADS):
            lo, hi = h * ATT_HEADDIM, (h + 1) * ATT_HEADDIM
            pv = jnp.dot(vt_refs[lo // slab][lo % slab:lo % slab + ATT_HEADDIM, pl.ds(off, KC)],
                         p_s[h],
                         preferred_element_type=F32)
            acc_s[h] = alphas[h] * acc_s[h] + pv
        return tuple(new_ms), tuple(new_ls)

    m0 = tuple(jnp.full((1, QB), -1e30, F32) for _ in range(ATT_HEADS))
    l0 = tuple(jnp.zeros((1, QB), F32) for _ in range(ATT_HEADS))
    _, ls = lax.fori_loop(0, nkc, att_chunk, (m0, l0))
    for h in range(ATT_HEADS):
        o_ref[:, h * ATT_HEADDIM:(h + 1) * ATT_HEADDIM] = (acc_s[h] / ls[h]).T.astype(o_ref.dtype)


def _dsa(qt, iqt, vt, kk, sm, smt, g, b, bsz, seq):
    qb = min(DSA_QB, seq)
    nq = seq // qb
    t = bsz * seq
    topk = min(TOPK_MAX, seq // 4)
    single = pl.Buffered(1)
    ns = DSA_KV_SPLIT
    slab = ATT_WIDTH // ns
    vt_specs = [pl.BlockSpec((slab, seq), lambda bi, qi, q=q: (q, bi), pipeline_mode=single)
                for q in range(ns)]
    k_specs = [pl.BlockSpec((seq, slab), lambda bi, qi, q=q: (bi, q), pipeline_mode=single)
               for q in range(ns)]
    return pl.pallas_call(
        functools.partial(_dsa_kernel, topk=topk),
        name="dsa",
        grid=(bsz, nq),
        in_specs=[pl.BlockSpec((ATT_WIDTH, qb), lambda bi, qi: (0, bi * nq + qi)),
                  pl.BlockSpec((IDX_HEADS * IDX_DIM, qb), lambda bi, qi: (0, bi * nq + qi)),
                  *vt_specs, *k_specs,
                  pl.BlockSpec((seq, SM_W), lambda bi, qi: (bi, 0), pipeline_mode=single),
                  pl.BlockSpec((SM_W, qb), lambda bi, qi: (0, bi * nq + qi)),
                  pl.BlockSpec((1, IDX_DIM), lambda bi, qi: (0, 0)),
                  pl.BlockSpec((1, IDX_DIM), lambda bi, qi: (0, 0))],
        out_specs=pl.BlockSpec((qb, ATT_WIDTH), lambda bi, qi: (bi * nq + qi, 0)),
        out_shape=jax.ShapeDtypeStruct((t, ATT_WIDTH), BF16),
        scratch_shapes=[pltpu.VMEM((seq, IDX_DIM), BF16),
                        pltpu.VMEM((seq // KEY_CHUNK, KEY_CHUNK, qb), F32),
                        pltpu.VMEM((ATT_HEADS, ATT_HEADDIM, qb), F32),
                        pltpu.VMEM((ATT_HEADS, KEY_CHUNK, qb), F32),
                        pltpu.VMEM((ATT_HEADS, KEY_CHUNK, qb), BF16),
                        pltpu.VMEM((KEY_CHUNK, qb), F32)],
        compiler_params=pltpu.CompilerParams(
            dimension_semantics=("parallel", "arbitrary"), vmem_limit_bytes=VMEM_LIMIT),
    )(qt, iqt, *([vt] * ns), *([kk] * ns), sm, smt, g.astype(F32).reshape(1, IDX_DIM),
      b.astype(F32).reshape(1, IDX_DIM))


def _route_rows(logits_t, rb):
    aff = _sigmoid(logits_t)
    biased = aff + rb
    rb_rows = [biased[r:r + 1, :] for r in range(N_EXPERTS)]
    ra_rows = [aff[r:r + 1, :] for r in range(N_EXPERTS)]
    epg = EXPERTS_PER_GROUP
    gs = []
    for g in range(N_EXPERT_GROUPS):
        a, b, c, d = rb_rows[epg * g:epg * g + epg]
        p, q = jnp.maximum(a, b), jnp.minimum(a, b)
        r, s = jnp.maximum(c, d), jnp.minimum(c, d)
        gs.append(jnp.maximum(p, r) + jnp.maximum(jnp.minimum(p, r), jnp.maximum(q, s)))
    best = gs[0]
    bidx = jnp.zeros_like(best, dtype=I32)
    for g in range(1, N_EXPERT_GROUPS):
        better = gs[g] > best
        best = jnp.where(better, gs[g], best)
        bidx = jnp.where(better, g, bidx)
    vb = [rb_rows[j] for j in range(epg)]
    va = [ra_rows[j] for j in range(epg)]
    for g in range(1, N_EXPERT_GROUPS):
        pick = bidx == g
        vb = [jnp.where(pick, rb_rows[epg * g + j], vb[j]) for j in range(epg)]
        va = [jnp.where(pick, ra_rows[epg * g + j], va[j]) for j in range(epg)]
    t1, a1, i1 = vb[0], va[0], jnp.zeros_like(bidx)
    for j in range(1, epg):
        better = vb[j] > t1
        t1 = jnp.where(better, vb[j], t1)
        a1 = jnp.where(better, va[j], a1)
        i1 = jnp.where(better, j, i1)
    t2 = jnp.full_like(t1, -jnp.inf)
    a2 = jnp.zeros_like(a1)
    i2 = jnp.zeros_like(i1)
    for j in range(epg):
        vj = jnp.where(i1 == j, -jnp.inf, vb[j])
        better = vj > t2
        t2 = jnp.where(better, vj, t2)
        a2 = jnp.where(better, va[j], a2)
        i2 = jnp.where(better, j, i2)
    den = a1 + a2
    return bidx * epg + i1, bidx * epg + i2, a1 / den, a2 / den


def _outproj_kernel(ys_ref, ya_ref, w1_ref, w2_ref, h_ref, g_ref, b_ref, rwh_ref, rwl_ref, rb_ref,
                    h1_ref, h1b_ref, h1r_ref, eidx_ref, gate_ref):
    mix = jnp.dot(ys_ref[...], w1_ref[...], preferred_element_type=F32)
    mix = mix + jnp.dot(ya_ref[...], w2_ref[...], preferred_element_type=F32)
    h1 = _layer_norm_rows(ALPHA * h_ref[...] + mix, g_ref[...], b_ref[...])
    h1_ref[...] = h1
    _store_token_rows(h1r_ref, h1)
    h1_hi = h1.astype(BF16)
    h1b_ref[...] = h1_hi
    h1_lo = (h1 - h1_hi.astype(F32)).astype(BF16)
    logits = jnp.dot(h1_hi, rwh_ref[...], preferred_element_type=F32)
    logits = logits + jnp.dot(h1_hi, rwl_ref[...], preferred_element_type=F32)
    logits = logits + jnp.dot(h1_lo, rwh_ref[...], preferred_element_type=F32)
    logits_t = logits.T[0:N_EXPERTS, :]
    e1, e2, g1, g2 = _route_rows(logits_t, rb_ref[...])
    eidx_ref[0:1, :] = e1
    eidx_ref[1:2, :] = e2
    gate_ref[0:1, :] = g1
    gate_ref[1:2, :] = g2


def _outproj_ln_router(y_ssd, y_att, w_out, h, ln_g, ln_b, router_w, router_b, tm):
    t, d = h.shape
    w = w_out.astype(BF16)
    rw = jnp.pad(router_w.astype(F32), ((0, 0), (0, V7X_LANES - N_EXPERTS)))
    rw_hi = rw.astype(BF16)
    rw_lo = (rw - rw_hi.astype(F32)).astype(BF16)
    const = lambda shape: pl.BlockSpec(shape, lambda i: (0, 0))
    return pl.pallas_call(
        _outproj_kernel,
        name="outproj_ln_router",
        grid=(t // tm,),
        in_specs=[pl.BlockSpec((tm, SSD_WIDTH), lambda i: (i, 0)),
                  pl.BlockSpec((tm, ATT_WIDTH), lambda i: (i, 0)),
                  pl.BlockSpec((SSD_WIDTH, d), lambda i: (0, 0)),
                  pl.BlockSpec((ATT_WIDTH, d), lambda i: (1, 0)),
                  pl.BlockSpec((tm, d), lambda i: (i, 0)),
                  const((1, d)), const((1, d)), const((d, V7X_LANES)), const((d, V7X_LANES)),
                  const((N_EXPERTS, 1))],
        out_specs=[pl.BlockSpec((tm, d), lambda i: (i, 0)),
                   pl.BlockSpec((tm, d), lambda i: (i, 0)),
                   pl.BlockSpec((tm * (d // V7X_LANES), V7X_LANES), lambda i: (i, 0)),
                   pl.BlockSpec((2, tm), lambda i: (0, i)),
                   pl.BlockSpec((2, tm), lambda i: (0, i))],
        out_shape=[jax.ShapeDtypeStruct((t, d), F32), jax.ShapeDtypeStruct((t, d), BF16),
                   jax.ShapeDtypeStruct((t * (d // V7X_LANES), V7X_LANES), F32),
                   jax.ShapeDtypeStruct((2, t), I32), jax.ShapeDtypeStruct((2, t), F32)],
        compiler_params=pltpu.CompilerParams(
            dimension_semantics=("parallel",), vmem_limit_bytes=VMEM_LIMIT),
    )(y_ssd, y_att, w, w, h, ln_g.astype(F32).reshape(1, d), ln_b.astype(F32).reshape(1, d),
      rw_hi, rw_lo, router_b.astype(F32).reshape(N_EXPERTS, 1))


def _expert_kernel(be_ref, nu_ref, st_ref, h_hbm, *rest):
    ns = MOE_W_SPLIT
    wg_refs, wu_refs, wd_refs = rest[0:ns], rest[ns:2 * ns], rest[2 * ns:3 * ns]
    o_ref, xbuf, wg_s, wu_s, wd_s, sem = rest[3 * ns:]
    i = pl.program_id(0)
    nb = pl.num_programs(0)
    slot = lax.rem(i, 2)
    nxt_base = jnp.minimum(i + 1, nb - 1) * MOE_BLOCK

    parts = wg_s.shape[0] // V7X_LANES

    def row_copy(tok, r, s):
        return pltpu.make_async_copy(h_hbm.at[pl.ds(tok * parts, parts), :],
                                     xbuf.at[s, pl.ds(r * parts, parts), :], sem.at[s])

    def issue_rolled(base, s):
        def body(r, carry):
            row_copy(st_ref[base + r], r, s).start()
            return carry
        lax.fori_loop(0, MOE_BLOCK, body, 0)

    def wait_rows(s):
        pltpu.make_async_copy(h_hbm.at[pl.ds(0, MOE_BLOCK * parts), :], xbuf.at[s],
                              sem.at[s]).wait()

    @pl.when(i == 0)
    def _():
        issue_rolled(0, 0)

    wait_rows(slot)

    @pl.when(i < nu_ref[0])
    def _():
        prev = be_ref[jnp.maximum(i - 1, 0)]

        @pl.when((i == 0) | (be_ref[i] != prev))
        def _():
            for refs, dst in ((wg_refs, wg_s), (wu_refs, wu_s), (wd_refs, wd_s)):
                rows = dst.shape[0] // ns
                for q in range(ns):
                    dst[q * rows:(q + 1) * rows, :] = refs[q][...].astype(BF16)

        for r in range(MOE_BLOCK):
            row_copy(st_ref[nxt_base + r], r, 1 - slot).start()
        x = _load_token_rows(xbuf.at[slot], 0, MOE_BLOCK, parts).astype(BF16)
        g = jnp.dot(x, wg_s[...], preferred_element_type=F32)
        u = jnp.dot(x, wu_s[...], preferred_element_type=F32)
        a = (g * _sigmoid(g) * u).astype(BF16)
        _store_token_rows(o_ref, jnp.dot(a, wd_s[...], preferred_element_type=F32))

    @pl.when(i >= nu_ref[0])
    def _():
        issue_rolled(nxt_base, 1 - slot)
        o_ref[...] = jnp.zeros_like(o_ref)

    @pl.when(i == nb - 1)
    def _():
        wait_rows(1 - slot)


def _experts(block_expert, n_used, slot_tok, h1r, w_gate, w_up, w_down, layer):
    d = w_gate.shape[2]
    parts = d // V7X_LANES
    cap = slot_tok.shape[0]
    nb = cap // MOE_BLOCK
    ns = MOE_W_SPLIT

    def slabs(rows, cols):
        return [pl.BlockSpec((None, None, rows // ns, cols),
                             lambda i, be, nu, st, q=q: (layer, be[i], q, 0)) for q in range(ns)]

    return pl.pallas_call(
        _expert_kernel,
        name="moe_experts",
        grid_spec=pltpu.PrefetchScalarGridSpec(
            num_scalar_prefetch=3,
            grid=(nb,),
            in_specs=([pl.BlockSpec(memory_space=pl.ANY)] + slabs(d, D_FF) + slabs(d, D_FF)
                      + slabs(D_FF, d)),
            out_specs=pl.BlockSpec((MOE_BLOCK * parts, V7X_LANES), lambda i, be, nu, st: (i, 0)),
            scratch_shapes=[pltpu.VMEM((2, MOE_BLOCK * parts, V7X_LANES), F32),
                            pltpu.VMEM((d, D_FF), BF16), pltpu.VMEM((d, D_FF), BF16),
                            pltpu.VMEM((D_FF, d), BF16), pltpu.SemaphoreType.DMA((2,))]),
        out_shape=jax.ShapeDtypeStruct((cap * parts, V7X_LANES), F32),
        compiler_params=pltpu.CompilerParams(
            dimension_semantics=("arbitrary",), vmem_limit_bytes=VMEM_LIMIT),
    )(block_expert, n_used, slot_tok, h1r, *([w_gate] * ns + [w_up] * ns + [w_down] * ns))


def _moe_plan(eidx, t):
    flat_e = eidx.reshape(2 * t)
    onehot = (flat_e[:, None] == jnp.arange(N_EXPERTS, dtype=I32)[None, :]).astype(I32)
    cs = jnp.cumsum(onehot, axis=0)
    rank = jnp.sum(onehot * cs, axis=1) - 1
    counts = cs[-1]
    padded = (counts + MOE_BLOCK - 1) // MOE_BLOCK * MOE_BLOCK
    pad_end = jnp.cumsum(padded)
    pad_start = pad_end - padded
    dest = (pad_start[flat_e] + rank).astype(I32)
    cap = 2 * t + N_EXPERTS * MOE_BLOCK
    nb = cap // MOE_BLOCK
    block_start = jnp.arange(nb, dtype=I32) * MOE_BLOCK
    block_expert = jnp.minimum(
        jnp.sum((pad_end[None, :] <= block_start[:, None]).astype(I32), axis=1), N_EXPERTS - 1)
    n_used = (pad_end[-1] // MOE_BLOCK).astype(I32).reshape(1)
    tok = jnp.tile(jnp.arange(t, dtype=I32), 2)
    slot_tok = jnp.zeros((cap,), I32).at[dest].set(tok, unique_indices=True)
    return dest, block_expert, n_used, slot_tok


def _ple_kernel(dest_ref, h1_ref, h1b_ref, yb_hbm, gt_ref, p_ref, wg_ref, wp_ref, g_ref, b_ref,
                o_ref, ob_ref, gbuf, sem):
    i = pl.program_id(0)
    n = pl.num_programs(0)
    tm = h1_ref.shape[0]
    t = n * tm
    slot = lax.rem(i, 2)
    nxt_base = jnp.minimum(i + 1, n - 1) * tm

    parts = h1_ref.shape[1] // V7X_LANES

    def row_copy(src_row, k, r, s):
        return pltpu.make_async_copy(yb_hbm.at[pl.ds(src_row * parts, parts), :],
                                     gbuf.at[s, pl.ds((k * tm + r) * parts, parts), :], sem.at[s])

    def wait_tile(s):
        pltpu.make_async_copy(yb_hbm.at[pl.ds(0, 2 * tm * parts), :], gbuf.at[s], sem.at[s]).wait()

    @pl.when(i == 0)
    def _():
        def body(r, carry):
            for k in range(2):
                row_copy(dest_ref[k * t + r], k, r, 0).start()
            return carry
        lax.fori_loop(0, tm, body, 0)

    wait_tile(slot)
    for r in range(tm):
        for k in range(2):
            row_copy(dest_ref[k * t + nxt_base + r], k, r, 1 - slot).start()
    ffn = (gt_ref[:, 0:1] * _load_token_rows(gbuf.at[slot], 0, tm, parts)
           + gt_ref[:, 1:2] * _load_token_rows(gbuf.at[slot], tm, tm, parts))
    gate = _sigmoid(jnp.dot(h1b_ref[...], wg_ref[...], preferred_element_type=F32))
    pe = jnp.dot(p_ref[...].astype(BF16), wp_ref[...], preferred_element_type=F32)
    u = ALPHA * h1_ref[...] + ffn + gate * pe
    h2 = _layer_norm_rows(u, g_ref[...], b_ref[...])
    o_ref[...] = h2
    ob_ref[...] = h2.astype(BF16)

    @pl.when(i == n - 1)
    def _():
        wait_tile(1 - slot)


def _ple_ln(dest, h1, h1b, yb, gates_t, p, ple_gate_w, ple_w, ln_g, ln_b, tm):
    t, d = h1.shape
    pd = p.shape[1]
    row = lambda w: pl.BlockSpec((tm, w), lambda i, dest: (i, 0))
    const = lambda shape: pl.BlockSpec(shape, lambda i, dest: (0, 0), pipeline_mode=pl.Buffered(1))
    return pl.pallas_call(
        _ple_kernel,
        name="ple_ln",
        grid_spec=pltpu.PrefetchScalarGridSpec(
            num_scalar_prefetch=1,
            grid=(t // tm,),
            in_specs=[row(d), row(d), pl.BlockSpec(memory_space=pl.ANY), row(2), row(pd),
                      const((d, d)), const((pd, d)), const((1, d)), const((1, d))],
            out_specs=[row(d), row(d)],
            scratch_shapes=[pltpu.VMEM((2, 2 * tm * (d // V7X_LANES), V7X_LANES), F32),
                            pltpu.SemaphoreType.DMA((2,))]),
        out_shape=[jax.ShapeDtypeStruct((t, d), F32), jax.ShapeDtypeStruct((t, d), BF16)],
        compiler_params=pltpu.CompilerParams(
            dimension_semantics=("arbitrary",), vmem_limit_bytes=VMEM_LIMIT),
    )(dest, h1, h1b, yb, gates_t, p, ple_gate_w.astype(BF16), ple_w.astype(BF16),
      ln_g.astype(F32).reshape(1, d), ln_b.astype(F32).reshape(1, d))


def _layer(h, hb, p_i, w_in, layer, conv_w, conv_b, dt_bias, a_log, d_skip, ssm_norm_w,
           idx_k_norm_g, idx_k_norm_b, w_out, ln1_g, ln1_b, router_w, router_b,
           w_gate, w_up, w_down, ple_w, ple_gate_w, ln2_g, ln2_b, bsz, seq):
    t = bsz * seq
    tm = min(1024, t)
    row0 = layer * (OFF_IW + IDX_HEADS)
    za = _proj(hb, w_in, row0, OFF_Z, OFF_DT - OFF_Z, F32, False, tm)
    kk = _proj(hb, w_in, row0, OFF_Q + ATT_WIDTH, ATT_WIDTH, BF16, False, tm)
    qt = _proj(hb, w_in, row0, OFF_Q, ATT_WIDTH, BF16, True, tm)
    vt = _proj(hb, w_in, row0, OFF_Q + 2 * ATT_WIDTH, ATT_WIDTH, BF16, True, tm)
    iqt = _proj(hb, w_in, row0, OFF_IQ, IDX_HEADS * IDX_DIM, BF16, True, tm)
    sm, smt = _proj_small(hb, w_in, row0, tm)

    y_ssd = _ssd(za, sm, conv_w, conv_b, dt_bias, a_log, d_skip, ssm_norm_w, bsz, seq)
    y_att = _dsa(qt, iqt, vt, kk, sm, smt, idx_k_norm_g, idx_k_norm_b, bsz, seq)
    h1, h1b, h1r, eidx, gates = _outproj_ln_router(
        y_ssd, y_att, w_out, h, ln1_g, ln1_b, router_w, router_b, min(256, t))

    dest, block_expert, n_used, slot_tok = _moe_plan(eidx, t)
    yb = _experts(block_expert, n_used, slot_tok, h1r, w_gate, w_up, w_down, layer)
    return _ple_ln(dest, h1, h1b, yb, gates.T, p_i, ple_gate_w, ple_w, ln2_g, ln2_b, min(256, t))


def kernel(x, p, w_in, conv_w, conv_b, dt_bias, a_log, d_skip, ssm_norm_w, idx_k_norm_g,
           idx_k_norm_b, w_out, ln1_g, ln1_b, router_w, router_b, w_gate, w_up, w_down,
           ple_w, ple_gate_w, ln2_g, ln2_b):
    bsz, seq, d = x.shape
    t = bsz * seq
    h = x.reshape(t, d).astype(F32)
    hb = h.astype(BF16)
    w_in_t = jnp.swapaxes(w_in, 1, 2).reshape(-1, d)
    for i in range(w_in.shape[0]):
        h, hb = _layer(h, hb, p[i].reshape(t, -1), w_in_t, i, conv_w[i], conv_b[i], dt_bias[i],
                       a_log[i], d_skip[i], ssm_norm_w[i], idx_k_norm_g[i], idx_k_norm_b[i],
                       w_out[i], ln1_g[i], ln1_b[i], router_w, router_b, w_gate, w_up,
                       w_down, ple_w[i], ple_gate_w[i], ln2_g[i], ln2_b[i], bsz, seq)
    return h.reshape(bsz, seq, d).astype(x.dtype)
```

```python
import functools

import jax
import jax.numpy as jnp
from jax import lax
from jax.experimental import pallas as pl
from jax.experimental.pallas import tpu as pltpu

F32 = jnp.float32
BF16 = jnp.bfloat16
I32 = jnp.int32

SSD_WIDTH = 1024
SSD_HEADDIM = 64
SSD_HEADS = 16
SSD_STATE = 128
SSD_GROUPS = 2
CONV_WIDTH = 4
CHUNK = 128
CONV_DIM = SSD_WIDTH + 2 * SSD_GROUPS * SSD_STATE
ATT_HEADS = 8
ATT_HEADDIM = 128
ATT_WIDTH = ATT_HEADS * ATT_HEADDIM
IDX_HEADS = 16
IDX_DIM = 64
TOPK_MAX = 256
Q_BLOCK = 128
N_EXPERTS = 32
N_EXPERT_GROUPS = 8
EXPERTS_PER_GROUP = 4
D_FF = 512
ALPHA = (2.0 * 2) ** 0.25
LN_EPS = 1e-5

OFF_Z = 0
OFF_XBC = OFF_Z + SSD_WIDTH
OFF_DT = OFF_XBC + CONV_DIM
OFF_Q = OFF_DT + SSD_HEADS
OFF_IQ = OFF_Q + 3 * ATT_WIDTH
OFF_IK = OFF_IQ + IDX_HEADS * IDX_DIM
OFF_IW = OFF_IK + IDX_DIM
SM_DT = 0
SM_IK = SSD_HEADS
SM_IW = SM_IK + IDX_DIM
SM_W = 128

V7X_LANES = 128
VMEM_LIMIT = 56 * 1024 * 1024
FLOAT_LOWEST = -3.0e38
SEARCH_PROBES_PER_CHECK = 3
SEARCH_MAX_ITERS = 200
KEY_CHUNK = 256
DSA_QB = 256
DSA_KV_SPLIT = 4
MOE_BLOCK = 256
MOE_W_SPLIT = 4
GATHER_DMA_PRIORITY = 1


def _sigmoid(x):
    return 1.0 / (1.0 + jnp.exp(-x))


def _softplus(x):
    return jnp.maximum(x, 0.0) + jnp.log(1.0 + jnp.exp(-jnp.abs(x)))


def _nt_dot(a, b, **kw):
    return lax.dot_general(a, b, (((1,), (1,)), ((), ())), preferred_element_type=F32, **kw)


def _layer_norm_rows(u, g, b):
    mu = jnp.mean(u, axis=-1, keepdims=True)
    d = u - mu
    var = jnp.mean(d * d, axis=-1, keepdims=True)
    return d * lax.rsqrt(var + LN_EPS) * g + b


PROJ_ROWS = 256


def _proj_kernel(x_ref, wt_ref, o_ref, w_s, *, transposed):
    tn, k = wt_ref.shape

    @pl.when(pl.program_id(1) == 0)
    def _():
        if transposed:
            w_s[...] = wt_ref[...].astype(BF16)
        else:
            def stage(r, carry):
                r0 = pl.multiple_of(r * PROJ_ROWS, PROJ_ROWS)
                w_s[pl.ds(r0, PROJ_ROWS), :] = wt_ref[:, pl.ds(r0, PROJ_ROWS)].T.astype(BF16)
                return carry

            lax.fori_loop(0, k // PROJ_ROWS, stage, 0)

    if transposed:
        o_ref[...] = _nt_dot(w_s[...], x_ref[...]).astype(o_ref.dtype)
    else:
        o_ref[...] = jnp.dot(x_ref[...], w_s[...], preferred_element_type=F32).astype(o_ref.dtype)


def _proj(x, w_in_t, row0, c0, width, out_dtype, transposed, tm, tn=512):
    m, k = x.shape
    assert (row0 + c0) % 8 == 0 and width % tn == 0 and m % tm == 0
    in_specs = [pl.BlockSpec((tm, k), lambda j, i: (i, 0)),
                pl.BlockSpec((pl.Element(tn), pl.Element(k)),
                             lambda j, i: (pl.multiple_of(row0 + c0 + j * tn, 8), 0))]
    if transposed:
        out_spec = pl.BlockSpec((tn, tm), lambda j, i: (j, i))
        out_shape = jax.ShapeDtypeStruct((width, m), out_dtype)
        scratch = pltpu.VMEM((tn, k), BF16)
    else:
        out_spec = pl.BlockSpec((tm, tn), lambda j, i: (i, j))
        out_shape = jax.ShapeDtypeStruct((m, width), out_dtype)
        scratch = pltpu.VMEM((k, tn), BF16)
    return pl.pallas_call(
        functools.partial(_proj_kernel, transposed=transposed),
        name="in_proj_t" if transposed else "in_proj",
        grid=(width // tn, m // tm),
        in_specs=in_specs,
        out_specs=out_spec,
        out_shape=out_shape,
        scratch_shapes=[scratch],
        compiler_params=pltpu.CompilerParams(
            dimension_semantics=("parallel", "arbitrary"), vmem_limit_bytes=VMEM_LIMIT),
    )(x, w_in_t)


def _proj_small_kernel(x_ref, wa_ref, wb_ref, o_ref, ot_ref, w_s, wt_s):
    @pl.when(pl.program_id(0) == 0)
    def _():
        k = wa_ref.shape[1]
        tail = SM_W - IDX_DIM - IDX_HEADS
        wt = jnp.concatenate(
            [wa_ref[0:SSD_HEADS, :], wb_ref[tail:SM_W, :],
             jnp.zeros((SM_W - SSD_HEADS - IDX_DIM - IDX_HEADS, k), F32)], axis=0)
        wt_s[...] = wt.astype(BF16)
        w_s[...] = wt.T.astype(BF16)

    o_ref[...] = jnp.dot(x_ref[...], w_s[...], preferred_element_type=F32)
    ot_ref[...] = _nt_dot(wt_s[...], x_ref[...])


def _proj_small(x, w_in_t, row0, tm):
    m, k = x.shape
    d_in = OFF_IW + IDX_HEADS
    assert OFF_IW - OFF_IK == IDX_DIM and (row0 + OFF_DT) % 8 == 0 and (row0 + d_in) % 8 == 0
    return pl.pallas_call(
        _proj_small_kernel,
        name="in_proj_small",
        grid=(m // tm,),
        in_specs=[pl.BlockSpec((tm, k), lambda i: (i, 0)),
                  pl.BlockSpec((pl.Element(SM_W), pl.Element(k)), lambda i: (row0 + OFF_DT, 0)),
                  pl.BlockSpec((pl.Element(SM_W), pl.Element(k)),
                               lambda i: (row0 + d_in - SM_W, 0))],
        out_specs=[pl.BlockSpec((tm, SM_W), lambda i: (i, 0)),
                   pl.BlockSpec((SM_W, tm), lambda i: (0, i))],
        out_shape=[jax.ShapeDtypeStruct((m, SM_W), F32), jax.ShapeDtypeStruct((SM_W, m), F32)],
        scratch_shapes=[pltpu.VMEM((k, SM_W), BF16), pltpu.VMEM((SM_W, k), BF16)],
        compiler_params=pltpu.CompilerParams(
            dimension_semantics=("arbitrary",), vmem_limit_bytes=VMEM_LIMIT),
    )(x, w_in_t, w_in_t)


def _ssd_kernel(za_ref, sm_ref, cw_ref, cb_ref, dtb_ref, a_ref, dexp_ref, nw_ref, e_ref,
                o_ref, cbuf, state, ybuf):
    c = pl.program_id(1)
    L = CHUNK

    @pl.when(c == 0)
    def _():
        cbuf[0:8, :] = jnp.zeros((8, CONV_DIM), F32)
        state[...] = jnp.zeros_like(state)

    cbuf[8:8 + L, :] = za_ref[:, SSD_WIDTH:SSD_WIDTH + CONV_DIM]
    acc = cb_ref[...] + cw_ref[0:1, :] * cbuf[5:5 + L, :]
    acc = acc + cw_ref[1:2, :] * cbuf[6:6 + L, :]
    acc = acc + cw_ref[2:3, :] * cbuf[7:7 + L, :]
    acc = acc + cw_ref[3:4, :] * cbuf[8:8 + L, :]
    cbuf[0:8, :] = cbuf[L:L + 8, :]
    xbc = acc * _sigmoid(acc)
    xs = xbc[:, 0:SSD_WIDTH]
    bm = xbc[:, SSD_WIDTH:SSD_WIDTH + SSD_GROUPS * SSD_STATE]
    cm = xbc[:, SSD_WIDTH + SSD_GROUPS * SSD_STATE:CONV_DIM]

    dt = _softplus(sm_ref[...] + dtb_ref[...])
    da = dt * a_ref[...]
    row = lax.broadcasted_iota(I32, (L, L), 0)
    col = lax.broadcasted_iota(I32, (L, L), 1)
    causal = col <= row
    tri = jnp.where(causal, 1.0, 0.0).astype(F32)
    a_cs = jnp.dot(tri, da, preferred_element_type=F32, precision=lax.Precision.HIGHEST)
    a_cs_t = a_cs.T
    expa = jnp.exp(a_cs)
    dte = jnp.exp(a_cs[L - 1:L, :] - a_cs)
    stacked = jnp.concatenate([dt, expa, dte], axis=0)
    expd = jnp.dot(stacked, e_ref[...], preferred_element_type=F32,
                   precision=lax.Precision.HIGHEST)
    dt_x = expd[0:L, :]
    expa_x = expd[L:2 * L, :]
    dte_x = expd[2 * L:3 * L, :]
    xdt = xs * dt_x
    hpg = SSD_HEADS // SSD_GROUPS
    gw = hpg * SSD_HEADDIM
    for g in range(SSD_GROUPS):
        cm_g = cm[:, g * SSD_STATE:(g + 1) * SSD_STATE].astype(BF16)
        bm_gf = bm[:, g * SSD_STATE:(g + 1) * SSD_STATE]
        bm_g = bm_gf.astype(BF16)
        cb = _nt_dot(cm_g, bm_g)
        for hh in range(hpg):
            h = g * hpg + hh
            seg = a_cs[:, h:h + 1] - a_cs_t[h:h + 1, :]
            dec = jnp.exp(jnp.where(causal, seg, -jnp.inf))
            mh = (cb * dec).astype(BF16)
            xh = xdt[:, h * SSD_HEADDIM:(h + 1) * SSD_HEADDIM].astype(BF16)
            ybuf[:, h * SSD_HEADDIM:(h + 1) * SSD_HEADDIM] = jnp.dot(
                mh, xh, preferred_element_type=F32)
        st_g = state[:, g * gw:(g + 1) * gw]
        y_off = jnp.dot(cm_g, st_g.astype(BF16), preferred_element_type=F32)
        ybuf[:, g * gw:(g + 1) * gw] = (ybuf[:, g * gw:(g + 1) * gw]
                                        + y_off * expa_x[:, g * gw:(g + 1) * gw])
        xw = (xdt[:, g * gw:(g + 1) * gw] * dte_x[:, g * gw:(g + 1) * gw]).astype(BF16)
        upd = jnp.dot(bm_gf.T.astype(BF16), xw, preferred_element_type=F32)
        state[:, g * gw:(g + 1) * gw] = st_g * expa_x[L - 1:L, g * gw:(g + 1) * gw] + upd

    y = ybuf[...] + xs * dexp_ref[...]
    z = za_ref[:, 0:SSD_WIDTH]
    yg = y * (z * _sigmoid(z))
    ms = jnp.mean(yg * yg, axis=-1, keepdims=True)
    o_ref[...] = (yg * lax.rsqrt(ms + LN_EPS) * nw_ref[...]).astype(o_ref.dtype)


def _ssd(za, sm, conv_w, conv_b, dt_bias, a_log, d_skip, ssm_norm_w, bsz, seq):
    nc = seq // CHUNK
    t = bsz * seq
    pad = SM_W - SSD_HEADS
    dtb = jnp.pad(dt_bias.astype(F32), (0, pad)).reshape(1, SM_W)
    a = jnp.pad(-jnp.exp(a_log.astype(F32)), (0, pad)).reshape(1, SM_W)
    dexp = jnp.repeat(d_skip.astype(F32), SSD_HEADDIM).reshape(1, SSD_WIDTH)
    e = (jnp.arange(SSD_WIDTH)[None, :] // SSD_HEADDIM == jnp.arange(SM_W)[:, None]).astype(F32)
    const = lambda shape: pl.BlockSpec(shape, lambda b, c: (0, 0))
    return pl.pallas_call(
        _ssd_kernel,
        name="ssd",
        grid=(bsz, nc),
        in_specs=[pl.BlockSpec((CHUNK, SSD_WIDTH + CONV_DIM), lambda b, c: (b * nc + c, 0)),
                  pl.BlockSpec((CHUNK, SM_W), lambda b, c: (b * nc + c, 0)),
                  const((CONV_WIDTH, CONV_DIM)), const((1, CONV_DIM)), const((1, SM_W)),
                  const((1, SM_W)), const((1, SSD_WIDTH)), const((1, SSD_WIDTH)),
                  const((SM_W, SSD_WIDTH))],
        out_specs=pl.BlockSpec((CHUNK, SSD_WIDTH), lambda b, c: (b * nc + c, 0)),
        out_shape=jax.ShapeDtypeStruct((t, SSD_WIDTH), BF16),
        scratch_shapes=[pltpu.VMEM((CHUNK + 8, CONV_DIM), F32),
                        pltpu.VMEM((SSD_STATE, SSD_WIDTH), F32),
                        pltpu.VMEM((CHUNK, SSD_WIDTH), F32)],
        compiler_params=pltpu.CompilerParams(
            dimension_semantics=("parallel", "arbitrary"), vmem_limit_bytes=VMEM_LIMIT),
    )(za, sm, conv_w.astype(F32), conv_b.astype(F32).reshape(1, CONV_DIM), dtb, a, dexp,
      ssm_norm_w.astype(F32).reshape(1, SSD_WIDTH), e)


def _dsa_kernel(qt_ref, iqt_ref, *rest, topk):
    ns = DSA_KV_SPLIT
    vt_refs, k_refs = rest[0:ns], rest[ns:2 * ns]
    smb_ref, smt_ref, g_ref, b_ref, o_ref, ikn, keys, acc_s, s_s, p_s, bias_s = rest[2 * ns:]
    slab = ATT_WIDTH // ns
    qi = pl.program_id(1)
    QB = o_ref.shape[0]
    KC = KEY_CHUNK

    @pl.when(qi == 0)
    def _():
        ik = smb_ref[:, SM_IK:SM_IK + IDX_DIM]
        ikn[...] = _layer_norm_rows(ik, g_ref[...], b_ref[...]).astype(BF16)

    nkc = (qi * QB + QB - 1) // KC + 1
    w = smt_ref[SM_IW:SM_IW + IDX_HEADS, :] * (IDX_DIM ** -0.5 * IDX_HEADS ** -0.5)
    qpos = qi * QB + lax.broadcasted_iota(I32, (KC, QB), 1)
    kiota = lax.broadcasted_iota(I32, (KC, QB), 0)

    def score_chunk(c, carry):
        rmax, rmin = carry
        off = pl.multiple_of(c * KC, KC)
        kc = ikn[pl.ds(off, KC), :]
        acc = jnp.zeros((KC, QB), F32)
        for h in range(IDX_HEADS):
            d = jnp.dot(kc, iqt_ref[h * IDX_DIM:(h + 1) * IDX_DIM, :], preferred_element_type=F32)
            acc = acc + w[h:h + 1, :] * jnp.maximum(d, 0.0)
        vis = off + kiota <= qpos
        sv = jnp.where(vis, acc, -jnp.inf)
        keys[c] = sv
        rmax = jnp.maximum(rmax, jnp.max(sv, axis=0, keepdims=True))
        rmin = jnp.minimum(rmin, jnp.min(jnp.where(vis, acc, jnp.inf), axis=0, keepdims=True))
        return rmax, rmin

    rmax, rmin = lax.fori_loop(
        0, nkc, score_chunk,
        (jnp.full((1, QB), -jnp.inf, F32), jnp.full((1, QB), jnp.inf, F32)))

    kf = float(topk)

    def count_ge(mid):
        def count_chunk(c, acc):
            m = jnp.where(keys[c] >= mid, 1.0, 0.0)
            return acc + jnp.sum(m.reshape(KC // 8, 8, QB), axis=0)

        acc = lax.fori_loop(0, nkc, count_chunk, jnp.zeros((8, QB), F32))
        return jnp.sum(acc, axis=0, keepdims=True)

    nvis = (qi * QB + lax.broadcasted_iota(I32, (1, QB), 1) + 1).astype(F32)
    small = nvis <= kf

    def search_cond(st):
        return (st[0] > 0.0) & (st[1] < SEARCH_MAX_ITERS)

    def search_body(st):
        _, it, lo, hi, clo = st
        for _ in range(SEARCH_PROBES_PER_CHECK):
            mid = 0.5 * lo + 0.5 * hi
            splittable = (mid > lo) & (mid < hi)
            c = count_ge(mid)
            up = splittable & (c >= kf)
            down = splittable & (c < kf)
            lo = jnp.where(up, mid, lo)
            clo = jnp.where(up, c, clo)
            hi = jnp.where(down, mid, hi)
        active = jnp.logical_not(small) & splittable & (clo != kf)
        return jnp.sum(jnp.where(active, 1.0, 0.0)), it + 1, lo, hi, clo

    hi0 = rmax + jnp.abs(rmax) * 1e-6 + 1e-37
    st = lax.while_loop(search_cond, search_body,
                        (jnp.float32(1.0), jnp.int32(0), rmin, hi0, nvis))
    thr = jnp.where(small, FLOAT_LOWEST, st[2])

    scale = ATT_HEADDIM ** -0.5
    acc_s[...] = jnp.zeros_like(acc_s)

    def att_chunk(c, carry):
        ms, ls = carry
        off = pl.multiple_of(c * KC, KC)
        bias_s[...] = jnp.where(keys[c] >= thr, 0.0, -1e30)
        for h in range(ATT_HEADS):
            lo, hi = h * ATT_HEADDIM, (h + 1) * ATT_HEADDIM
            s_s[h] = jnp.dot(k_refs[lo // slab][pl.ds(off, KC), lo % slab:lo % slab + ATT_HEADDIM],
                             qt_ref[lo:hi, :],
                             preferred_element_type=F32)
        new_ms, new_ls, alphas = [], [], []
        for h in range(ATT_HEADS):
            s = s_s[h] * scale + bias_s[...]
            m_new = jnp.maximum(ms[h], jnp.max(s, axis=0, keepdims=True))
            alpha = jnp.exp(ms[h] - m_new)
            p = jnp.exp(s - m_new)
            new_ls.append(alpha * ls[h] + jnp.sum(p, axis=0, keepdims=True))
            new_ms.append(m_new)
            alphas.append(alpha)
            p_s[h] = p.astype(BF16)
        for h in range(ATT_HEADS):
            lo, hi = h * ATT_HEADDIM, (h + 1) * ATT_HEADDIM
            pv = jnp.dot(vt_refs[lo // slab][lo % slab:lo % slab + ATT_HEADDIM, pl.ds(off, KC)],
                         p_s[h],
                         preferred_element_type=F32)
            acc_s[h] = alphas[h] * acc_s[h] + pv
        return tuple(new_ms), tuple(new_ls)

    m0 = tuple(jnp.full((1, QB), -1e30, F32) for _ in range(ATT_HEADS))
    l0 = tuple(jnp.zeros((1, QB), F32) for _ in range(ATT_HEADS))
    _, ls = lax.fori_loop(0, nkc, att_chunk, (m0, l0))
    for h in range(ATT_HEADS):
        o_ref[:, h * ATT_HEADDIM:(h + 1) * ATT_HEADDIM] = (acc_s[h] / ls[h]).T.astype(o_ref.dtype)


def _dsa(qt, iqt, vt, kk, sm, smt, g, b, bsz, seq):
    qb = min(DSA_QB, seq)
    nq = seq // qb
    t = bsz * seq
    topk = min(TOPK_MAX, seq // 4)
    single = pl.Buffered(1)
    ns = DSA_KV_SPLIT
    slab = ATT_WIDTH // ns
    vt_specs = [pl.BlockSpec((slab, seq), lambda bi, qi, q=q: (q, bi), pipeline_mode=single)
                for q in range(ns)]
    k_specs = [pl.BlockSpec((seq, slab), lambda bi, qi, q=q: (bi, q), pipeline_mode=single)
               for q in range(ns)]
    return pl.pallas_call(
        functools.partial(_dsa_kernel, topk=topk),
        name="dsa",
        grid=(bsz, nq),
        in_specs=[pl.BlockSpec((ATT_WIDTH, qb), lambda bi, qi: (0, bi * nq + qi)),
                  pl.BlockSpec((IDX_HEADS * IDX_DIM, qb), lambda bi, qi: (0, bi * nq + qi)),
                  *vt_specs, *k_specs,
                  pl.BlockSpec((seq, SM_W), lambda bi, qi: (bi, 0), pipeline_mode=single),
                  pl.BlockSpec((SM_W, qb), lambda bi, qi: (0, bi * nq + qi)),
                  pl.BlockSpec((1, IDX_DIM), lambda bi, qi: (0, 0)),
                  pl.BlockSpec((1, IDX_DIM), lambda bi, qi: (0, 0))],
        out_specs=pl.BlockSpec((qb, ATT_WIDTH), lambda bi, qi: (bi * nq + qi, 0)),
        out_shape=jax.ShapeDtypeStruct((t, ATT_WIDTH), BF16),
        scratch_shapes=[pltpu.VMEM((seq, IDX_DIM), BF16),
                        pltpu.VMEM((seq // KEY_CHUNK, KEY_CHUNK, qb), F32),
                        pltpu.VMEM((ATT_HEADS, ATT_HEADDIM, qb), F32),
                        pltpu.VMEM((ATT_HEADS, KEY_CHUNK, qb), F32),
                        pltpu.VMEM((ATT_HEADS, KEY_CHUNK, qb), BF16),
                        pltpu.VMEM((KEY_CHUNK, qb), F32)],
        compiler_params=pltpu.CompilerParams(
            dimension_semantics=("parallel", "arbitrary"), vmem_limit_bytes=VMEM_LIMIT),
    )(qt, iqt, *([vt] * ns), *([kk] * ns), sm, smt, g.astype(F32).reshape(1, IDX_DIM),
      b.astype(F32).reshape(1, IDX_DIM))


def _route_rows(logits_t, rb):
    aff = _sigmoid(logits_t)
    biased = aff + rb
    rb_rows = [biased[r:r + 1, :] for r in range(N_EXPERTS)]
    ra_rows = [aff[r:r + 1, :] for r in range(N_EXPERTS)]
    epg = EXPERTS_PER_GROUP
    gs = []
    for g in range(N_EXPERT_GROUPS):
        a, b, c, d = rb_rows[epg * g:epg * g + epg]
        p, q = jnp.maximum(a, b), jnp.minimum(a, b)
        r, s = jnp.maximum(c, d), jnp.minimum(c, d)
        gs.append(jnp.maximum(p, r) + jnp.maximum(jnp.minimum(p, r), jnp.maximum(q, s)))
    best = gs[0]
    bidx = jnp.zeros_like(best, dtype=I32)
    for g in range(1, N_EXPERT_GROUPS):
        better = gs[g] > best
        best = jnp.where(better, gs[g], best)
        bidx = jnp.where(better, g, bidx)
    vb = [rb_rows[j] for j in range(epg)]
    va = [ra_rows[j] for j in range(epg)]
    for g in range(1, N_EXPERT_GROUPS):
        pick = bidx == g
        vb = [jnp.where(pick, rb_rows[epg * g + j], vb[j]) for j in range(epg)]
        va = [jnp.where(pick, ra_rows[epg * g + j], va[j]) for j in range(epg)]
    t1, a1, i1 = vb[0], va[0], jnp.zeros_like(bidx)
    for j in range(1, epg):
        better = vb[j] > t1
        t1 = jnp.where(better, vb[j], t1)
        a1 = jnp.where(better, va[j], a1)
        i1 = jnp.where(better, j, i1)
    t2 = jnp.full_like(t1, -jnp.inf)
    a2 = jnp.zeros_like(a1)
    i2 = jnp.zeros_like(i1)
    for j in range(epg):
        vj = jnp.where(i1 == j, -jnp.inf, vb[j])
        better = vj > t2
        t2 = jnp.where(better, vj, t2)
        a2 = jnp.where(better, va[j], a2)
        i2 = jnp.where(better, j, i2)
    den = a1 + a2
    return bidx * epg + i1, bidx * epg + i2, a1 / den, a2 / den


def _outproj_kernel(ys_ref, ya_ref, w1_ref, w2_ref, h_ref, g_ref, b_ref, rwh_ref, rwl_ref, rb_ref,
                    h1_ref, h1b_ref, eidx_ref, gate_ref):
    mix = jnp.dot(ys_ref[...], w1_ref[...], preferred_element_type=F32)
    mix = mix + jnp.dot(ya_ref[...], w2_ref[...], preferred_element_type=F32)
    h1 = _layer_norm_rows(ALPHA * h_ref[...] + mix, g_ref[...], b_ref[...])
    h1_ref[...] = h1
    h1_hi = h1.astype(BF16)
    h1b_ref[...] = h1_hi
    h1_lo = (h1 - h1_hi.astype(F32)).astype(BF16)
    logits = jnp.dot(h1_hi, rwh_ref[...], preferred_element_type=F32)
    logits = logits + jnp.dot(h1_hi, rwl_ref[...], preferred_element_type=F32)
    logits = logits + jnp.dot(h1_lo, rwh_ref[...], preferred_element_type=F32)
    logits_t = logits.T[0:N_EXPERTS, :]
    e1, e2, g1, g2 = _route_rows(logits_t, rb_ref[...])
    eidx_ref[0:1, :] = e1
    eidx_ref[1:2, :] = e2
    gate_ref[0:1, :] = g1
    gate_ref[1:2, :] = g2


def _outproj_ln_router(y_ssd, y_att, w_out, h, ln_g, ln_b, router_w, router_b, tm):
    t, d = h.shape
    w = w_out.astype(BF16)
    rw = jnp.pad(router_w.astype(F32), ((0, 0), (0, V7X_LANES - N_EXPERTS)))
    rw_hi = rw.astype(BF16)
    rw_lo = (rw - rw_hi.astype(F32)).astype(BF16)
    const = lambda shape: pl.BlockSpec(shape, lambda i: (0, 0))
    return pl.pallas_call(
        _outproj_kernel,
        name="outproj_ln_router",
        grid=(t // tm,),
        in_specs=[pl.BlockSpec((tm, SSD_WIDTH), lambda i: (i, 0)),
                  pl.BlockSpec((tm, ATT_WIDTH), lambda i: (i, 0)),
                  pl.BlockSpec((SSD_WIDTH, d), lambda i: (0, 0)),
                  pl.BlockSpec((ATT_WIDTH, d), lambda i: (1, 0)),
                  pl.BlockSpec((tm, d), lambda i: (i, 0)),
                  const((1, d)), const((1, d)), const((d, V7X_LANES)), const((d, V7X_LANES)),
                  const((N_EXPERTS, 1))],
        out_specs=[pl.BlockSpec((tm, d), lambda i: (i, 0)),
                   pl.BlockSpec((tm, d), lambda i: (i, 0)),
                   pl.BlockSpec((2, tm), lambda i: (0, i)),
                   pl.BlockSpec((2, tm), lambda i: (0, i))],
        out_shape=[jax.ShapeDtypeStruct((t, d), F32), jax.ShapeDtypeStruct((t, d), BF16),
                   jax.ShapeDtypeStruct((2, t), I32), jax.ShapeDtypeStruct((2, t), F32)],
        compiler_params=pltpu.CompilerParams(
            dimension_semantics=("parallel",), vmem_limit_bytes=VMEM_LIMIT),
    )(y_ssd, y_att, w, w, h, ln_g.astype(F32).reshape(1, d), ln_b.astype(F32).reshape(1, d),
      rw_hi, rw_lo, router_b.astype(F32).reshape(N_EXPERTS, 1))


def _expert_kernel(be_ref, nu_ref, st_ref, h_hbm, *rest):
    ns = MOE_W_SPLIT
    wg_refs, wu_refs, wd_refs = rest[0:ns], rest[ns:2 * ns], rest[2 * ns:3 * ns]
    o_ref, xbuf, wg_s, wu_s, wd_s, sem = rest[3 * ns:]
    i = pl.program_id(0)
    nb = pl.num_programs(0)
    slot = lax.rem(i, 2)
    nxt_base = jnp.minimum(i + 1, nb - 1) * MOE_BLOCK

    def row_copy(tok, r, s):
        return pltpu.make_async_copy(h_hbm.at[pl.ds(tok, 1), :], xbuf.at[s, pl.ds(r, 1), :],
                                     sem.at[s])

    def issue_rolled(base, s):
        def body(r, carry):
            row_copy(st_ref[base + r], r, s).start(priority=GATHER_DMA_PRIORITY)
            return carry
        lax.fori_loop(0, MOE_BLOCK, body, 0)

    def wait_rows(s):
        pltpu.make_async_copy(h_hbm.at[pl.ds(0, MOE_BLOCK), :], xbuf.at[s], sem.at[s]).wait()

    @pl.when(i == 0)
    def _():
        issue_rolled(0, 0)

    wait_rows(slot)

    @pl.when(i < nu_ref[0])
    def _():
        prev = be_ref[jnp.maximum(i - 1, 0)]

        @pl.when((i == 0) | (be_ref[i] != prev))
        def _():
            for refs, dst in ((wg_refs, wg_s), (wu_refs, wu_s), (wd_refs, wd_s)):
                rows = dst.shape[0] // ns
                for q in range(ns):
                    dst[q * rows:(q + 1) * rows, :] = refs[q][...].astype(BF16)

        for r in range(MOE_BLOCK):
            row_copy(st_ref[nxt_base + r], r, 1 - slot).start(priority=GATHER_DMA_PRIORITY)
        x = xbuf[slot].astype(BF16)
        g = jnp.dot(x, wg_s[...], preferred_element_type=F32)
        u = jnp.dot(x, wu_s[...], preferred_element_type=F32)
        a = (g * _sigmoid(g) * u).astype(BF16)
        o_ref[...] = jnp.dot(a, wd_s[...], preferred_element_type=F32)

    @pl.when(i >= nu_ref[0])
    def _():
        issue_rolled(nxt_base, 1 - slot)
        o_ref[...] = jnp.zeros_like(o_ref)

    @pl.when(i == nb - 1)
    def _():
        wait_rows(1 - slot)


def _experts(block_expert, n_used, slot_tok, h1, w_gate, w_up, w_down, layer):
    t, d = h1.shape
    cap = slot_tok.shape[0]
    nb = cap // MOE_BLOCK
    ns = MOE_W_SPLIT

    def slabs(rows, cols):
        return [pl.BlockSpec((None, None, rows // ns, cols),
                             lambda i, be, nu, st, q=q: (layer, be[i], q, 0)) for q in range(ns)]

    return pl.pallas_call(
        _expert_kernel,
        name="moe_experts",
        grid_spec=pltpu.PrefetchScalarGridSpec(
            num_scalar_prefetch=3,
            grid=(nb,),
            in_specs=([pl.BlockSpec(memory_space=pl.ANY)] + slabs(d, D_FF) + slabs(d, D_FF)
                      + slabs(D_FF, d)),
            out_specs=pl.BlockSpec((MOE_BLOCK, d), lambda i, be, nu, st: (i, 0)),
            scratch_shapes=[pltpu.VMEM((2, MOE_BLOCK, d), F32),
                            pltpu.VMEM((d, D_FF), BF16), pltpu.VMEM((d, D_FF), BF16),
                            pltpu.VMEM((D_FF, d), BF16), pltpu.SemaphoreType.DMA((2,))]),
        out_shape=jax.ShapeDtypeStruct((cap, d), F32),
        compiler_params=pltpu.CompilerParams(
            dimension_semantics=("arbitrary",), vmem_limit_bytes=VMEM_LIMIT),
    )(block_expert, n_used, slot_tok, h1, *([w_gate] * ns + [w_up] * ns + [w_down] * ns))


def _moe_plan(eidx, t):
    flat_e = eidx.reshape(2 * t)
    onehot = (flat_e[:, None] == jnp.arange(N_EXPERTS, dtype=I32)[None, :]).astype(I32)
    cs = jnp.cumsum(onehot, axis=0)
    rank = jnp.sum(onehot * cs, axis=1) - 1
    counts = cs[-1]
    padded = (counts + MOE_BLOCK - 1) // MOE_BLOCK * MOE_BLOCK
    pad_end = jnp.cumsum(padded)
    pad_start = pad_end - padded
    dest = (pad_start[flat_e] + rank).astype(I32)
    cap = 2 * t + N_EXPERTS * MOE_BLOCK
    nb = cap // MOE_BLOCK
    block_start = jnp.arange(nb, dtype=I32) * MOE_BLOCK
    block_expert = jnp.minimum(
        jnp.sum((pad_end[None, :] <= block_start[:, None]).astype(I32), axis=1), N_EXPERTS - 1)
    n_used = (pad_end[-1] // MOE_BLOCK).astype(I32).reshape(1)
    tok = jnp.tile(jnp.arange(t, dtype=I32), 2)
    slot_tok = jnp.zeros((cap,), I32).at[dest].set(tok, unique_indices=True)
    return dest, block_expert, n_used, slot_tok


def _ple_kernel(dest_ref, h1_ref, h1b_ref, yb_hbm, gt_ref, p_ref, wg_ref, wp_ref, g_ref, b_ref,
                o_ref, ob_ref, gbuf, sem):
    i = pl.program_id(0)
    n = pl.num_programs(0)
    tm = h1_ref.shape[0]
    t = n * tm
    slot = lax.rem(i, 2)
    nxt_base = jnp.minimum(i + 1, n - 1) * tm

    def row_copy(src_row, k, r, s):
        return pltpu.make_async_copy(yb_hbm.at[pl.ds(src_row, 1), :],
                                     gbuf.at[s, pl.ds(k * tm + r, 1), :], sem.at[s])

    def wait_tile(s):
        pltpu.make_async_copy(yb_hbm.at[pl.ds(0, 2 * tm), :], gbuf.at[s], sem.at[s]).wait()

    @pl.when(i == 0)
    def _():
        def body(r, carry):
            for k in range(2):
                row_copy(dest_ref[k * t + r], k, r, 0).start(priority=k)
            return carry
        lax.fori_loop(0, tm, body, 0)

    wait_tile(slot)
    for r in range(tm):
        for k in range(2):
            row_copy(dest_ref[k * t + nxt_base + r], k, r, 1 - slot).start(priority=k)
    ffn = gt_ref[:, 0:1] * gbuf[slot, 0:tm, :] + gt_ref[:, 1:2] * gbuf[slot, tm:2 * tm, :]
    gate = _sigmoid(jnp.dot(h1b_ref[...], wg_ref[...], preferred_element_type=F32))
    pe = jnp.dot(p_ref[...].astype(BF16), wp_ref[...], preferred_element_type=F32)
    u = ALPHA * h1_ref[...] + ffn + gate * pe
    h2 = _layer_norm_rows(u, g_ref[...], b_ref[...])
    o_ref[...] = h2
    ob_ref[...] = h2.astype(BF16)

    @pl.when(i == n - 1)
    def _():
        wait_tile(1 - slot)


def _ple_ln(dest, h1, h1b, yb, gates_t, p, ple_gate_w, ple_w, ln_g, ln_b, tm):
    t, d = h1.shape
    pd = p.shape[1]
    row = lambda w: pl.BlockSpec((tm, w), lambda i, dest: (i, 0))
    const = lambda shape: pl.BlockSpec(shape, lambda i, dest: (0, 0), pipeline_mode=pl.Buffered(1))
    return pl.pallas_call(
        _ple_kernel,
        name="ple_ln",
        grid_spec=pltpu.PrefetchScalarGridSpec(
            num_scalar_prefetch=1,
            grid=(t // tm,),
            in_specs=[row(d), row(d), pl.BlockSpec(memory_space=pl.ANY), row(2), row(pd),
                      const((d, d)), const((pd, d)), const((1, d)), const((1, d))],
            out_specs=[row(d), row(d)],
            scratch_shapes=[pltpu.VMEM((2, 2 * tm, d), F32), pltpu.SemaphoreType.DMA((2,))]),
        out_shape=[jax.ShapeDtypeStruct((t, d), F32), jax.ShapeDtypeStruct((t, d), BF16)],
        compiler_params=pltpu.CompilerParams(
            dimension_semantics=("arbitrary",), vmem_limit_bytes=VMEM_LIMIT),
    )(dest, h1, h1b, yb, gates_t, p, ple_gate_w.astype(BF16), ple_w.astype(BF16),
      ln_g.astype(F32).reshape(1, d), ln_b.astype(F32).reshape(1, d))


def _layer(h, hb, p_i, w_in, layer, conv_w, conv_b, dt_bias, a_log, d_skip, ssm_norm_w,
           idx_k_norm_g, idx_k_norm_b, w_out, ln1_g, ln1_b, router_w, router_b,
           w_gate, w_up, w_down, ple_w, ple_gate_w, ln2_g, ln2_b, bsz, seq):
    t = bsz * seq
    tm = min(1024, t)
    row0 = layer * (OFF_IW + IDX_HEADS)
    za = _proj(hb, w_in, row0, OFF_Z, OFF_DT - OFF_Z, F32, False, tm)
    kk = _proj(hb, w_in, row0, OFF_Q + ATT_WIDTH, ATT_WIDTH, BF16, False, tm)
    qt = _proj(hb, w_in, row0, OFF_Q, ATT_WIDTH, BF16, True, tm)
    vt = _proj(hb, w_in, row0, OFF_Q + 2 * ATT_WIDTH, ATT_WIDTH, BF16, True, tm)
    iqt = _proj(hb, w_in, row0, OFF_IQ, IDX_HEADS * IDX_DIM, BF16, True, tm)
    sm, smt = _proj_small(hb, w_in, row0, tm)

    y_ssd = _ssd(za, sm, conv_w, conv_b, dt_bias, a_log, d_skip, ssm_norm_w, bsz, seq)
    y_att = _dsa(qt, iqt, vt, kk, sm, smt, idx_k_norm_g, idx_k_norm_b, bsz, seq)
    h1, h1b, eidx, gates = _outproj_ln_router(
        y_ssd, y_att, w_out, h, ln1_g, ln1_b, router_w, router_b, min(256, t))

    dest, block_expert, n_used, slot_tok = _moe_plan(eidx, t)
    yb = _experts(block_expert, n_used, slot_tok, h1, w_gate, w_up, w_down, layer)
    return _ple_ln(dest, h1, h1b, yb, gates.T, p_i, ple_gate_w, ple_w, ln2_g, ln2_b, min(256, t))


def kernel(x, p, w_in, conv_w, conv_b, dt_bias, a_log, d_skip, ssm_norm_w, idx_k_norm_g,
           idx_k_norm_b, w_out, ln1_g, ln1_b, router_w, router_b, w_gate, w_up, w_down,
           ple_w, ple_gate_w, ln2_g, ln2_b):
    bsz, seq, d = x.shape
    t = bsz * seq
    h = x.reshape(t, d).astype(F32)
    hb = h.astype(BF16)
    w_in_t = jnp.swapaxes(w_in, 1, 2).reshape(-1, d)
    for i in range(w_in.shape[0]):
        h, hb = _layer(h, hb, p[i].reshape(t, -1), w_in_t, i, conv_w[i], conv_b[i], dt_bias[i],
                       a_log[i], d_skip[i], ssm_norm_w[i], idx_k_norm_g[i], idx_k_norm_b[i],
                       w_out[i], ln1_g[i], ln1_b[i], router_w, router_b, w_gate, w_up,
                       w_down, ple_w[i], ple_gate_w[i], ln2_g[i], ln2_b[i], bsz, seq)
    return h.reshape(bsz, seq, d).astype(x.dtype)
```

```python
import functools

import jax
import jax.numpy as jnp
from jax import lax
from jax.experimental import pallas as pl
from jax.experimental.pallas import tpu as pltpu

F32 = jnp.float32
BF16 = jnp.bfloat16
I32 = jnp.int32

SSD_WIDTH = 1024
SSD_HEADDIM = 64
SSD_HEADS = 16
SSD_STATE = 128
SSD_GROUPS = 2
CONV_WIDTH = 4
CHUNK = 128
CONV_DIM = SSD_WIDTH + 2 * SSD_GROUPS * SSD_STATE
ATT_HEADS = 8
ATT_HEADDIM = 128
ATT_WIDTH = ATT_HEADS * ATT_HEADDIM
IDX_HEADS = 16
IDX_DIM = 64
TOPK_MAX = 256
Q_BLOCK = 128
N_EXPERTS = 32
N_EXPERT_GROUPS = 8
EXPERTS_PER_GROUP = 4
D_FF = 512
ALPHA = (2.0 * 2) ** 0.25
LN_EPS = 1e-5

OFF_Z = 0
OFF_XBC = OFF_Z + SSD_WIDTH
OFF_DT = OFF_XBC + CONV_DIM
OFF_Q = OFF_DT + SSD_HEADS
OFF_IQ = OFF_Q + 3 * ATT_WIDTH
OFF_IK = OFF_IQ + IDX_HEADS * IDX_DIM
OFF_IW = OFF_IK + IDX_DIM
SM_DT = 0
SM_IK = SSD_HEADS
SM_IW = SM_IK + IDX_DIM
SM_W = 128

V7X_LANES = 128
VMEM_LIMIT = 56 * 1024 * 1024
FLOAT_LOWEST = -3.0e38
SEARCH_PROBES_PER_CHECK = 3
SEARCH_MAX_ITERS = 200
KEY_CHUNK = 256
DSA_QB = 256
DSA_KV_SPLIT = 4
MOE_BLOCK = 256
MOE_W_SPLIT = 4
GATHER_DMA_PRIORITY = 1
GATHER_GROUP = 8


def _sigmoid(x):
    return 1.0 / (1.0 + jnp.exp(-x))


def _softplus(x):
    return jnp.maximum(x, 0.0) + jnp.log(1.0 + jnp.exp(-jnp.abs(x)))


def _nt_dot(a, b, **kw):
    return lax.dot_general(a, b, (((1,), (1,)), ((), ())), preferred_element_type=F32, **kw)


def _layer_norm_rows(u, g, b):
    mu = jnp.mean(u, axis=-1, keepdims=True)
    d = u - mu
    var = jnp.mean(d * d, axis=-1, keepdims=True)
    return d * lax.rsqrt(var + LN_EPS) * g + b


PROJ_ROWS = 256


def _proj_kernel(x_ref, wt_ref, o_ref, w_s, *, transposed):
    tn, k = wt_ref.shape

    @pl.when(pl.program_id(1) == 0)
    def _():
        if transposed:
            w_s[...] = wt_ref[...].astype(BF16)
        else:
            def stage(r, carry):
                r0 = pl.multiple_of(r * PROJ_ROWS, PROJ_ROWS)
                w_s[pl.ds(r0, PROJ_ROWS), :] = wt_ref[:, pl.ds(r0, PROJ_ROWS)].T.astype(BF16)
                return carry

            lax.fori_loop(0, k // PROJ_ROWS, stage, 0)

    if transposed:
        o_ref[...] = _nt_dot(w_s[...], x_ref[...]).astype(o_ref.dtype)
    else:
        o_ref[...] = jnp.dot(x_ref[...], w_s[...], preferred_element_type=F32).astype(o_ref.dtype)


def _proj(x, w_in_t, row0, c0, width, out_dtype, transposed, tm, tn=512):
    m, k = x.shape
    assert (row0 + c0) % 8 == 0 and width % tn == 0 and m % tm == 0
    in_specs = [pl.BlockSpec((tm, k), lambda j, i: (i, 0)),
                pl.BlockSpec((pl.Element(tn), pl.Element(k)),
                             lambda j, i: (pl.multiple_of(row0 + c0 + j * tn, 8), 0))]
    if transposed:
        out_spec = pl.BlockSpec((tn, tm), lambda j, i: (j, i))
        out_shape = jax.ShapeDtypeStruct((width, m), out_dtype)
        scratch = pltpu.VMEM((tn, k), BF16)
    else:
        out_spec = pl.BlockSpec((tm, tn), lambda j, i: (i, j))
        out_shape = jax.ShapeDtypeStruct((m, width), out_dtype)
        scratch = pltpu.VMEM((k, tn), BF16)
    return pl.pallas_call(
        functools.partial(_proj_kernel, transposed=transposed),
        name="in_proj_t" if transposed else "in_proj",
        grid=(width // tn, m // tm),
        in_specs=in_specs,
        out_specs=out_spec,
        out_shape=out_shape,
        scratch_shapes=[scratch],
        compiler_params=pltpu.CompilerParams(
            dimension_semantics=("parallel", "arbitrary"), vmem_limit_bytes=VMEM_LIMIT),
    )(x, w_in_t)


def _proj_small_kernel(x_ref, wa_ref, wb_ref, o_ref, ot_ref, w_s, wt_s):
    @pl.when(pl.program_id(0) == 0)
    def _():
        k = wa_ref.shape[1]
        tail = SM_W - IDX_DIM - IDX_HEADS
        wt = jnp.concatenate(
            [wa_ref[0:SSD_HEADS, :], wb_ref[tail:SM_W, :],
             jnp.zeros((SM_W - SSD_HEADS - IDX_DIM - IDX_HEADS, k), F32)], axis=0)
        wt_s[...] = wt.astype(BF16)
        w_s[...] = wt.T.astype(BF16)

    o_ref[...] = jnp.dot(x_ref[...], w_s[...], preferred_element_type=F32)
    ot_ref[...] = _nt_dot(wt_s[...], x_ref[...])


def _proj_small(x, w_in_t, row0, tm):
    m, k = x.shape
    d_in = OFF_IW + IDX_HEADS
    assert OFF_IW - OFF_IK == IDX_DIM and (row0 + OFF_DT) % 8 == 0 and (row0 + d_in) % 8 == 0
    return pl.pallas_call(
        _proj_small_kernel,
        name="in_proj_small",
        grid=(m // tm,),
        in_specs=[pl.BlockSpec((tm, k), lambda i: (i, 0)),
                  pl.BlockSpec((pl.Element(SM_W), pl.Element(k)), lambda i: (row0 + OFF_DT, 0)),
                  pl.BlockSpec((pl.Element(SM_W), pl.Element(k)),
                               lambda i: (row0 + d_in - SM_W, 0))],
        out_specs=[pl.BlockSpec((tm, SM_W), lambda i: (i, 0)),
                   pl.BlockSpec((SM_W, tm), lambda i: (0, i))],
        out_shape=[jax.ShapeDtypeStruct((m, SM_W), F32), jax.ShapeDtypeStruct((SM_W, m), F32)],
        scratch_shapes=[pltpu.VMEM((k, SM_W), BF16), pltpu.VMEM((SM_W, k), BF16)],
        compiler_params=pltpu.CompilerParams(
            dimension_semantics=("arbitrary",), vmem_limit_bytes=VMEM_LIMIT),
    )(x, w_in_t, w_in_t)


def _ssd_kernel(za_ref, sm_ref, cw_ref, cb_ref, dtb_ref, a_ref, dexp_ref, nw_ref, e_ref,
                o_ref, cbuf, state, ybuf):
    c = pl.program_id(1)
    L = CHUNK

    @pl.when(c == 0)
    def _():
        cbuf[0:8, :] = jnp.zeros((8, CONV_DIM), F32)
        state[...] = jnp.zeros_like(state)

    cbuf[8:8 + L, :] = za_ref[:, SSD_WIDTH:SSD_WIDTH + CONV_DIM]
    acc = cb_ref[...] + cw_ref[0:1, :] * cbuf[5:5 + L, :]
    acc = acc + cw_ref[1:2, :] * cbuf[6:6 + L, :]
    acc = acc + cw_ref[2:3, :] * cbuf[7:7 + L, :]
    acc = acc + cw_ref[3:4, :] * cbuf[8:8 + L, :]
    cbuf[0:8, :] = cbuf[L:L + 8, :]
    xbc = acc * _sigmoid(acc)
    xs = xbc[:, 0:SSD_WIDTH]
    bm = xbc[:, SSD_WIDTH:SSD_WIDTH + SSD_GROUPS * SSD_STATE]
    cm = xbc[:, SSD_WIDTH + SSD_GROUPS * SSD_STATE:CONV_DIM]

    dt = _softplus(sm_ref[...] + dtb_ref[...])
    da = dt * a_ref[...]
    row = lax.broadcasted_iota(I32, (L, L), 0)
    col = lax.broadcasted_iota(I32, (L, L), 1)
    causal = col <= row
    tri = jnp.where(causal, 1.0, 0.0).astype(F32)
    a_cs = jnp.dot(tri, da, preferred_element_type=F32, precision=lax.Precision.HIGHEST)
    a_cs_t = a_cs.T
    expa = jnp.exp(a_cs)
    dte = jnp.exp(a_cs[L - 1:L, :] - a_cs)
    stacked = jnp.concatenate([dt, expa, dte], axis=0)
    expd = jnp.dot(stacked, e_ref[...], preferred_element_type=F32,
                   precision=lax.Precision.HIGHEST)
    dt_x = expd[0:L, :]
    expa_x = expd[L:2 * L, :]
    dte_x = expd[2 * L:3 * L, :]
    xdt = xs * dt_x
    hpg = SSD_HEADS // SSD_GROUPS
    gw = hpg * SSD_HEADDIM
    for g in range(SSD_GROUPS):
        cm_g = cm[:, g * SSD_STATE:(g + 1) * SSD_STATE].astype(BF16)
        bm_gf = bm[:, g * SSD_STATE:(g + 1) * SSD_STATE]
        bm_g = bm_gf.astype(BF16)
        cb = _nt_dot(cm_g, bm_g)
        for hh in range(hpg):
            h = g * hpg + hh
            seg = a_cs[:, h:h + 1] - a_cs_t[h:h + 1, :]
            dec = jnp.exp(jnp.where(causal, seg, -jnp.inf))
            mh = (cb * dec).astype(BF16)
            xh = xdt[:, h * SSD_HEADDIM:(h + 1) * SSD_HEADDIM].astype(BF16)
            ybuf[:, h * SSD_HEADDIM:(h + 1) * SSD_HEADDIM] = jnp.dot(
                mh, xh, preferred_element_type=F32)
        st_g = state[:, g * gw:(g + 1) * gw]
        y_off = jnp.dot(cm_g, st_g.astype(BF16), preferred_element_type=F32)
        ybuf[:, g * gw:(g + 1) * gw] = (ybuf[:, g * gw:(g + 1) * gw]
                                        + y_off * expa_x[:, g * gw:(g + 1) * gw])
        xw = (xdt[:, g * gw:(g + 1) * gw] * dte_x[:, g * gw:(g + 1) * gw]).astype(BF16)
        upd = jnp.dot(bm_gf.T.astype(BF16), xw, preferred_element_type=F32)
        state[:, g * gw:(g + 1) * gw] = st_g * expa_x[L - 1:L, g * gw:(g + 1) * gw] + upd

    y = ybuf[...] + xs * dexp_ref[...]
    z = za_ref[:, 0:SSD_WIDTH]
    yg = y * (z * _sigmoid(z))
    ms = jnp.mean(yg * yg, axis=-1, keepdims=True)
    o_ref[...] = (yg * lax.rsqrt(ms + LN_EPS) * nw_ref[...]).astype(o_ref.dtype)


def _ssd(za, sm, conv_w, conv_b, dt_bias, a_log, d_skip, ssm_norm_w, bsz, seq):
    nc = seq // CHUNK
    t = bsz * seq
    pad = SM_W - SSD_HEADS
    dtb = jnp.pad(dt_bias.astype(F32), (0, pad)).reshape(1, SM_W)
    a = jnp.pad(-jnp.exp(a_log.astype(F32)), (0, pad)).reshape(1, SM_W)
    dexp = jnp.repeat(d_skip.astype(F32), SSD_HEADDIM).reshape(1, SSD_WIDTH)
    e = (jnp.arange(SSD_WIDTH)[None, :] // SSD_HEADDIM == jnp.arange(SM_W)[:, None]).astype(F32)
    const = lambda shape: pl.BlockSpec(shape, lambda b, c: (0, 0))
    return pl.pallas_call(
        _ssd_kernel,
        name="ssd",
        grid=(bsz, nc),
        in_specs=[pl.BlockSpec((CHUNK, SSD_WIDTH + CONV_DIM), lambda b, c: (b * nc + c, 0)),
                  pl.BlockSpec((CHUNK, SM_W), lambda b, c: (b * nc + c, 0)),
                  const((CONV_WIDTH, CONV_DIM)), const((1, CONV_DIM)), const((1, SM_W)),
                  const((1, SM_W)), const((1, SSD_WIDTH)), const((1, SSD_WIDTH)),
                  const((SM_W, SSD_WIDTH))],
        out_specs=pl.BlockSpec((CHUNK, SSD_WIDTH), lambda b, c: (b * nc + c, 0)),
        out_shape=jax.ShapeDtypeStruct((t, SSD_WIDTH), BF16),
        scratch_shapes=[pltpu.VMEM((CHUNK + 8, CONV_DIM), F32),
                        pltpu.VMEM((SSD_STATE, SSD_WIDTH), F32),
                        pltpu.VMEM((CHUNK, SSD_WIDTH), F32)],
        compiler_params=pltpu.CompilerParams(
            dimension_semantics=("parallel", "arbitrary"), vmem_limit_bytes=VMEM_LIMIT),
    )(za, sm, conv_w.astype(F32), conv_b.astype(F32).reshape(1, CONV_DIM), dtb, a, dexp,
      ssm_norm_w.astype(F32).reshape(1, SSD_WIDTH), e)


def _dsa_kernel(qt_ref, iqt_ref, *rest, topk):
    ns = DSA_KV_SPLIT
    vt_refs, k_refs = rest[0:ns], rest[ns:2 * ns]
    smb_ref, smt_ref, g_ref, b_ref, o_ref, ikn, keys, acc_s, s_s, p_s, bias_s = rest[2 * ns:]
    slab = ATT_WIDTH // ns
    qi = pl.program_id(1)
    QB = o_ref.shape[0]
    KC = KEY_CHUNK

    @pl.when(qi == 0)
    def _():
        ik = smb_ref[:, SM_IK:SM_IK + IDX_DIM]
        ikn[...] = _layer_norm_rows(ik, g_ref[...], b_ref[...]).astype(BF16)

    nkc = (qi * QB + QB - 1) // KC + 1
    w = smt_ref[SM_IW:SM_IW + IDX_HEADS, :] * (IDX_DIM ** -0.5 * IDX_HEADS ** -0.5)
    qpos = qi * QB + lax.broadcasted_iota(I32, (KC, QB), 1)
    kiota = lax.broadcasted_iota(I32, (KC, QB), 0)

    def score_chunk(c, carry):
        rmax, rmin = carry
        off = pl.multiple_of(c * KC, KC)
        kc = ikn[pl.ds(off, KC), :]
        acc = jnp.zeros((KC, QB), F32)
        for h in range(IDX_HEADS):
            d = jnp.dot(kc, iqt_ref[h * IDX_DIM:(h + 1) * IDX_DIM, :], preferred_element_type=F32)
            acc = acc + w[h:h + 1, :] * jnp.maximum(d, 0.0)
        vis = off + kiota <= qpos
        sv = jnp.where(vis, acc, -jnp.inf)
        keys[c] = sv
        rmax = jnp.maximum(rmax, jnp.max(sv, axis=0, keepdims=True))
        rmin = jnp.minimum(rmin, jnp.min(jnp.where(vis, acc, jnp.inf), axis=0, keepdims=True))
        return rmax, rmin

    rmax, rmin = lax.fori_loop(
        0, nkc, score_chunk,
        (jnp.full((1, QB), -jnp.inf, F32), jnp.full((1, QB), jnp.inf, F32)))

    kf = float(topk)

    def count_ge(mid):
        def count_chunk(c, acc):
            m = jnp.where(keys[c] >= mid, 1.0, 0.0)
            return acc + jnp.sum(m.reshape(KC // 8, 8, QB), axis=0)

        acc = lax.fori_loop(0, nkc, count_chunk, jnp.zeros((8, QB), F32))
        return jnp.sum(acc, axis=0, keepdims=True)

    nvis = (qi * QB + lax.broadcasted_iota(I32, (1, QB), 1) + 1).astype(F32)
    small = nvis <= kf

    def search_cond(st):
        return (st[0] > 0.0) & (st[1] < SEARCH_MAX_ITERS)

    def search_body(st):
        _, it, lo, hi, clo = st
        for _ in range(SEARCH_PROBES_PER_CHECK):
            mid = 0.5 * lo + 0.5 * hi
            splittable = (mid > lo) & (mid < hi)
            c = count_ge(mid)
            up = splittable & (c >= kf)
            down = splittable & (c < kf)
            lo = jnp.where(up, mid, lo)
            clo = jnp.where(up, c, clo)
            hi = jnp.where(down, mid, hi)
        active = jnp.logical_not(small) & splittable & (clo != kf)
        return jnp.sum(jnp.where(active, 1.0, 0.0)), it + 1, lo, hi, clo

    hi0 = rmax + jnp.abs(rmax) * 1e-6 + 1e-37
    st = lax.while_loop(search_cond, search_body,
                        (jnp.float32(1.0), jnp.int32(0), rmin, hi0, nvis))
    thr = jnp.where(small, FLOAT_LOWEST, st[2])

    scale = ATT_HEADDIM ** -0.5
    acc_s[...] = jnp.zeros_like(acc_s)

    def att_chunk(c, carry):
        ms, ls = carry
        off = pl.multiple_of(c * KC, KC)
        bias_s[...] = jnp.where(keys[c] >= thr, 0.0, -1e30)
        for h in range(ATT_HEADS):
            lo, hi = h * ATT_HEADDIM, (h + 1) * ATT_HEADDIM
            s_s[h] = jnp.dot(k_refs[lo // slab][pl.ds(off, KC), lo % slab:lo % slab + ATT_HEADDIM],
                             qt_ref[lo:hi, :],
                             preferred_element_type=F32)
        new_ms, new_ls, alphas = [], [], []
        for h in range(ATT_HEADS):
            s = s_s[h] * scale + bias_s[...]
            m_new = jnp.maximum(ms[h], jnp.max(s, axis=0, keepdims=True))
            alpha = jnp.exp(ms[h] - m_new)
            p = jnp.exp(s - m_new)
            new_ls.append(alpha * ls[h] + jnp.sum(p, axis=0, keepdims=True))
            new_ms.append(m_new)
            alphas.append(alpha)
            p_s[h] = p.astype(BF16)
        for h in range(ATT_HEADS):
            lo, hi = h * ATT_HEADDIM, (h + 1) * ATT_HEADDIM
            pv = jnp.dot(vt_refs[lo // slab][lo % slab:lo % slab + ATT_HEADDIM, pl.ds(off, KC)],
                         p_s[h],
                         preferred_element_type=F32)
            acc_s[h] = alphas[h] * acc_s[h] + pv
        return tuple(new_ms), tuple(new_ls)

    m0 = tuple(jnp.full((1, QB), -1e30, F32) for _ in range(ATT_HEADS))
    l0 = tuple(jnp.zeros((1, QB), F32) for _ in range(ATT_HEADS))
    _, ls = lax.fori_loop(0, nkc, att_chunk, (m0, l0))
    for h in range(ATT_HEADS):
        o_ref[:, h * ATT_HEADDIM:(h + 1) * ATT_HEADDIM] = (acc_s[h] / ls[h]).T.astype(o_ref.dtype)


def _dsa(qt, iqt, vt, kk, sm, smt, g, b, bsz, seq):
    qb = min(DSA_QB, seq)
    nq = seq // qb
    t = bsz * seq
    topk = min(TOPK_MAX, seq // 4)
    single = pl.Buffered(1)
    ns = DSA_KV_SPLIT
    slab = ATT_WIDTH // ns
    vt_specs = [pl.BlockSpec((slab, seq), lambda bi, qi, q=q: (q, bi), pipeline_mode=single)
                for q in range(ns)]
    k_specs = [pl.BlockSpec((seq, slab), lambda bi, qi, q=q: (bi, q), pipeline_mode=single)
               for q in range(ns)]
    return pl.pallas_call(
        functools.partial(_dsa_kernel, topk=topk),
        name="dsa",
        grid=(bsz, nq),
        in_specs=[pl.BlockSpec((ATT_WIDTH, qb), lambda bi, qi: (0, bi * nq + qi)),
                  pl.BlockSpec((IDX_HEADS * IDX_DIM, qb), lambda bi, qi: (0, bi * nq + qi)),
                  *vt_specs, *k_specs,
                  pl.BlockSpec((seq, SM_W), lambda bi, qi: (bi, 0), pipeline_mode=single),
                  pl.BlockSpec((SM_W, qb), lambda bi, qi: (0, bi * nq + qi)),
                  pl.BlockSpec((1, IDX_DIM), lambda bi, qi: (0, 0)),
                  pl.BlockSpec((1, IDX_DIM), lambda bi, qi: (0, 0))],
        out_specs=pl.BlockSpec((qb, ATT_WIDTH), lambda bi, qi: (bi * nq + qi, 0)),
        out_shape=jax.ShapeDtypeStruct((t, ATT_WIDTH), BF16),
        scratch_shapes=[pltpu.VMEM((seq, IDX_DIM), BF16),
                        pltpu.VMEM((seq // KEY_CHUNK, KEY_CHUNK, qb), F32),
                        pltpu.VMEM((ATT_HEADS, ATT_HEADDIM, qb), F32),
                        pltpu.VMEM((ATT_HEADS, KEY_CHUNK, qb), F32),
                        pltpu.VMEM((ATT_HEADS, KEY_CHUNK, qb), BF16),
                        pltpu.VMEM((KEY_CHUNK, qb), F32)],
        compiler_params=pltpu.CompilerParams(
            dimension_semantics=("parallel", "arbitrary"), vmem_limit_bytes=VMEM_LIMIT),
    )(qt, iqt, *([vt] * ns), *([kk] * ns), sm, smt, g.astype(F32).reshape(1, IDX_DIM),
      b.astype(F32).reshape(1, IDX_DIM))


def _route_rows(logits_t, rb):
    aff = _sigmoid(logits_t)
    biased = aff + rb
    rb_rows = [biased[r:r + 1, :] for r in range(N_EXPERTS)]
    ra_rows = [aff[r:r + 1, :] for r in range(N_EXPERTS)]
    epg = EXPERTS_PER_GROUP
    gs = []
    for g in range(N_EXPERT_GROUPS):
        a, b, c, d = rb_rows[epg * g:epg * g + epg]
        p, q = jnp.maximum(a, b), jnp.minimum(a, b)
        r, s = jnp.maximum(c, d), jnp.minimum(c, d)
        gs.append(jnp.maximum(p, r) + jnp.maximum(jnp.minimum(p, r), jnp.maximum(q, s)))
    best = gs[0]
    bidx = jnp.zeros_like(best, dtype=I32)
    for g in range(1, N_EXPERT_GROUPS):
        better = gs[g] > best
        best = jnp.where(better, gs[g], best)
        bidx = jnp.where(better, g, bidx)
    vb = [rb_rows[j] for j in range(epg)]
    va = [ra_rows[j] for j in range(epg)]
    for g in range(1, N_EXPERT_GROUPS):
        pick = bidx == g
        vb = [jnp.where(pick, rb_rows[epg * g + j], vb[j]) for j in range(epg)]
        va = [jnp.where(pick, ra_rows[epg * g + j], va[j]) for j in range(epg)]
    t1, a1, i1 = vb[0], va[0], jnp.zeros_like(bidx)
    for j in range(1, epg):
        better = vb[j] > t1
        t1 = jnp.where(better, vb[j], t1)
        a1 = jnp.where(better, va[j], a1)
        i1 = jnp.where(better, j, i1)
    t2 = jnp.full_like(t1, -jnp.inf)
    a2 = jnp.zeros_like(a1)
    i2 = jnp.zeros_like(i1)
    for j in range(epg):
        vj = jnp.where(i1 == j, -jnp.inf, vb[j])
        better = vj > t2
        t2 = jnp.where(better, vj, t2)
        a2 = jnp.where(better, va[j], a2)
        i2 = jnp.where(better, j, i2)
    den = a1 + a2
    return bidx * epg + i1, bidx * epg + i2, a1 / den, a2 / den


def _outproj_kernel(ys_ref, ya_ref, w1_ref, w2_ref, h_ref, g_ref, b_ref, rwh_ref, rwl_ref, rb_ref,
                    h1_ref, h1b_ref, eidx_ref, gate_ref):
    mix = jnp.dot(ys_ref[...], w1_ref[...], preferred_element_type=F32)
    mix = mix + jnp.dot(ya_ref[...], w2_ref[...], preferred_element_type=F32)
    h1 = _layer_norm_rows(ALPHA * h_ref[...] + mix, g_ref[...], b_ref[...])
    h1_ref[...] = h1
    h1_hi = h1.astype(BF16)
    h1b_ref[...] = h1_hi
    h1_lo = (h1 - h1_hi.astype(F32)).astype(BF16)
    logits = jnp.dot(h1_hi, rwh_ref[...], preferred_element_type=F32)
    logits = logits + jnp.dot(h1_hi, rwl_ref[...], preferred_element_type=F32)
    logits = logits + jnp.dot(h1_lo, rwh_ref[...], preferred_element_type=F32)
    logits_t = logits.T[0:N_EXPERTS, :]
    e1, e2, g1, g2 = _route_rows(logits_t, rb_ref[...])
    eidx_ref[0:1, :] = e1
    eidx_ref[1:2, :] = e2
    gate_ref[0:1, :] = g1
    gate_ref[1:2, :] = g2


def _outproj_ln_router(y_ssd, y_att, w_out, h, ln_g, ln_b, router_w, router_b, tm):
    t, d = h.shape
    w = w_out.astype(BF16)
    rw = jnp.pad(router_w.astype(F32), ((0, 0), (0, V7X_LANES - N_EXPERTS)))
    rw_hi = rw.astype(BF16)
    rw_lo = (rw - rw_hi.astype(F32)).astype(BF16)
    const = lambda shape: pl.BlockSpec(shape, lambda i: (0, 0))
    return pl.pallas_call(
        _outproj_kernel,
        name="outproj_ln_router",
        grid=(t // tm,),
        in_specs=[pl.BlockSpec((tm, SSD_WIDTH), lambda i: (i, 0)),
                  pl.BlockSpec((tm, ATT_WIDTH), lambda i: (i, 0)),
                  pl.BlockSpec((SSD_WIDTH, d), lambda i: (0, 0)),
                  pl.BlockSpec((ATT_WIDTH, d), lambda i: (1, 0)),
                  pl.BlockSpec((tm, d), lambda i: (i, 0)),
                  const((1, d)), const((1, d)), const((d, V7X_LANES)), const((d, V7X_LANES)),
                  const((N_EXPERTS, 1))],
        out_specs=[pl.BlockSpec((tm, d), lambda i: (i, 0)),
                   pl.BlockSpec((tm, d), lambda i: (i, 0)),
                   pl.BlockSpec((2, tm), lambda i: (0, i)),
                   pl.BlockSpec((2, tm), lambda i: (0, i))],
        out_shape=[jax.ShapeDtypeStruct((t, d), F32), jax.ShapeDtypeStruct((t, d), BF16),
                   jax.ShapeDtypeStruct((2, t), I32), jax.ShapeDtypeStruct((2, t), F32)],
        compiler_params=pltpu.CompilerParams(
            dimension_semantics=("parallel",), vmem_limit_bytes=VMEM_LIMIT),
    )(y_ssd, y_att, w, w, h, ln_g.astype(F32).reshape(1, d), ln_b.astype(F32).reshape(1, d),
      rw_hi, rw_lo, router_b.astype(F32).reshape(N_EXPERTS, 1))


def _expert_kernel(be_ref, run_ref, nxt_ref, nv_ref, nu_ref, st_ref, h_hbm, wg_hbm, wu_hbm, wd_hbm,
                   o_ref, xbuf, wg_f, wu_f, wd_f, wg_s, wu_s, wd_s, xsem, wsem, *, layer):
    i = pl.program_id(0)
    nu = nu_ref[0]
    slot = lax.rem(i, 2)
    ns = MOE_W_SPLIT

    def row_copy(tok, r, s):
        return pltpu.make_async_copy(h_hbm.at[pl.ds(tok, 1), :], xbuf.at[s, pl.ds(r, 1), :],
                                     xsem.at[s])

    def row_groups(blk):
        return (nv_ref[blk] + GATHER_GROUP - 1) // GATHER_GROUP

    def issue_rows(blk, s):
        def body(gi, carry):
            for u in range(GATHER_GROUP):
                r = gi * GATHER_GROUP + u
                row_copy(st_ref[blk * MOE_BLOCK + r], r, s).start(priority=GATHER_DMA_PRIORITY)
            return carry
        lax.fori_loop(0, row_groups(blk), body, 0)

    def wait_rows(blk, s):
        def body(gi, carry):
            for u in range(GATHER_GROUP):
                row_copy(0, gi * GATHER_GROUP + u, s).wait()
            return carry
        lax.fori_loop(0, row_groups(blk), body, 0)

    def weight_copies(e, ws):
        copies = []
        for src, dst in ((wg_hbm, wg_f), (wu_hbm, wu_f), (wd_hbm, wd_f)):
            rows = dst.shape[1] // ns
            for q in range(ns):
                copies.append(pltpu.make_async_copy(
                    src.at[layer, e, pl.ds(q * rows, rows), :],
                    dst.at[ws, pl.ds(q * rows, rows), :], wsem.at[ws]))
        return copies

    @pl.when(i == 0)
    def _():
        xbuf[...] = jnp.zeros_like(xbuf)
        issue_rows(0, 0)
        for c in weight_copies(be_ref[0], 0):
            c.start()

    @pl.when(i < nu)
    def _():
        wait_rows(i, slot)

        @pl.when(i + 1 < nu)
        def _():
            issue_rows(i + 1, 1 - slot)

        ws = lax.rem(run_ref[i], 2)

        @pl.when((i == 0) | (be_ref[i] != be_ref[jnp.maximum(i - 1, 0)]))
        def _():
            for c in weight_copies(be_ref[i], ws):
                c.wait()
            wg_s[...] = wg_f[ws].astype(BF16)
            wu_s[...] = wu_f[ws].astype(BF16)
            wd_s[...] = wd_f[ws].astype(BF16)

            @pl.when(nxt_ref[i] >= 0)
            def _():
                for c in weight_copies(nxt_ref[i], 1 - ws):
                    c.start()

        x = xbuf[slot].astype(BF16)
        g = jnp.dot(x, wg_s[...], preferred_element_type=F32)
        u = jnp.dot(x, wu_s[...], preferred_element_type=F32)
        a = (g * _sigmoid(g) * u).astype(BF16)
        o_ref[...] = jnp.dot(a, wd_s[...], preferred_element_type=F32)

    @pl.when(i >= nu)
    def _():
        o_ref[...] = jnp.zeros_like(o_ref)


def _experts(plan, h1, w_gate, w_up, w_down, layer):
    t, d = h1.shape
    cap = plan["slot_tok"].shape[0]
    nb = cap // MOE_BLOCK
    hbm = pl.BlockSpec(memory_space=pl.ANY)
    return pl.pallas_call(
        functools.partial(_expert_kernel, layer=layer),
        name="moe_experts",
        grid_spec=pltpu.PrefetchScalarGridSpec(
            num_scalar_prefetch=6,
            grid=(nb,),
            in_specs=[hbm, hbm, hbm, hbm],
            out_specs=pl.BlockSpec((MOE_BLOCK, d), lambda i, *_: (i, 0)),
            scratch_shapes=[pltpu.VMEM((2, MOE_BLOCK, d), F32),
                            pltpu.VMEM((2, d, D_FF), F32), pltpu.VMEM((2, d, D_FF), F32),
                            pltpu.VMEM((2, D_FF, d), F32),
                            pltpu.VMEM((d, D_FF), BF16), pltpu.VMEM((d, D_FF), BF16),
                            pltpu.VMEM((D_FF, d), BF16),
                            pltpu.SemaphoreType.DMA((2,)), pltpu.SemaphoreType.DMA((2,))]),
        out_shape=jax.ShapeDtypeStruct((cap, d), F32),
        compiler_params=pltpu.CompilerParams(
            dimension_semantics=("arbitrary",), vmem_limit_bytes=VMEM_LIMIT),
    )(plan["block_expert"], plan["run"], plan["next_expert"], plan["valid_rows"], plan["n_used"],
      plan["slot_tok"], h1, w_gate, w_up, w_down)


def _moe_plan(eidx, t):
    flat_e = eidx.reshape(2 * t)
    onehot = (flat_e[:, None] == jnp.arange(N_EXPERTS, dtype=I32)[None, :]).astype(I32)
    cs = jnp.cumsum(onehot, axis=0)
    rank = jnp.sum(onehot * cs, axis=1) - 1
    counts = cs[-1]
    padded = (counts + MOE_BLOCK - 1) // MOE_BLOCK * MOE_BLOCK
    pad_end = jnp.cumsum(padded)
    pad_start = pad_end - padded
    dest = (pad_start[flat_e] + rank).astype(I32)
    cap = 2 * t + N_EXPERTS * MOE_BLOCK
    nb = cap // MOE_BLOCK
    block_start = jnp.arange(nb, dtype=I32) * MOE_BLOCK
    block_expert = jnp.minimum(
        jnp.sum((pad_end[None, :] <= block_start[:, None]).astype(I32), axis=1), N_EXPERTS - 1)
    n_used = (pad_end[-1] // MOE_BLOCK).astype(I32).reshape(1)
    tok = jnp.tile(jnp.arange(t, dtype=I32), 2)
    slot_tok = jnp.zeros((cap,), I32).at[dest].set(tok, unique_indices=True)
    blk = jnp.arange(nb, dtype=I32)
    used = blk < n_used[0]
    valid_rows = jnp.where(
        used, jnp.clip(counts[block_expert] - (block_start - pad_start[block_expert]), 0, MOE_BLOCK), 0)
    first = used & jnp.concatenate([jnp.ones((1,), bool), block_expert[1:] != block_expert[:-1]])
    run = jnp.cumsum(first.astype(I32)) - 1
    later = (block_expert[None, :] > block_expert[:, None]) & used[None, :]
    nxt = jnp.min(jnp.where(later, block_expert[None, :], N_EXPERTS), axis=1)
    next_expert = jnp.where(nxt < N_EXPERTS, nxt, -1)
    plan = dict(block_expert=block_expert, run=run.astype(I32), next_expert=next_expert.astype(I32),
                valid_rows=valid_rows.astype(I32), n_used=n_used, slot_tok=slot_tok)
    return dest, plan


def _ple_kernel(dest_ref, h1_ref, h1b_ref, yb_hbm, gt_ref, p_ref, wg_ref, wp_ref, g_ref, b_ref,
                o_ref, ob_ref, gbuf, sem):
    i = pl.program_id(0)
    n = pl.num_programs(0)
    tm = h1_ref.shape[0]
    t = n * tm
    slot = lax.rem(i, 2)
    nxt_base = jnp.minimum(i + 1, n - 1) * tm

    def row_copy(src_row, k, r, s):
        return pltpu.make_async_copy(yb_hbm.at[pl.ds(src_row, 1), :],
                                     gbuf.at[s, pl.ds(k * tm + r, 1), :], sem.at[s])

    def wait_tile(s):
        pltpu.make_async_copy(yb_hbm.at[pl.ds(0, 2 * tm), :], gbuf.at[s], sem.at[s]).wait()

    @pl.when(i == 0)
    def _():
        def body(r, carry):
            for k in range(2):
                row_copy(dest_ref[k * t + r], k, r, 0).start(priority=k)
            return carry
        lax.fori_loop(0, tm, body, 0)

    wait_tile(slot)
    for r in range(tm):
        for k in range(2):
            row_copy(dest_ref[k * t + nxt_base + r], k, r, 1 - slot).start(priority=k)
    ffn = gt_ref[:, 0:1] * gbuf[slot, 0:tm, :] + gt_ref[:, 1:2] * gbuf[slot, tm:2 * tm, :]
    gate = _sigmoid(jnp.dot(h1b_ref[...], wg_ref[...], preferred_element_type=F32))
    pe = jnp.dot(p_ref[...].astype(BF16), wp_ref[...], preferred_element_type=F32)
    u = ALPHA * h1_ref[...] + ffn + gate * pe
    h2 = _layer_norm_rows(u, g_ref[...], b_ref[...])
    o_ref[...] = h2
    ob_ref[...] = h2.astype(BF16)

    @pl.when(i == n - 1)
    def _():
        wait_tile(1 - slot)


def _ple_ln(dest, h1, h1b, yb, gates_t, p, ple_gate_w, ple_w, ln_g, ln_b, tm):
    t, d = h1.shape
    pd = p.shape[1]
    row = lambda w: pl.BlockSpec((tm, w), lambda i, dest: (i, 0))
    const = lambda shape: pl.BlockSpec(shape, lambda i, dest: (0, 0), pipeline_mode=pl.Buffered(1))
    return pl.pallas_call(
        _ple_kernel,
        name="ple_ln",
        grid_spec=pltpu.PrefetchScalarGridSpec(
            num_scalar_prefetch=1,
            grid=(t // tm,),
            in_specs=[row(d), row(d), pl.BlockSpec(memory_space=pl.ANY), row(2), row(pd),
                      const((d, d)), const((pd, d)), const((1, d)), const((1, d))],
            out_specs=[row(d), row(d)],
            scratch_shapes=[pltpu.VMEM((2, 2 * tm, d), F32), pltpu.SemaphoreType.DMA((2,))]),
        out_shape=[jax.ShapeDtypeStruct((t, d), F32), jax.ShapeDtypeStruct((t, d), BF16)],
        compiler_params=pltpu.CompilerParams(
            dimension_semantics=("arbitrary",), vmem_limit_bytes=VMEM_LIMIT),
    )(dest, h1, h1b, yb, gates_t, p, ple_gate_w.astype(BF16), ple_w.astype(BF16),
      ln_g.astype(F32).reshape(1, d), ln_b.astype(F32).reshape(1, d))


def _layer(h, hb, p_i, w_in, layer, conv_w, conv_b, dt_bias, a_log, d_skip, ssm_norm_w,
           idx_k_norm_g, idx_k_norm_b, w_out, ln1_g, ln1_b, router_w, router_b,
           w_gate, w_up, w_down, ple_w, ple_gate_w, ln2_g, ln2_b, bsz, seq):
    t = bsz * seq
    tm = min(1024, t)
    row0 = layer * (OFF_IW + IDX_HEADS)
    za = _proj(hb, w_in, row0, OFF_Z, OFF_DT - OFF_Z, F32, False, tm)
    kk = _proj(hb, w_in, row0, OFF_Q + ATT_WIDTH, ATT_WIDTH, BF16, False, tm)
    qt = _proj(hb, w_in, row0, OFF_Q, ATT_WIDTH, BF16, True, tm)
    vt = _proj(hb, w_in, row0, OFF_Q + 2 * ATT_WIDTH, ATT_WIDTH, BF16, True, tm)
    iqt = _proj(hb, w_in, row0, OFF_IQ, IDX_HEADS * IDX_DIM, BF16, True, tm)
    sm, smt = _proj_small(hb, w_in, row0, tm)

    y_ssd = _ssd(za, sm, conv_w, conv_b, dt_bias, a_log, d_skip, ssm_norm_w, bsz, seq)
    y_att = _dsa(qt, iqt, vt, kk, sm, smt, idx_k_norm_g, idx_k_norm_b, bsz, seq)
    h1, h1b, eidx, gates = _outproj_ln_router(
        y_ssd, y_att, w_out, h, ln1_g, ln1_b, router_w, router_b, min(256, t))

    dest, plan = _moe_plan(eidx, t)
    yb = _experts(plan, h1, w_gate, w_up, w_down, layer)
    return _ple_ln(dest, h1, h1b, yb, gates.T, p_i, ple_gate_w, ple_w, ln2_g, ln2_b, min(256, t))


def kernel(x, p, w_in, conv_w, conv_b, dt_bias, a_log, d_skip, ssm_norm_w, idx_k_norm_g,
           idx_k_norm_b, w_out, ln1_g, ln1_b, router_w, router_b, w_gate, w_up, w_down,
           ple_w, ple_gate_w, ln2_g, ln2_b):
    bsz, seq, d = x.shape
    t = bsz * seq
    h = x.reshape(t, d).astype(F32)
    hb = h.astype(BF16)
    w_in_t = jnp.swapaxes(w_in, 1, 2).reshape(-1, d)
    for i in range(w_in.shape[0]):
        h, hb = _layer(h, hb, p[i].reshape(t, -1), w_in_t, i, conv_w[i], conv_b[i], dt_bias[i],
                       a_log[i], d_skip[i], ssm_norm_w[i], idx_k_norm_g[i], idx_k_norm_b[i],
                       w_out[i], ln1_g[i], ln1_b[i], router_w, router_b, w_gate, w_up,
                       w_down, ple_w[i], ple_gate_w[i], ln2_g[i], ln2_b[i], bsz, seq)
    return h.reshape(bsz, seq, d).astype(x.dtype)
```

```python
import functools

import jax
import jax.numpy as jnp
from jax import lax
from jax.experimental import pallas as pl
from jax.experimental.pallas import tpu as pltpu

F32 = jnp.float32
BF16 = jnp.bfloat16
I32 = jnp.int32

SSD_WIDTH = 1024
SSD_HEADDIM = 64
SSD_HEADS = 16
SSD_STATE = 128
SSD_GROUPS = 2
CONV_WIDTH = 4
CHUNK = 128
CONV_DIM = SSD_WIDTH + 2 * SSD_GROUPS * SSD_STATE
ATT_HEADS = 8
ATT_HEADDIM = 128
ATT_WIDTH = ATT_HEADS * ATT_HEADDIM
IDX_HEADS = 16
IDX_DIM = 64
TOPK_MAX = 256
Q_BLOCK = 128
N_EXPERTS = 32
N_EXPERT_GROUPS = 8
EXPERTS_PER_GROUP = 4
D_FF = 512
ALPHA = (2.0 * 2) ** 0.25
LN_EPS = 1e-5

OFF_Z = 0
OFF_XBC = OFF_Z + SSD_WIDTH
OFF_DT = OFF_XBC + CONV_DIM
OFF_Q = OFF_DT + SSD_HEADS
OFF_IQ = OFF_Q + 3 * ATT_WIDTH
OFF_IK = OFF_IQ + IDX_HEADS * IDX_DIM
OFF_IW = OFF_IK + IDX_DIM
SM_DT = 0
SM_IK = SSD_HEADS
SM_IW = SM_IK + IDX_DIM
SM_W = 128

V7X_LANES = 128
VMEM_LIMIT = 56 * 1024 * 1024
FLOAT_LOWEST = -3.0e38
SEARCH_PROBES_PER_CHECK = 3
SEARCH_MAX_ITERS = 200
KEY_CHUNK = 512
SEARCH_CHUNK = 256
DSA_QB = 256
DSA_KV_SPLIT = 4
MOE_BLOCK = 256
MOE_W_SPLIT = 4
GATHER_DMA_PRIORITY = 1
GATHER_GROUP = 8


def _sigmoid(x):
    return 1.0 / (1.0 + jnp.exp(-x))


def _softplus(x):
    return jnp.maximum(x, 0.0) + jnp.log(1.0 + jnp.exp(-jnp.abs(x)))


def _nt_dot(a, b, **kw):
    return lax.dot_general(a, b, (((1,), (1,)), ((), ())), preferred_element_type=F32, **kw)


def _layer_norm_rows(u, g, b):
    mu = jnp.mean(u, axis=-1, keepdims=True)
    d = u - mu
    var = jnp.mean(d * d, axis=-1, keepdims=True)
    return d * lax.rsqrt(var + LN_EPS) * g + b


PROJ_ROWS = 256


def _proj_kernel(x_ref, wt_ref, o_ref, w_s, *, transposed, out_scale):
    tn, k = wt_ref.shape

    @pl.when(pl.program_id(1) == 0)
    def _():
        if transposed:
            w_s[...] = wt_ref[...].astype(BF16)
        else:
            def stage(r, carry):
                r0 = pl.multiple_of(r * PROJ_ROWS, PROJ_ROWS)
                w_s[pl.ds(r0, PROJ_ROWS), :] = wt_ref[:, pl.ds(r0, PROJ_ROWS)].T.astype(BF16)
                return carry

            lax.fori_loop(0, k // PROJ_ROWS, stage, 0)

    if transposed:
        y = _nt_dot(w_s[...], x_ref[...])
    else:
        y = jnp.dot(x_ref[...], w_s[...], preferred_element_type=F32)
    if out_scale != 1.0:
        y = y * out_scale
    o_ref[...] = y.astype(o_ref.dtype)


def _proj(x, w_in_t, row0, c0, width, out_dtype, transposed, tm, tn=512, out_scale=1.0):
    m, k = x.shape
    assert (row0 + c0) % 8 == 0 and width % tn == 0 and m % tm == 0
    in_specs = [pl.BlockSpec((tm, k), lambda j, i: (i, 0)),
                pl.BlockSpec((pl.Element(tn), pl.Element(k)),
                             lambda j, i: (pl.multiple_of(row0 + c0 + j * tn, 8), 0))]
    if transposed:
        out_spec = pl.BlockSpec((tn, tm), lambda j, i: (j, i))
        out_shape = jax.ShapeDtypeStruct((width, m), out_dtype)
        scratch = pltpu.VMEM((tn, k), BF16)
    else:
        out_spec = pl.BlockSpec((tm, tn), lambda j, i: (i, j))
        out_shape = jax.ShapeDtypeStruct((m, width), out_dtype)
        scratch = pltpu.VMEM((k, tn), BF16)
    return pl.pallas_call(
        functools.partial(_proj_kernel, transposed=transposed, out_scale=out_scale),
        name="in_proj_t" if transposed else "in_proj",
        grid=(width // tn, m // tm),
        in_specs=in_specs,
        out_specs=out_spec,
        out_shape=out_shape,
        scratch_shapes=[scratch],
        compiler_params=pltpu.CompilerParams(
            dimension_semantics=("parallel", "arbitrary"), vmem_limit_bytes=VMEM_LIMIT),
    )(x, w_in_t)


def _proj_small_kernel(x_ref, wa_ref, wb_ref, o_ref, ot_ref, w_s, wt_s):
    @pl.when(pl.program_id(0) == 0)
    def _():
        k = wa_ref.shape[1]
        tail = SM_W - IDX_DIM - IDX_HEADS
        wt = jnp.concatenate(
            [wa_ref[0:SSD_HEADS, :], wb_ref[tail:SM_W, :],
             jnp.zeros((SM_W - SSD_HEADS - IDX_DIM - IDX_HEADS, k), F32)], axis=0)
        wt_s[...] = wt.astype(BF16)
        w_s[...] = wt.T.astype(BF16)

    o_ref[...] = jnp.dot(x_ref[...], w_s[...], preferred_element_type=F32)
    ot_ref[...] = _nt_dot(wt_s[...], x_ref[...])


def _proj_small(x, w_in_t, row0, tm):
    m, k = x.shape
    d_in = OFF_IW + IDX_HEADS
    assert OFF_IW - OFF_IK == IDX_DIM and (row0 + OFF_DT) % 8 == 0 and (row0 + d_in) % 8 == 0
    return pl.pallas_call(
        _proj_small_kernel,
        name="in_proj_small",
        grid=(m // tm,),
        in_specs=[pl.BlockSpec((tm, k), lambda i: (i, 0)),
                  pl.BlockSpec((pl.Element(SM_W), pl.Element(k)), lambda i: (row0 + OFF_DT, 0)),
                  pl.BlockSpec((pl.Element(SM_W), pl.Element(k)),
                               lambda i: (row0 + d_in - SM_W, 0))],
        out_specs=[pl.BlockSpec((tm, SM_W), lambda i: (i, 0)),
                   pl.BlockSpec((SM_W, tm), lambda i: (0, i))],
        out_shape=[jax.ShapeDtypeStruct((m, SM_W), F32), jax.ShapeDtypeStruct((SM_W, m), F32)],
        scratch_shapes=[pltpu.VMEM((k, SM_W), BF16), pltpu.VMEM((SM_W, k), BF16)],
        compiler_params=pltpu.CompilerParams(
            dimension_semantics=("arbitrary",), vmem_limit_bytes=VMEM_LIMIT),
    )(x, w_in_t, w_in_t)


def _ssd_kernel(za_ref, sm_ref, cw_ref, cb_ref, dtb_ref, a_ref, dexp_ref, nw_ref, e_ref,
                o_ref, cbuf, state, ybuf):
    c = pl.program_id(1)
    L = CHUNK

    @pl.when(c == 0)
    def _():
        cbuf[0:8, :] = jnp.zeros((8, CONV_DIM), F32)
        state[...] = jnp.zeros_like(state)

    cbuf[8:8 + L, :] = za_ref[:, SSD_WIDTH:SSD_WIDTH + CONV_DIM]
    acc = cb_ref[...] + cw_ref[0:1, :] * cbuf[5:5 + L, :]
    acc = acc + cw_ref[1:2, :] * cbuf[6:6 + L, :]
    acc = acc + cw_ref[2:3, :] * cbuf[7:7 + L, :]
    acc = acc + cw_ref[3:4, :] * cbuf[8:8 + L, :]
    cbuf[0:8, :] = cbuf[L:L + 8, :]
    xbc = acc * _sigmoid(acc)
    xs = xbc[:, 0:SSD_WIDTH]
    bm = xbc[:, SSD_WIDTH:SSD_WIDTH + SSD_GROUPS * SSD_STATE]
    cm = xbc[:, SSD_WIDTH + SSD_GROUPS * SSD_STATE:CONV_DIM]

    dt = _softplus(sm_ref[...] + dtb_ref[...])
    da = dt * a_ref[...]
    row = lax.broadcasted_iota(I32, (L, L), 0)
    col = lax.broadcasted_iota(I32, (L, L), 1)
    causal = col <= row
    tri = jnp.where(causal, 1.0, 0.0).astype(F32)
    a_cs = jnp.dot(tri, da, preferred_element_type=F32, precision=lax.Precision.HIGHEST)
    a_cs_t = a_cs.T
    expa = jnp.exp(a_cs)
    dte = jnp.exp(a_cs[L - 1:L, :] - a_cs)
    stacked = jnp.concatenate([dt, expa, dte], axis=0)
    st_hi = stacked.astype(BF16)
    st_lo = (stacked - st_hi.astype(F32)).astype(BF16)
    expd = (jnp.dot(st_hi, e_ref[...], preferred_element_type=F32)
            + jnp.dot(st_lo, e_ref[...], preferred_element_type=F32))
    dt_x = expd[0:L, :]
    expa_x = expd[L:2 * L, :]
    dte_x = expd[2 * L:3 * L, :]
    xdt = xs * dt_x
    hpg = SSD_HEADS // SSD_GROUPS
    gw = hpg * SSD_HEADDIM
    for g in range(SSD_GROUPS):
        cm_g = cm[:, g * SSD_STATE:(g + 1) * SSD_STATE].astype(BF16)
        bm_gf = bm[:, g * SSD_STATE:(g + 1) * SSD_STATE]
        bm_g = bm_gf.astype(BF16)
        cb = _nt_dot(cm_g, bm_g)
        for hh in range(hpg):
            h = g * hpg + hh
            seg = a_cs[:, h:h + 1] - a_cs_t[h:h + 1, :]
            dec = jnp.exp(jnp.where(causal, seg, -jnp.inf))
            mh = (cb * dec).astype(BF16)
            xh = xdt[:, h * SSD_HEADDIM:(h + 1) * SSD_HEADDIM].astype(BF16)
            ybuf[:, h * SSD_HEADDIM:(h + 1) * SSD_HEADDIM] = jnp.dot(
                mh, xh, preferred_element_type=F32)
        st_g = state[:, g * gw:(g + 1) * gw]
        y_off = jnp.dot(cm_g, st_g.astype(BF16), preferred_element_type=F32)
        ybuf[:, g * gw:(g + 1) * gw] = (ybuf[:, g * gw:(g + 1) * gw]
                                        + y_off * expa_x[:, g * gw:(g + 1) * gw])
        xw = (xdt[:, g * gw:(g + 1) * gw] * dte_x[:, g * gw:(g + 1) * gw]).astype(BF16)
        upd = jnp.dot(bm_gf.T.astype(BF16), xw, preferred_element_type=F32)
        state[:, g * gw:(g + 1) * gw] = st_g * expa_x[L - 1:L, g * gw:(g + 1) * gw] + upd

    y = ybuf[...] + xs * dexp_ref[...]
    z = za_ref[:, 0:SSD_WIDTH]
    yg = y * (z * _sigmoid(z))
    ms = jnp.mean(yg * yg, axis=-1, keepdims=True)
    o_ref[...] = (yg * lax.rsqrt(ms + LN_EPS) * nw_ref[...]).astype(o_ref.dtype)


def _ssd(za, sm, conv_w, conv_b, dt_bias, a_log, d_skip, ssm_norm_w, bsz, seq):
    nc = seq // CHUNK
    t = bsz * seq
    pad = SM_W - SSD_HEADS
    dtb = jnp.pad(dt_bias.astype(F32), (0, pad)).reshape(1, SM_W)
    a = jnp.pad(-jnp.exp(a_log.astype(F32)), (0, pad)).reshape(1, SM_W)
    dexp = jnp.repeat(d_skip.astype(F32), SSD_HEADDIM).reshape(1, SSD_WIDTH)
    e = (jnp.arange(SSD_WIDTH)[None, :] // SSD_HEADDIM == jnp.arange(SM_W)[:, None]).astype(BF16)
    const = lambda shape: pl.BlockSpec(shape, lambda b, c: (0, 0))
    return pl.pallas_call(
        _ssd_kernel,
        name="ssd",
        grid=(bsz, nc),
        in_specs=[pl.BlockSpec((CHUNK, SSD_WIDTH + CONV_DIM), lambda b, c: (b * nc + c, 0)),
                  pl.BlockSpec((CHUNK, SM_W), lambda b, c: (b * nc + c, 0)),
                  const((CONV_WIDTH, CONV_DIM)), const((1, CONV_DIM)), const((1, SM_W)),
                  const((1, SM_W)), const((1, SSD_WIDTH)), const((1, SSD_WIDTH)),
                  const((SM_W, SSD_WIDTH))],
        out_specs=pl.BlockSpec((CHUNK, SSD_WIDTH), lambda b, c: (b * nc + c, 0)),
        out_shape=jax.ShapeDtypeStruct((t, SSD_WIDTH), BF16),
        scratch_shapes=[pltpu.VMEM((CHUNK + 8, CONV_DIM), F32),
                        pltpu.VMEM((SSD_STATE, SSD_WIDTH), F32),
                        pltpu.VMEM((CHUNK, SSD_WIDTH), F32)],
        compiler_params=pltpu.CompilerParams(
            dimension_semantics=("parallel", "arbitrary"), vmem_limit_bytes=VMEM_LIMIT),
    )(za, sm, conv_w.astype(F32), conv_b.astype(F32).reshape(1, CONV_DIM), dtb, a, dexp,
      ssm_norm_w.astype(F32).reshape(1, SSD_WIDTH), e)


def _dsa_kernel(qt_ref, iqt_ref, *rest, topk):
    ns = DSA_KV_SPLIT
    vt_refs, k_refs = rest[0:ns], rest[ns:2 * ns]
    smb_ref, smt_ref, g_ref, b_ref, o_ref, ikn, keys, acc_s, s_s, p_s, bias_s = rest[2 * ns:]
    slab = ATT_WIDTH // ns
    qi = pl.program_id(1)
    QB = o_ref.shape[0]
    KC = KEY_CHUNK
    SC = SEARCH_CHUNK

    @pl.when(qi == 0)
    def _():
        ik = smb_ref[:, SM_IK:SM_IK + IDX_DIM]
        ikn[...] = _layer_norm_rows(ik, g_ref[...], b_ref[...]).astype(BF16)

    nkc = (qi * QB + QB - 1) // KC + 1
    nsc = (qi * QB + QB - 1) // SC + 1
    w = smt_ref[SM_IW:SM_IW + IDX_HEADS, :] * (IDX_DIM ** -0.5 * IDX_HEADS ** -0.5)
    qpos = qi * QB + lax.broadcasted_iota(I32, (KC, QB), 1)
    kiota = lax.broadcasted_iota(I32, (KC, QB), 0)

    def score_chunk(c, carry):
        rmax, rmin = carry
        off = pl.multiple_of(c * KC, KC)
        kc = ikn[pl.ds(off, KC), :]
        acc = jnp.zeros((KC, QB), F32)
        for h in range(IDX_HEADS):
            d = jnp.dot(kc, iqt_ref[h * IDX_DIM:(h + 1) * IDX_DIM, :], preferred_element_type=F32)
            acc = acc + w[h:h + 1, :] * jnp.maximum(d, 0.0)
        vis = off + kiota <= qpos
        sv = jnp.where(vis, acc, -jnp.inf)
        for j in range(KC // SC):
            keys[c * (KC // SC) + j] = sv[j * SC:(j + 1) * SC, :]
        rmax = jnp.maximum(rmax, jnp.max(sv, axis=0, keepdims=True))
        rmin = jnp.minimum(rmin, jnp.min(jnp.where(vis, acc, jnp.inf), axis=0, keepdims=True))
        return rmax, rmin

    rmax, rmin = lax.fori_loop(
        0, nkc, score_chunk,
        (jnp.full((1, QB), -jnp.inf, F32), jnp.full((1, QB), jnp.inf, F32)))

    kf = float(topk)

    def count_ge(mid):
        def count_chunk(c, acc):
            m = jnp.where(keys[c] >= mid, 1.0, 0.0)
            return acc + jnp.sum(m.reshape(SC // 8, 8, QB), axis=0)

        acc = lax.fori_loop(0, nsc, count_chunk, jnp.zeros((8, QB), F32))
        return jnp.sum(acc, axis=0, keepdims=True)

    nvis = (qi * QB + lax.broadcasted_iota(I32, (1, QB), 1) + 1).astype(F32)
    small = nvis <= kf

    def search_cond(st):
        return (st[0] > 0.0) & (st[1] < SEARCH_MAX_ITERS)

    def search_body(st):
        _, it, lo, hi, clo = st
        for _ in range(SEARCH_PROBES_PER_CHECK):
            mid = 0.5 * lo + 0.5 * hi
            splittable = (mid > lo) & (mid < hi)
            c = count_ge(mid)
            up = splittable & (c >= kf)
            down = splittable & (c < kf)
            lo = jnp.where(up, mid, lo)
            clo = jnp.where(up, c, clo)
            hi = jnp.where(down, mid, hi)
        active = jnp.logical_not(small) & splittable & (clo != kf)
        return jnp.sum(jnp.where(active, 1.0, 0.0)), it + 1, lo, hi, clo

    hi0 = rmax + jnp.abs(rmax) * 1e-6 + 1e-37
    st = lax.while_loop(search_cond, search_body,
                        (jnp.float32(1.0), jnp.int32(0), rmin, hi0, nvis))
    thr = jnp.where(small, FLOAT_LOWEST, st[2])

    acc_s[...] = jnp.zeros_like(acc_s)

    def att_chunk(c, carry):
        ms, ls = carry
        off = pl.multiple_of(c * KC, KC)
        for j in range(KC // SC):
            bias_s[j * SC:(j + 1) * SC, :] = jnp.where(keys[c * (KC // SC) + j] >= thr, 0.0, -1e30)
        for h in range(ATT_HEADS):
            lo, hi = h * ATT_HEADDIM, (h + 1) * ATT_HEADDIM
            s_s[h] = jnp.dot(k_refs[lo // slab][pl.ds(off, KC), lo % slab:lo % slab + ATT_HEADDIM],
                             qt_ref[lo:hi, :],
                             preferred_element_type=F32)
        new_ms, new_ls, alphas = [], [], []
        for h in range(ATT_HEADS):
            s = s_s[h] + bias_s[...]
            m_new = jnp.maximum(ms[h], jnp.max(s, axis=0, keepdims=True))
            alpha = jnp.exp(ms[h] - m_new)
            p = jnp.exp(s - m_new)
            new_ls.append(alpha * ls[h] + jnp.sum(p, axis=0, keepdims=True))
            new_ms.append(m_new)
            alphas.append(alpha)
            p_s[h] = p.astype(BF16)
        for h in range(ATT_HEADS):
            lo, hi = h * ATT_HEADDIM, (h + 1) * ATT_HEADDIM
            pv = jnp.dot(vt_refs[lo // slab][lo % slab:lo % slab + ATT_HEADDIM, pl.ds(off, KC)],
                         p_s[h],
                         preferred_element_type=F32)
            acc_s[h] = alphas[h] * acc_s[h] + pv
        return tuple(new_ms), tuple(new_ls)

    m0 = tuple(jnp.full((1, QB), -1e30, F32) for _ in range(ATT_HEADS))
    l0 = tuple(jnp.zeros((1, QB), F32) for _ in range(ATT_HEADS))
    _, ls = lax.fori_loop(0, nkc, att_chunk, (m0, l0))
    for h in range(ATT_HEADS):
        o_ref[:, h * ATT_HEADDIM:(h + 1) * ATT_HEADDIM] = (acc_s[h] / ls[h]).T.astype(o_ref.dtype)


def _dsa(qt, iqt, vt, kk, sm, smt, g, b, bsz, seq):
    qb = min(DSA_QB, seq)
    nq = seq // qb
    t = bsz * seq
    topk = min(TOPK_MAX, seq // 4)
    single = pl.Buffered(1)
    ns = DSA_KV_SPLIT
    slab = ATT_WIDTH // ns
    vt_specs = [pl.BlockSpec((slab, seq), lambda bi, qi, q=q: (q, bi), pipeline_mode=single)
                for q in range(ns)]
    k_specs = [pl.BlockSpec((seq, slab), lambda bi, qi, q=q: (bi, q), pipeline_mode=single)
               for q in range(ns)]
    return pl.pallas_call(
        functools.partial(_dsa_kernel, topk=topk),
        name="dsa",
        grid=(bsz, nq),
        in_specs=[pl.BlockSpec((ATT_WIDTH, qb), lambda bi, qi: (0, bi * nq + qi)),
                  pl.BlockSpec((IDX_HEADS * IDX_DIM, qb), lambda bi, qi: (0, bi * nq + qi)),
                  *vt_specs, *k_specs,
                  pl.BlockSpec((seq, SM_W), lambda bi, qi: (bi, 0), pipeline_mode=single),
                  pl.BlockSpec((SM_W, qb), lambda bi, qi: (0, bi * nq + qi)),
                  pl.BlockSpec((1, IDX_DIM), lambda bi, qi: (0, 0)),
                  pl.BlockSpec((1, IDX_DIM), lambda bi, qi: (0, 0))],
        out_specs=pl.BlockSpec((qb, ATT_WIDTH), lambda bi, qi: (bi * nq + qi, 0)),
        out_shape=jax.ShapeDtypeStruct((t, ATT_WIDTH), BF16),
        scratch_shapes=[pltpu.VMEM((seq, IDX_DIM), BF16),
                        pltpu.VMEM((seq // SEARCH_CHUNK, SEARCH_CHUNK, qb), F32),
                        pltpu.VMEM((ATT_HEADS, ATT_HEADDIM, qb), F32),
                        pltpu.VMEM((ATT_HEADS, KEY_CHUNK, qb), F32),
                        pltpu.VMEM((ATT_HEADS, KEY_CHUNK, qb), BF16),
                        pltpu.VMEM((KEY_CHUNK, qb), F32)],
        compiler_params=pltpu.CompilerParams(
            dimension_semantics=("parallel", "arbitrary"), vmem_limit_bytes=VMEM_LIMIT),
    )(qt, iqt, *([vt] * ns), *([kk] * ns), sm, smt, g.astype(F32).reshape(1, IDX_DIM),
      b.astype(F32).reshape(1, IDX_DIM))


def _route_rows(logits_t, rb):
    aff = _sigmoid(logits_t)
    biased = aff + rb
    rb_rows = [biased[r:r + 1, :] for r in range(N_EXPERTS)]
    ra_rows = [aff[r:r + 1, :] for r in range(N_EXPERTS)]
    epg = EXPERTS_PER_GROUP
    gs = []
    for g in range(N_EXPERT_GROUPS):
        a, b, c, d = rb_rows[epg * g:epg * g + epg]
        p, q = jnp.maximum(a, b), jnp.minimum(a, b)
        r, s = jnp.maximum(c, d), jnp.minimum(c, d)
        gs.append(jnp.maximum(p, r) + jnp.maximum(jnp.minimum(p, r), jnp.maximum(q, s)))
    best = gs[0]
    bidx = jnp.zeros_like(best, dtype=I32)
    for g in range(1, N_EXPERT_GROUPS):
        better = gs[g] > best
        best = jnp.where(better, gs[g], best)
        bidx = jnp.where(better, g, bidx)
    vb = [rb_rows[j] for j in range(epg)]
    va = [ra_rows[j] for j in range(epg)]
    for g in range(1, N_EXPERT_GROUPS):
        pick = bidx == g
        vb = [jnp.where(pick, rb_rows[epg * g + j], vb[j]) for j in range(epg)]
        va = [jnp.where(pick, ra_rows[epg * g + j], va[j]) for j in range(epg)]
    t1, a1, i1 = vb[0], va[0], jnp.zeros_like(bidx)
    for j in range(1, epg):
        better = vb[j] > t1
        t1 = jnp.where(better, vb[j], t1)
        a1 = jnp.where(better, va[j], a1)
        i1 = jnp.where(better, j, i1)
    t2 = jnp.full_like(t1, -jnp.inf)
    a2 = jnp.zeros_like(a1)
    i2 = jnp.zeros_like(i1)
    for j in range(epg):
        vj = jnp.where(i1 == j, -jnp.inf, vb[j])
        better = vj > t2
        t2 = jnp.where(better, vj, t2)
        a2 = jnp.where(better, va[j], a2)
        i2 = jnp.where(better, j, i2)
    den = a1 + a2
    return bidx * epg + i1, bidx * epg + i2, a1 / den, a2 / den


def _outproj_kernel(ys_ref, ya_ref, w1_ref, w2_ref, h_ref, g_ref, b_ref, rwh_ref, rwl_ref, rb_ref,
                    h1_ref, h1b_ref, eidx_ref, gate_ref):
    mix = jnp.dot(ys_ref[...], w1_ref[...], preferred_element_type=F32)
    mix = mix + jnp.dot(ya_ref[...], w2_ref[...], preferred_element_type=F32)
    h1 = _layer_norm_rows(ALPHA * h_ref[...] + mix, g_ref[...], b_ref[...])
    h1_ref[...] = h1
    h1_hi = h1.astype(BF16)
    h1b_ref[...] = h1_hi
    h1_lo = (h1 - h1_hi.astype(F32)).astype(BF16)
    logits = jnp.dot(h1_hi, rwh_ref[...], preferred_element_type=F32)
    logits = logits + jnp.dot(h1_hi, rwl_ref[...], preferred_element_type=F32)
    logits = logits + jnp.dot(h1_lo, rwh_ref[...], preferred_element_type=F32)
    logits_t = logits.T[0:N_EXPERTS, :]
    e1, e2, g1, g2 = _route_rows(logits_t, rb_ref[...])
    eidx_ref[0:1, :] = e1
    eidx_ref[1:2, :] = e2
    gate_ref[0:1, :] = g1
    gate_ref[1:2, :] = g2


def _outproj_ln_router(y_ssd, y_att, w_out, h, ln_g, ln_b, router_w, router_b, tm):
    t, d = h.shape
    w = w_out.astype(BF16)
    rw = jnp.pad(router_w.astype(F32), ((0, 0), (0, V7X_LANES - N_EXPERTS)))
    rw_hi = rw.astype(BF16)
    rw_lo = (rw - rw_hi.astype(F32)).astype(BF16)
    const = lambda shape: pl.BlockSpec(shape, lambda i: (0, 0))
    return pl.pallas_call(
        _outproj_kernel,
        name="outproj_ln_router",
        grid=(t // tm,),
        in_specs=[pl.BlockSpec((tm, SSD_WIDTH), lambda i: (i, 0)),
                  pl.BlockSpec((tm, ATT_WIDTH), lambda i: (i, 0)),
                  pl.BlockSpec((SSD_WIDTH, d), lambda i: (0, 0)),
                  pl.BlockSpec((ATT_WIDTH, d), lambda i: (1, 0)),
                  pl.BlockSpec((tm, d), lambda i: (i, 0)),
                  const((1, d)), const((1, d)), const((d, V7X_LANES)), const((d, V7X_LANES)),
                  const((N_EXPERTS, 1))],
        out_specs=[pl.BlockSpec((tm, d), lambda i: (i, 0)),
                   pl.BlockSpec((tm, d), lambda i: (i, 0)),
                   pl.BlockSpec((2, tm), lambda i: (0, i)),
                   pl.BlockSpec((2, tm), lambda i: (0, i))],
        out_shape=[jax.ShapeDtypeStruct((t, d), F32), jax.ShapeDtypeStruct((t, d), BF16),
                   jax.ShapeDtypeStruct((2, t), I32), jax.ShapeDtypeStruct((2, t), F32)],
        compiler_params=pltpu.CompilerParams(
            dimension_semantics=("parallel",), vmem_limit_bytes=VMEM_LIMIT),
    )(y_ssd, y_att, w, w, h, ln_g.astype(F32).reshape(1, d), ln_b.astype(F32).reshape(1, d),
      rw_hi, rw_lo, router_b.astype(F32).reshape(N_EXPERTS, 1))


def _expert_kernel(be_ref, run_ref, nxt_ref, nv_ref, nu_ref, st_ref, h_hbm, wg_hbm, wu_hbm, wd_hbm,
                   o_ref, xbuf, wg_f, wu_f, wd_f, wg_s, wu_s, wd_s, xsem, wsem, *, layer):
    i = pl.program_id(0)
    nu = nu_ref[0]
    slot = lax.rem(i, 2)
    ns = MOE_W_SPLIT

    def row_copy(tok, r, s):
        return pltpu.make_async_copy(h_hbm.at[pl.ds(tok, 1), :], xbuf.at[s, pl.ds(r, 1), :],
                                     xsem.at[s])

    def row_groups(blk):
        return (nv_ref[blk] + GATHER_GROUP - 1) // GATHER_GROUP

    def issue_rows(blk, s):
        def body(gi, carry):
            for u in range(GATHER_GROUP):
                r = gi * GATHER_GROUP + u
                row_copy(st_ref[blk * MOE_BLOCK + r], r, s).start(priority=GATHER_DMA_PRIORITY)
            return carry
        lax.fori_loop(0, row_groups(blk), body, 0)

    def wait_rows(blk, s):
        def body(gi, carry):
            for u in range(GATHER_GROUP):
                row_copy(0, gi * GATHER_GROUP + u, s).wait()
            return carry
        lax.fori_loop(0, row_groups(blk), body, 0)

    def weight_copies(e, ws):
        copies = []
        for src, dst in ((wg_hbm, wg_f), (wu_hbm, wu_f), (wd_hbm, wd_f)):
            rows = dst.shape[1] // ns
            for q in range(ns):
                copies.append(pltpu.make_async_copy(
                    src.at[layer, e, pl.ds(q * rows, rows), :],
                    dst.at[ws, pl.ds(q * rows, rows), :], wsem.at[ws]))
        return copies

    @pl.when(i == 0)
    def _():
        xbuf[...] = jnp.zeros_like(xbuf)
        issue_rows(0, 0)
        for c in weight_copies(be_ref[0], 0):
            c.start()

    @pl.when(i < nu)
    def _():
        wait_rows(i, slot)

        @pl.when(i + 1 < nu)
        def _():
            issue_rows(i + 1, 1 - slot)

        ws = lax.rem(run_ref[i], 2)

        @pl.when((i == 0) | (be_ref[i] != be_ref[jnp.maximum(i - 1, 0)]))
        def _():
            for c in weight_copies(be_ref[i], ws):
                c.wait()
            wg_s[...] = wg_f[ws].astype(BF16)
            wu_s[...] = wu_f[ws].astype(BF16)
            wd_s[...] = wd_f[ws].astype(BF16)

            @pl.when(nxt_ref[i] >= 0)
            def _():
                for c in weight_copies(nxt_ref[i], 1 - ws):
                    c.start()

        x = xbuf[slot].astype(BF16)
        g = jnp.dot(x, wg_s[...], preferred_element_type=F32)
        u = jnp.dot(x, wu_s[...], preferred_element_type=F32)
        a = (g * _sigmoid(g) * u).astype(BF16)
        o_ref[...] = jnp.dot(a, wd_s[...], preferred_element_type=F32)

    @pl.when(i >= nu)
    def _():
        o_ref[...] = jnp.zeros_like(o_ref)


def _experts(plan, h1, w_gate, w_up, w_down, layer):
    t, d = h1.shape
    cap = plan["slot_tok"].shape[0]
    nb = cap // MOE_BLOCK
    hbm = pl.BlockSpec(memory_space=pl.ANY)
    return pl.pallas_call(
        functools.partial(_expert_kernel, layer=layer),
        name="moe_experts",
        grid_spec=pltpu.PrefetchScalarGridSpec(
            num_scalar_prefetch=6,
            grid=(nb,),
            in_specs=[hbm, hbm, hbm, hbm],
            out_specs=pl.BlockSpec((MOE_BLOCK, d), lambda i, *_: (i, 0)),
            scratch_shapes=[pltpu.VMEM((2, MOE_BLOCK, d), F32),
                            pltpu.VMEM((2, d, D_FF), F32), pltpu.VMEM((2, d, D_FF), F32),
                            pltpu.VMEM((2, D_FF, d), F32),
                            pltpu.VMEM((d, D_FF), BF16), pltpu.VMEM((d, D_FF), BF16),
                            pltpu.VMEM((D_FF, d), BF16),
                            pltpu.SemaphoreType.DMA((2,)), pltpu.SemaphoreType.DMA((2,))]),
        out_shape=jax.ShapeDtypeStruct((cap, d), F32),
        compiler_params=pltpu.CompilerParams(
            dimension_semantics=("arbitrary",), vmem_limit_bytes=VMEM_LIMIT),
    )(plan["block_expert"], plan["run"], plan["next_expert"], plan["valid_rows"], plan["n_used"],
      plan["slot_tok"], h1, w_gate, w_up, w_down)


def _moe_plan(eidx, t):
    flat_e = eidx.reshape(2 * t)
    onehot = (flat_e[:, None] == jnp.arange(N_EXPERTS, dtype=I32)[None, :]).astype(I32)
    cs = jnp.cumsum(onehot, axis=0)
    rank = jnp.sum(onehot * cs, axis=1) - 1
    counts = cs[-1]
    padded = (counts + MOE_BLOCK - 1) // MOE_BLOCK * MOE_BLOCK
    pad_end = jnp.cumsum(padded)
    pad_start = pad_end - padded
    dest = (pad_start[flat_e] + rank).astype(I32)
    cap = 2 * t + N_EXPERTS * MOE_BLOCK
    nb = cap // MOE_BLOCK
    block_start = jnp.arange(nb, dtype=I32) * MOE_BLOCK
    block_expert = jnp.minimum(
        jnp.sum((pad_end[None, :] <= block_start[:, None]).astype(I32), axis=1), N_EXPERTS - 1)
    n_used = (pad_end[-1] // MOE_BLOCK).astype(I32).reshape(1)
    tok = jnp.tile(jnp.arange(t, dtype=I32), 2)
    slot_tok = jnp.zeros((cap,), I32).at[dest].set(tok, unique_indices=True)
    blk = jnp.arange(nb, dtype=I32)
    used = blk < n_used[0]
    valid_rows = jnp.where(
        used, jnp.clip(counts[block_expert] - (block_start - pad_start[block_expert]), 0, MOE_BLOCK), 0)
    first = used & jnp.concatenate([jnp.ones((1,), bool), block_expert[1:] != block_expert[:-1]])
    run = jnp.cumsum(first.astype(I32)) - 1
    later = (block_expert[None, :] > block_expert[:, None]) & used[None, :]
    nxt = jnp.min(jnp.where(later, block_expert[None, :], N_EXPERTS), axis=1)
    next_expert = jnp.where(nxt < N_EXPERTS, nxt, -1)
    plan = dict(block_expert=block_expert, run=run.astype(I32), next_expert=next_expert.astype(I32),
                valid_rows=valid_rows.astype(I32), n_used=n_used, slot_tok=slot_tok)
    return dest, plan


def _ple_kernel(dest_ref, h1_ref, h1b_ref, yb_hbm, gt_ref, p_ref, wg_ref, wp_ref, g_ref, b_ref,
                o_ref, ob_ref, gbuf, sem):
    i = pl.program_id(0)
    n = pl.num_programs(0)
    tm = h1_ref.shape[0]
    t = n * tm
    slot = lax.rem(i, 2)
    nxt_base = jnp.minimum(i + 1, n - 1) * tm

    def row_copy(src_row, k, r, s):
        return pltpu.make_async_copy(yb_hbm.at[pl.ds(src_row, 1), :],
                                     gbuf.at[s, pl.ds(k * tm + r, 1), :], sem.at[s])

    def wait_tile(s):
        pltpu.make_async_copy(yb_hbm.at[pl.ds(0, 2 * tm), :], gbuf.at[s], sem.at[s]).wait()

    @pl.when(i == 0)
    def _():
        def body(r, carry):
            for k in range(2):
                row_copy(dest_ref[k * t + r], k, r, 0).start(priority=k)
            return carry
        lax.fori_loop(0, tm, body, 0)

    wait_tile(slot)
    for r in range(tm):
        for k in range(2):
            row_copy(dest_ref[k * t + nxt_base + r], k, r, 1 - slot).start(priority=k)
    ffn = gt_ref[:, 0:1] * gbuf[slot, 0:tm, :] + gt_ref[:, 1:2] * gbuf[slot, tm:2 * tm, :]
    gate = _sigmoid(jnp.dot(h1b_ref[...], wg_ref[...], preferred_element_type=F32))
    pe = jnp.dot(p_ref[...].astype(BF16), wp_ref[...], preferred_element_type=F32)
    u = ALPHA * h1_ref[...] + ffn + gate * pe
    h2 = _layer_norm_rows(u, g_ref[...], b_ref[...])
    o_ref[...] = h2
    ob_ref[...] = h2.astype(BF16)

    @pl.when(i == n - 1)
    def _():
        wait_tile(1 - slot)


def _ple_ln(dest, h1, h1b, yb, gates_t, p, ple_gate_w, ple_w, ln_g, ln_b, tm):
    t, d = h1.shape
    pd = p.shape[1]
    row = lambda w: pl.BlockSpec((tm, w), lambda i, dest: (i, 0))
    const = lambda shape: pl.BlockSpec(shape, lambda i, dest: (0, 0), pipeline_mode=pl.Buffered(1))
    return pl.pallas_call(
        _ple_kernel,
        name="ple_ln",
        grid_spec=pltpu.PrefetchScalarGridSpec(
            num_scalar_prefetch=1,
            grid=(t // tm,),
            in_specs=[row(d), row(d), pl.BlockSpec(memory_space=pl.ANY), row(2), row(pd),
                      const((d, d)), const((pd, d)), const((1, d)), const((1, d))],
            out_specs=[row(d), row(d)],
            scratch_shapes=[pltpu.VMEM((2, 2 * tm, d), F32), pltpu.SemaphoreType.DMA((2,))]),
        out_shape=[jax.ShapeDtypeStruct((t, d), F32), jax.ShapeDtypeStruct((t, d), BF16)],
        compiler_params=pltpu.CompilerParams(
            dimension_semantics=("arbitrary",), vmem_limit_bytes=VMEM_LIMIT),
    )(dest, h1, h1b, yb, gates_t, p, ple_gate_w.astype(BF16), ple_w.astype(BF16),
      ln_g.astype(F32).reshape(1, d), ln_b.astype(F32).reshape(1, d))


def _layer(h, hb, p_i, w_in, layer, conv_w, conv_b, dt_bias, a_log, d_skip, ssm_norm_w,
           idx_k_norm_g, idx_k_norm_b, w_out, ln1_g, ln1_b, router_w, router_b,
           w_gate, w_up, w_down, ple_w, ple_gate_w, ln2_g, ln2_b, bsz, seq):
    t = bsz * seq
    tm = min(1024, t)
    row0 = layer * (OFF_IW + IDX_HEADS)
    za = _proj(hb, w_in, row0, OFF_Z, OFF_DT - OFF_Z, F32, False, tm)
    kk = _proj(hb, w_in, row0, OFF_Q + ATT_WIDTH, ATT_WIDTH, BF16, False, tm)
    qt = _proj(hb, w_in, row0, OFF_Q, ATT_WIDTH, BF16, True, tm, out_scale=ATT_HEADDIM ** -0.5)
    vt = _proj(hb, w_in, row0, OFF_Q + 2 * ATT_WIDTH, ATT_WIDTH, BF16, True, tm)
    iqt = _proj(hb, w_in, row0, OFF_IQ, IDX_HEADS * IDX_DIM, BF16, True, tm)
    sm, smt = _proj_small(hb, w_in, row0, tm)

    y_ssd = _ssd(za, sm, conv_w, conv_b, dt_bias, a_log, d_skip, ssm_norm_w, bsz, seq)
    y_att = _dsa(qt, iqt, vt, kk, sm, smt, idx_k_norm_g, idx_k_norm_b, bsz, seq)
    h1, h1b, eidx, gates = _outproj_ln_router(
        y_ssd, y_att, w_out, h, ln1_g, ln1_b, router_w, router_b, min(256, t))

    dest, plan = _moe_plan(eidx, t)
    yb = _experts(plan, h1, w_gate, w_up, w_down, layer)
    return _ple_ln(dest, h1, h1b, yb, gates.T, p_i, ple_gate_w, ple_w, ln2_g, ln2_b, min(256, t))


def kernel(x, p, w_in, conv_w, conv_b, dt_bias, a_log, d_skip, ssm_norm_w, idx_k_norm_g,
           idx_k_norm_b, w_out, ln1_g, ln1_b, router_w, router_b, w_gate, w_up, w_down,
           ple_w, ple_gate_w, ln2_g, ln2_b):
    bsz, seq, d = x.shape
    t = bsz * seq
    h = x.reshape(t, d).astype(F32)
    hb = h.astype(BF16)
    w_in_t = jnp.swapaxes(w_in, 1, 2).reshape(-1, d)
    for i in range(w_in.shape[0]):
        h, hb = _layer(h, hb, p[i].reshape(t, -1), w_in_t, i, conv_w[i], conv_b[i], dt_bias[i],
                       a_log[i], d_skip[i], ssm_norm_w[i], idx_k_norm_g[i], idx_k_norm_b[i],
                       w_out[i], ln1_g[i], ln1_b[i], router_w, router_b, w_gate, w_up,
                       w_down, ple_w[i], ple_gate_w[i], ln2_g[i], ln2_b[i], bsz, seq)
    return h.reshape(bsz, seq, d).astype(x.dtype)
```

```python
import functools

import jax
import jax.numpy as jnp
from jax import lax
from jax.experimental import pallas as pl
from jax.experimental.pallas import tpu as pltpu

F32 = jnp.float32
BF16 = jnp.bfloat16
I32 = jnp.int32

SSD_WIDTH = 1024
SSD_HEADDIM = 64
SSD_HEADS = 16
SSD_STATE = 128
SSD_GROUPS = 2
CONV_WIDTH = 4
CHUNK = 128
CONV_DIM = SSD_WIDTH + 2 * SSD_GROUPS * SSD_STATE
ATT_HEADS = 8
ATT_HEADDIM = 128
ATT_WIDTH = ATT_HEADS * ATT_HEADDIM
IDX_HEADS = 16
IDX_DIM = 64
TOPK_MAX = 256
Q_BLOCK = 128
N_EXPERTS = 32
N_EXPERT_GROUPS = 8
EXPERTS_PER_GROUP = 4
D_FF = 512
ALPHA = (2.0 * 2) ** 0.25
LN_EPS = 1e-5

OFF_Z = 0
OFF_XBC = OFF_Z + SSD_WIDTH
OFF_DT = OFF_XBC + CONV_DIM
OFF_Q = OFF_DT + SSD_HEADS
OFF_IQ = OFF_Q + 3 * ATT_WIDTH
OFF_IK = OFF_IQ + IDX_HEADS * IDX_DIM
OFF_IW = OFF_IK + IDX_DIM
SM_DT = 0
SM_IK = SSD_HEADS
SM_IW = SM_IK + IDX_DIM
SM_W = 128

V7X_LANES = 128
VMEM_LIMIT = 56 * 1024 * 1024
FLOAT_LOWEST = -3.0e38
SEARCH_PROBES_PER_CHECK = 3
SEARCH_MAX_ITERS = 200
KEY_CHUNK = 512
SEARCH_CHUNK = 256
DSA_QB = 512
DSA_KV_SPLIT = 4
MOE_BLOCK = 256
MOE_W_SPLIT = 4
GATHER_DMA_PRIORITY = 1
GATHER_GROUP = 8


def _sigmoid(x):
    return 1.0 / (1.0 + jnp.exp(-x))


def _softplus(x):
    return jnp.maximum(x, 0.0) + jnp.log(1.0 + jnp.exp(-jnp.abs(x)))


def _nt_dot(a, b, **kw):
    return lax.dot_general(a, b, (((1,), (1,)), ((), ())), preferred_element_type=F32, **kw)


def _layer_norm_rows(u, g, b):
    mu = jnp.mean(u, axis=-1, keepdims=True)
    d = u - mu
    var = jnp.mean(d * d, axis=-1, keepdims=True)
    return d * lax.rsqrt(var + LN_EPS) * g + b


PROJ_ROWS = 256


def _proj_kernel(x_ref, wt_ref, o_ref, w_s, *, transposed, out_scale):
    tn, k = wt_ref.shape

    @pl.when(pl.program_id(1) == 0)
    def _():
        if transposed:
            w_s[...] = wt_ref[...].astype(BF16)
        else:
            def stage(r, carry):
                r0 = pl.multiple_of(r * PROJ_ROWS, PROJ_ROWS)
                w_s[pl.ds(r0, PROJ_ROWS), :] = wt_ref[:, pl.ds(r0, PROJ_ROWS)].T.astype(BF16)
                return carry

            lax.fori_loop(0, k // PROJ_ROWS, stage, 0)

    if transposed:
        y = _nt_dot(w_s[...], x_ref[...])
    else:
        y = jnp.dot(x_ref[...], w_s[...], preferred_element_type=F32)
    if out_scale != 1.0:
        y = y * out_scale
    o_ref[...] = y.astype(o_ref.dtype)


def _proj(x, w_in_t, row0, c0, width, out_dtype, transposed, tm, tn=512, out_scale=1.0):
    m, k = x.shape
    assert (row0 + c0) % 8 == 0 and width % tn == 0 and m % tm == 0
    in_specs = [pl.BlockSpec((tm, k), lambda j, i: (i, 0)),
                pl.BlockSpec((pl.Element(tn), pl.Element(k)),
                             lambda j, i: (pl.multiple_of(row0 + c0 + j * tn, 8), 0))]
    if transposed:
        out_spec = pl.BlockSpec((tn, tm), lambda j, i: (j, i))
        out_shape = jax.ShapeDtypeStruct((width, m), out_dtype)
        scratch = pltpu.VMEM((tn, k), BF16)
    else:
        out_spec = pl.BlockSpec((tm, tn), lambda j, i: (i, j))
        out_shape = jax.ShapeDtypeStruct((m, width), out_dtype)
        scratch = pltpu.VMEM((k, tn), BF16)
    return pl.pallas_call(
        functools.partial(_proj_kernel, transposed=transposed, out_scale=out_scale),
        name="in_proj_t" if transposed else "in_proj",
        grid=(width // tn, m // tm),
        in_specs=in_specs,
        out_specs=out_spec,
        out_shape=out_shape,
        scratch_shapes=[scratch],
        compiler_params=pltpu.CompilerParams(
            dimension_semantics=("parallel", "arbitrary"), vmem_limit_bytes=VMEM_LIMIT),
    )(x, w_in_t)


def _proj_small_kernel(x_ref, wa_ref, wb_ref, o_ref, ot_ref, w_s, wt_s):
    @pl.when(pl.program_id(0) == 0)
    def _():
        k = wa_ref.shape[1]
        tail = SM_W - IDX_DIM - IDX_HEADS
        wt = jnp.concatenate(
            [wa_ref[0:SSD_HEADS, :], wb_ref[tail:SM_W, :],
             jnp.zeros((SM_W - SSD_HEADS - IDX_DIM - IDX_HEADS, k), F32)], axis=0)
        wt_s[...] = wt.astype(BF16)
        w_s[...] = wt.T.astype(BF16)

    o_ref[...] = jnp.dot(x_ref[...], w_s[...], preferred_element_type=F32)
    ot_ref[...] = _nt_dot(wt_s[...], x_ref[...])


def _proj_small(x, w_in_t, row0, tm):
    m, k = x.shape
    d_in = OFF_IW + IDX_HEADS
    assert OFF_IW - OFF_IK == IDX_DIM and (row0 + OFF_DT) % 8 == 0 and (row0 + d_in) % 8 == 0
    return pl.pallas_call(
        _proj_small_kernel,
        name="in_proj_small",
        grid=(m // tm,),
        in_specs=[pl.BlockSpec((tm, k), lambda i: (i, 0)),
                  pl.BlockSpec((pl.Element(SM_W), pl.Element(k)), lambda i: (row0 + OFF_DT, 0)),
                  pl.BlockSpec((pl.Element(SM_W), pl.Element(k)),
                               lambda i: (row0 + d_in - SM_W, 0))],
        out_specs=[pl.BlockSpec((tm, SM_W), lambda i: (i, 0)),
                   pl.BlockSpec((SM_W, tm), lambda i: (0, i))],
        out_shape=[jax.ShapeDtypeStruct((m, SM_W), F32), jax.ShapeDtypeStruct((SM_W, m), F32)],
        scratch_shapes=[pltpu.VMEM((k, SM_W), BF16), pltpu.VMEM((SM_W, k), BF16)],
        compiler_params=pltpu.CompilerParams(
            dimension_semantics=("arbitrary",), vmem_limit_bytes=VMEM_LIMIT),
    )(x, w_in_t, w_in_t)


def _ssd_kernel(za_ref, sm_ref, cw_ref, cb_ref, dtb_ref, a_ref, dexp_ref, nw_ref, e_ref,
                o_ref, cbuf, state, ybuf):
    c = pl.program_id(1)
    L = CHUNK

    @pl.when(c == 0)
    def _():
        cbuf[0:8, :] = jnp.zeros((8, CONV_DIM), F32)
        state[...] = jnp.zeros_like(state)

    cbuf[8:8 + L, :] = za_ref[:, SSD_WIDTH:SSD_WIDTH + CONV_DIM]
    acc = cb_ref[...] + cw_ref[0:1, :] * cbuf[5:5 + L, :]
    acc = acc + cw_ref[1:2, :] * cbuf[6:6 + L, :]
    acc = acc + cw_ref[2:3, :] * cbuf[7:7 + L, :]
    acc = acc + cw_ref[3:4, :] * cbuf[8:8 + L, :]
    cbuf[0:8, :] = cbuf[L:L + 8, :]
    xbc = acc * _sigmoid(acc)
    xs = xbc[:, 0:SSD_WIDTH]
    bm = xbc[:, SSD_WIDTH:SSD_WIDTH + SSD_GROUPS * SSD_STATE]
    cm = xbc[:, SSD_WIDTH + SSD_GROUPS * SSD_STATE:CONV_DIM]

    dt = _softplus(sm_ref[...] + dtb_ref[...])
    da = dt * a_ref[...]
    row = lax.broadcasted_iota(I32, (L, L), 0)
    col = lax.broadcasted_iota(I32, (L, L), 1)
    causal = col <= row
    tri = jnp.where(causal, 1.0, 0.0).astype(F32)
    a_cs = jnp.dot(tri, da, preferred_element_type=F32, precision=lax.Precision.HIGHEST)
    a_cs_t = a_cs.T
    expa = jnp.exp(a_cs)
    dte = jnp.exp(a_cs[L - 1:L, :] - a_cs)
    stacked = jnp.concatenate([dt, expa, dte], axis=0)
    st_hi = stacked.astype(BF16)
    st_lo = (stacked - st_hi.astype(F32)).astype(BF16)
    expd = (jnp.dot(st_hi, e_ref[...], preferred_element_type=F32)
            + jnp.dot(st_lo, e_ref[...], preferred_element_type=F32))
    dt_x = expd[0:L, :]
    expa_x = expd[L:2 * L, :]
    dte_x = expd[2 * L:3 * L, :]
    xdt = xs * dt_x
    hpg = SSD_HEADS // SSD_GROUPS
    gw = hpg * SSD_HEADDIM
    for g in range(SSD_GROUPS):
        cm_g = cm[:, g * SSD_STATE:(g + 1) * SSD_STATE].astype(BF16)
        bm_gf = bm[:, g * SSD_STATE:(g + 1) * SSD_STATE]
        bm_g = bm_gf.astype(BF16)
        cb = _nt_dot(cm_g, bm_g)
        for hh in range(hpg):
            h = g * hpg + hh
            seg = a_cs[:, h:h + 1] - a_cs_t[h:h + 1, :]
            dec = jnp.exp(jnp.where(causal, seg, -jnp.inf))
            mh = (cb * dec).astype(BF16)
            xh = xdt[:, h * SSD_HEADDIM:(h + 1) * SSD_HEADDIM].astype(BF16)
            ybuf[:, h * SSD_HEADDIM:(h + 1) * SSD_HEADDIM] = jnp.dot(
                mh, xh, preferred_element_type=F32)
        st_g = state[:, g * gw:(g + 1) * gw]
        y_off = jnp.dot(cm_g, st_g.astype(BF16), preferred_element_type=F32)
        ybuf[:, g * gw:(g + 1) * gw] = (ybuf[:, g * gw:(g + 1) * gw]
                                        + y_off * expa_x[:, g * gw:(g + 1) * gw])
        xw = (xdt[:, g * gw:(g + 1) * gw] * dte_x[:, g * gw:(g + 1) * gw]).astype(BF16)
        upd = jnp.dot(bm_gf.T.astype(BF16), xw, preferred_element_type=F32)
        state[:, g * gw:(g + 1) * gw] = st_g * expa_x[L - 1:L, g * gw:(g + 1) * gw] + upd

    y = ybuf[...] + xs * dexp_ref[...]
    z = za_ref[:, 0:SSD_WIDTH]
    yg = y * (z * _sigmoid(z))
    ms = jnp.mean(yg * yg, axis=-1, keepdims=True)
    o_ref[...] = (yg * lax.rsqrt(ms + LN_EPS) * nw_ref[...]).astype(o_ref.dtype)


def _ssd(za, sm, conv_w, conv_b, dt_bias, a_log, d_skip, ssm_norm_w, bsz, seq):
    nc = seq // CHUNK
    t = bsz * seq
    pad = SM_W - SSD_HEADS
    dtb = jnp.pad(dt_bias.astype(F32), (0, pad)).reshape(1, SM_W)
    a = jnp.pad(-jnp.exp(a_log.astype(F32)), (0, pad)).reshape(1, SM_W)
    dexp = jnp.repeat(d_skip.astype(F32), SSD_HEADDIM).reshape(1, SSD_WIDTH)
    e = (jnp.arange(SSD_WIDTH)[None, :] // SSD_HEADDIM == jnp.arange(SM_W)[:, None]).astype(BF16)
    const = lambda shape: pl.BlockSpec(shape, lambda b, c: (0, 0))
    return pl.pallas_call(
        _ssd_kernel,
        name="ssd",
        grid=(bsz, nc),
        in_specs=[pl.BlockSpec((CHUNK, SSD_WIDTH + CONV_DIM), lambda b, c: (b * nc + c, 0)),
                  pl.BlockSpec((CHUNK, SM_W), lambda b, c: (b * nc + c, 0)),
                  const((CONV_WIDTH, CONV_DIM)), const((1, CONV_DIM)), const((1, SM_W)),
                  const((1, SM_W)), const((1, SSD_WIDTH)), const((1, SSD_WIDTH)),
                  const((SM_W, SSD_WIDTH))],
        out_specs=pl.BlockSpec((CHUNK, SSD_WIDTH), lambda b, c: (b * nc + c, 0)),
        out_shape=jax.ShapeDtypeStruct((t, SSD_WIDTH), BF16),
        scratch_shapes=[pltpu.VMEM((CHUNK + 8, CONV_DIM), F32),
                        pltpu.VMEM((SSD_STATE, SSD_WIDTH), F32),
                        pltpu.VMEM((CHUNK, SSD_WIDTH), F32)],
        compiler_params=pltpu.CompilerParams(
            dimension_semantics=("parallel", "arbitrary"), vmem_limit_bytes=VMEM_LIMIT),
    )(za, sm, conv_w.astype(F32), conv_b.astype(F32).reshape(1, CONV_DIM), dtb, a, dexp,
      ssm_norm_w.astype(F32).reshape(1, SSD_WIDTH), e)


def _dsa_kernel(qt_ref, iqt_ref, *rest, topk):
    ns = DSA_KV_SPLIT
    vt_refs, k_refs = rest[0:ns], rest[ns:2 * ns]
    smb_ref, smt_ref, g_ref, b_ref, o_ref, ikn, keys, acc_s, s_s, p_s, bias_s = rest[2 * ns:]
    slab = ATT_WIDTH // ns
    qi = pl.program_id(1)
    QB = o_ref.shape[0]
    KC = KEY_CHUNK
    SC = SEARCH_CHUNK

    @pl.when(qi == 0)
    def _():
        ik = smb_ref[:, SM_IK:SM_IK + IDX_DIM]
        ikn[...] = _layer_norm_rows(ik, g_ref[...], b_ref[...]).astype(BF16)

    nkc = (qi * QB + QB - 1) // KC + 1
    nsc = (qi * QB + QB - 1) // SC + 1
    w = smt_ref[SM_IW:SM_IW + IDX_HEADS, :] * (IDX_DIM ** -0.5 * IDX_HEADS ** -0.5)
    qpos = qi * QB + lax.broadcasted_iota(I32, (KC, QB), 1)
    kiota = lax.broadcasted_iota(I32, (KC, QB), 0)

    def score_chunk(c, carry):
        rmax, rmin = carry
        off = pl.multiple_of(c * KC, KC)
        kc = ikn[pl.ds(off, KC), :]
        acc = jnp.zeros((KC, QB), F32)
        for h in range(IDX_HEADS):
            d = jnp.dot(kc, iqt_ref[h * IDX_DIM:(h + 1) * IDX_DIM, :], preferred_element_type=F32)
            acc = acc + w[h:h + 1, :] * jnp.maximum(d, 0.0)
        vis = off + kiota <= qpos
        sv = jnp.where(vis, acc, -jnp.inf)
        for j in range(KC // SC):
            keys[c * (KC // SC) + j] = sv[j * SC:(j + 1) * SC, :]
        rmax = jnp.maximum(rmax, jnp.max(sv, axis=0, keepdims=True))
        rmin = jnp.minimum(rmin, jnp.min(jnp.where(vis, acc, jnp.inf), axis=0, keepdims=True))
        return rmax, rmin

    rmax, rmin = lax.fori_loop(
        0, nkc, score_chunk,
        (jnp.full((1, QB), -jnp.inf, F32), jnp.full((1, QB), jnp.inf, F32)))

    kf = float(topk)

    def count_ge(mid):
        def count_chunk(c, acc):
            m = jnp.where(keys[c] >= mid, 1.0, 0.0)
            return acc + jnp.sum(m.reshape(SC // 8, 8, QB), axis=0)

        acc = lax.fori_loop(0, nsc, count_chunk, jnp.zeros((8, QB), F32))
        return jnp.sum(acc, axis=0, keepdims=True)

    nvis = (qi * QB + lax.broadcasted_iota(I32, (1, QB), 1) + 1).astype(F32)
    small = nvis <= kf

    def search_cond(st):
        return (st[0] > 0.0) & (st[1] < SEARCH_MAX_ITERS)

    def search_body(st):
        _, it, lo, hi, clo = st
        for _ in range(SEARCH_PROBES_PER_CHECK):
            mid = 0.5 * lo + 0.5 * hi
            splittable = (mid > lo) & (mid < hi)
            c = count_ge(mid)
            up = splittable & (c >= kf)
            down = splittable & (c < kf)
            lo = jnp.where(up, mid, lo)
            clo = jnp.where(up, c, clo)
            hi = jnp.where(down, mid, hi)
        active = jnp.logical_not(small) & splittable & (clo != kf)
        return jnp.sum(jnp.where(active, 1.0, 0.0)), it + 1, lo, hi, clo

    hi0 = rmax + jnp.abs(rmax) * 1e-6 + 1e-37
    st = lax.while_loop(search_cond, search_body,
                        (jnp.float32(1.0), jnp.int32(0), rmin, hi0, nvis))
    thr = jnp.where(small, FLOAT_LOWEST, st[2])

    acc_s[...] = jnp.zeros_like(acc_s)

    def att_chunk(c, carry):
        ms, ls = carry
        off = pl.multiple_of(c * KC, KC)
        for j in range(KC // SC):
            bias_s[j * SC:(j + 1) * SC, :] = jnp.where(keys[c * (KC // SC) + j] >= thr, 0.0, -1e30)
        for h in range(ATT_HEADS):
            lo, hi = h * ATT_HEADDIM, (h + 1) * ATT_HEADDIM
            s_s[h] = jnp.dot(k_refs[lo // slab][pl.ds(off, KC), lo % slab:lo % slab + ATT_HEADDIM],
                             qt_ref[lo:hi, :],
                             preferred_element_type=F32)
        new_ms, new_ls, alphas = [], [], []
        for h in range(ATT_HEADS):
            s = s_s[h] + bias_s[...]
            m_new = jnp.maximum(ms[h], jnp.max(s, axis=0, keepdims=True))
            alpha = jnp.exp(ms[h] - m_new)
            p = jnp.exp(s - m_new)
            new_ls.append(alpha * ls[h] + jnp.sum(p, axis=0, keepdims=True))
            new_ms.append(m_new)
            alphas.append(alpha)
            p_s[h] = p.astype(BF16)
        for h in range(ATT_HEADS):
            lo, hi = h * ATT_HEADDIM, (h + 1) * ATT_HEADDIM
            pv = jnp.dot(vt_refs[lo // slab][lo % slab:lo % slab + ATT_HEADDIM, pl.ds(off, KC)],
                         p_s[h],
                         preferred_element_type=F32)
            acc_s[h] = alphas[h] * acc_s[h] + pv
        return tuple(new_ms), tuple(new_ls)

    m0 = tuple(jnp.full((1, QB), -1e30, F32) for _ in range(ATT_HEADS))
    l0 = tuple(jnp.zeros((1, QB), F32) for _ in range(ATT_HEADS))
    _, ls = lax.fori_loop(0, nkc, att_chunk, (m0, l0))
    for h in range(ATT_HEADS):
        o_ref[:, h * ATT_HEADDIM:(h + 1) * ATT_HEADDIM] = (acc_s[h] / ls[h]).T.astype(o_ref.dtype)


def _dsa(qt, iqt, vt, kk, sm, smt, g, b, bsz, seq):
    qb = min(DSA_QB, seq)
    nq = seq // qb
    t = bsz * seq
    topk = min(TOPK_MAX, seq // 4)
    single = pl.Buffered(1)
    ns = DSA_KV_SPLIT
    slab = ATT_WIDTH // ns
    vt_specs = [pl.BlockSpec((slab, seq), lambda bi, qi, q=q: (q, bi), pipeline_mode=single)
                for q in range(ns)]
    k_specs = [pl.BlockSpec((seq, slab), lambda bi, qi, q=q: (bi, q), pipeline_mode=single)
               for q in range(ns)]
    return pl.pallas_call(
        functools.partial(_dsa_kernel, topk=topk),
        name="dsa",
        grid=(bsz, nq),
        in_specs=[pl.BlockSpec((ATT_WIDTH, qb), lambda bi, qi: (0, bi * nq + qi)),
                  pl.BlockSpec((IDX_HEADS * IDX_DIM, qb), lambda bi, qi: (0, bi * nq + qi)),
                  *vt_specs, *k_specs,
                  pl.BlockSpec((seq, SM_W), lambda bi, qi: (bi, 0), pipeline_mode=single),
                  pl.BlockSpec((SM_W, qb), lambda bi, qi: (0, bi * nq + qi)),
                  pl.BlockSpec((1, IDX_DIM), lambda bi, qi: (0, 0)),
                  pl.BlockSpec((1, IDX_DIM), lambda bi, qi: (0, 0))],
        out_specs=pl.BlockSpec((qb, ATT_WIDTH), lambda bi, qi: (bi * nq + qi, 0)),
        out_shape=jax.ShapeDtypeStruct((t, ATT_WIDTH), BF16),
        scratch_shapes=[pltpu.VMEM((seq, IDX_DIM), BF16),
                        pltpu.VMEM((seq // SEARCH_CHUNK, SEARCH_CHUNK, qb), F32),
                        pltpu.VMEM((ATT_HEADS, ATT_HEADDIM, qb), F32),
                        pltpu.VMEM((ATT_HEADS, KEY_CHUNK, qb), F32),
                        pltpu.VMEM((ATT_HEADS, KEY_CHUNK, qb), BF16),
                        pltpu.VMEM((KEY_CHUNK, qb), F32)],
        compiler_params=pltpu.CompilerParams(
            dimension_semantics=("parallel", "arbitrary"), vmem_limit_bytes=VMEM_LIMIT),
    )(qt, iqt, *([vt] * ns), *([kk] * ns), sm, smt, g.astype(F32).reshape(1, IDX_DIM),
      b.astype(F32).reshape(1, IDX_DIM))


def _route_rows(logits_t, rb):
    aff = _sigmoid(logits_t)
    biased = aff + rb
    rb_rows = [biased[r:r + 1, :] for r in range(N_EXPERTS)]
    ra_rows = [aff[r:r + 1, :] for r in range(N_EXPERTS)]
    epg = EXPERTS_PER_GROUP
    gs = []
    for g in range(N_EXPERT_GROUPS):
        a, b, c, d = rb_rows[epg * g:epg * g + epg]
        p, q = jnp.maximum(a, b), jnp.minimum(a, b)
        r, s = jnp.maximum(c, d), jnp.minimum(c, d)
        gs.append(jnp.maximum(p, r) + jnp.maximum(jnp.minimum(p, r), jnp.maximum(q, s)))
    best = gs[0]
    bidx = jnp.zeros_like(best, dtype=I32)
    for g in range(1, N_EXPERT_GROUPS):
        better = gs[g] > best
        best = jnp.where(better, gs[g], best)
        bidx = jnp.where(better, g, bidx)
    vb = [rb_rows[j] for j in range(epg)]
    va = [ra_rows[j] for j in range(epg)]
    for g in range(1, N_EXPERT_GROUPS):
        pick = bidx == g
        vb = [jnp.where(pick, rb_rows[epg * g + j], vb[j]) for j in range(epg)]
        va = [jnp.where(pick, ra_rows[epg * g + j], va[j]) for j in range(epg)]
    t1, a1, i1 = vb[0], va[0], jnp.zeros_like(bidx)
    for j in range(1, epg):
        better = vb[j] > t1
        t1 = jnp.where(better, vb[j], t1)
        a1 = jnp.where(better, va[j], a1)
        i1 = jnp.where(better, j, i1)
    t2 = jnp.full_like(t1, -jnp.inf)
    a2 = jnp.zeros_like(a1)
    i2 = jnp.zeros_like(i1)
    for j in range(epg):
        vj = jnp.where(i1 == j, -jnp.inf, vb[j])
        better = vj > t2
        t2 = jnp.where(better, vj, t2)
        a2 = jnp.where(better, va[j], a2)
        i2 = jnp.where(better, j, i2)
    den = a1 + a2
    return bidx * epg + i1, bidx * epg + i2, a1 / den, a2 / den


def _outproj_kernel(ys_ref, ya_ref, w1_ref, w2_ref, h_ref, g_ref, b_ref, rwh_ref, rwl_ref, rb_ref,
                    h1_ref, h1b_ref, eidx_ref, gate_ref):
    mix = jnp.dot(ys_ref[...], w1_ref[...], preferred_element_type=F32)
    mix = mix + jnp.dot(ya_ref[...], w2_ref[...], preferred_element_type=F32)
    h1 = _layer_norm_rows(ALPHA * h_ref[...] + mix, g_ref[...], b_ref[...])
    h1_ref[...] = h1
    h1_hi = h1.astype(BF16)
    h1b_ref[...] = h1_hi
    h1_lo = (h1 - h1_hi.astype(F32)).astype(BF16)
    logits = jnp.dot(h1_hi, rwh_ref[...], preferred_element_type=F32)
    logits = logits + jnp.dot(h1_hi, rwl_ref[...], preferred_element_type=F32)
    logits = logits + jnp.dot(h1_lo, rwh_ref[...], preferred_element_type=F32)
    logits_t = logits.T[0:N_EXPERTS, :]
    e1, e2, g1, g2 = _route_rows(logits_t, rb_ref[...])
    eidx_ref[0:1, :] = e1
    eidx_ref[1:2, :] = e2
    gate_ref[0:1, :] = g1
    gate_ref[1:2, :] = g2


def _outproj_ln_router(y_ssd, y_att, w_out, h, ln_g, ln_b, router_w, router_b, tm):
    t, d = h.shape
    w = w_out.astype(BF16)
    rw = jnp.pad(router_w.astype(F32), ((0, 0), (0, V7X_LANES - N_EXPERTS)))
    rw_hi = rw.astype(BF16)
    rw_lo = (rw - rw_hi.astype(F32)).astype(BF16)
    const = lambda shape: pl.BlockSpec(shape, lambda i: (0, 0))
    return pl.pallas_call(
        _outproj_kernel,
        name="outproj_ln_router",
        grid=(t // tm,),
        in_specs=[pl.BlockSpec((tm, SSD_WIDTH), lambda i: (i, 0)),
                  pl.BlockSpec((tm, ATT_WIDTH), lambda i: (i, 0)),
                  pl.BlockSpec((SSD_WIDTH, d), lambda i: (0, 0)),
                  pl.BlockSpec((ATT_WIDTH, d), lambda i: (1, 0)),
                  pl.BlockSpec((tm, d), lambda i: (i, 0)),
                  const((1, d)), const((1, d)), const((d, V7X_LANES)), const((d, V7X_LANES)),
                  const((N_EXPERTS, 1))],
        out_specs=[pl.BlockSpec((tm, d), lambda i: (i, 0)),
                   pl.BlockSpec((tm, d), lambda i: (i, 0)),
                   pl.BlockSpec((2, tm), lambda i: (0, i)),
                   pl.BlockSpec((2, tm), lambda i: (0, i))],
        out_shape=[jax.ShapeDtypeStruct((t, d), F32), jax.ShapeDtypeStruct((t, d), BF16),
                   jax.ShapeDtypeStruct((2, t), I32), jax.ShapeDtypeStruct((2, t), F32)],
        compiler_params=pltpu.CompilerParams(
            dimension_semantics=("parallel",), vmem_limit_bytes=VMEM_LIMIT),
    )(y_ssd, y_att, w, w, h, ln_g.astype(F32).reshape(1, d), ln_b.astype(F32).reshape(1, d),
      rw_hi, rw_lo, router_b.astype(F32).reshape(N_EXPERTS, 1))


def _expert_kernel(be_ref, run_ref, nxt_ref, nv_ref, nu_ref, st_ref, h_hbm, wg_hbm, wu_hbm, wd_hbm,
                   o_ref, xbuf, wg_f, wu_f, wd_f, wg_s, wu_s, wd_s, xsem, wsem, *, layer):
    i = pl.program_id(0)
    nu = nu_ref[0]
    slot = lax.rem(i, 2)
    ns = MOE_W_SPLIT

    def row_copy(tok, r, s):
        return pltpu.make_async_copy(h_hbm.at[pl.ds(tok, 1), :], xbuf.at[s, pl.ds(r, 1), :],
                                     xsem.at[s])

    def row_groups(blk):
        return (nv_ref[blk] + GATHER_GROUP - 1) // GATHER_GROUP

    def issue_rows(blk, s):
        def body(gi, carry):
            for u in range(GATHER_GROUP):
                r = gi * GATHER_GROUP + u
                row_copy(st_ref[blk * MOE_BLOCK + r], r, s).start(priority=GATHER_DMA_PRIORITY)
            return carry
        lax.fori_loop(0, row_groups(blk), body, 0)

    def wait_rows(blk, s):
        def body(gi, carry):
            for u in range(GATHER_GROUP):
                row_copy(0, gi * GATHER_GROUP + u, s).wait()
            return carry
        lax.fori_loop(0, row_groups(blk), body, 0)

    def weight_copies(e, ws):
        copies = []
        for src, dst in ((wg_hbm, wg_f), (wu_hbm, wu_f), (wd_hbm, wd_f)):
            rows = dst.shape[1] // ns
            for q in range(ns):
                copies.append(pltpu.make_async_copy(
                    src.at[layer, e, pl.ds(q * rows, rows), :],
                    dst.at[ws, pl.ds(q * rows, rows), :], wsem.at[ws]))
        return copies

    @pl.when(i == 0)
    def _():
        xbuf[...] = jnp.zeros_like(xbuf)
        issue_rows(0, 0)
        for c in weight_copies(be_ref[0], 0):
            c.start()

    @pl.when(i < nu)
    def _():
        wait_rows(i, slot)

        @pl.when(i + 1 < nu)
        def _():
            issue_rows(i + 1, 1 - slot)

        ws = lax.rem(run_ref[i], 2)

        @pl.when((i == 0) | (be_ref[i] != be_ref[jnp.maximum(i - 1, 0)]))
        def _():
            for c in weight_copies(be_ref[i], ws):
                c.wait()
            wg_s[...] = wg_f[ws].astype(BF16)
            wu_s[...] = wu_f[ws].astype(BF16)
            wd_s[...] = wd_f[ws].astype(BF16)

            @pl.when(nxt_ref[i] >= 0)
            def _():
                for c in weight_copies(nxt_ref[i], 1 - ws):
                    c.start()

        x = xbuf[slot].astype(BF16)
        g = jnp.dot(x, wg_s[...], preferred_element_type=F32)
        u = jnp.dot(x, wu_s[...], preferred_element_type=F32)
        a = (g * _sigmoid(g) * u).astype(BF16)
        o_ref[...] = jnp.dot(a, wd_s[...], preferred_element_type=F32)

    @pl.when(i >= nu)
    def _():
        o_ref[...] = jnp.zeros_like(o_ref)


def _experts(plan, h1, w_gate, w_up, w_down, layer):
    t, d = h1.shape
    cap = plan["slot_tok"].shape[0]
    nb = cap // MOE_BLOCK
    hbm = pl.BlockSpec(memory_space=pl.ANY)
    return pl.pallas_call(
        functools.partial(_expert_kernel, layer=layer),
        name="moe_experts",
        grid_spec=pltpu.PrefetchScalarGridSpec(
            num_scalar_prefetch=6,
            grid=(nb,),
            in_specs=[hbm, hbm, hbm, hbm],
            out_specs=pl.BlockSpec((MOE_BLOCK, d), lambda i, *_: (i, 0)),
            scratch_shapes=[pltpu.VMEM((2, MOE_BLOCK, d), F32),
                            pltpu.VMEM((2, d, D_FF), F32), pltpu.VMEM((2, d, D_FF), F32),
                            pltpu.VMEM((2, D_FF, d), F32),
                            pltpu.VMEM((d, D_FF), BF16), pltpu.VMEM((d, D_FF), BF16),
                            pltpu.VMEM((D_FF, d), BF16),
                            pltpu.SemaphoreType.DMA((2,)), pltpu.SemaphoreType.DMA((2,))]),
        out_shape=jax.ShapeDtypeStruct((cap, d), F32),
        compiler_params=pltpu.CompilerParams(
            dimension_semantics=("arbitrary",), vmem_limit_bytes=VMEM_LIMIT),
    )(plan["block_expert"], plan["run"], plan["next_expert"], plan["valid_rows"], plan["n_used"],
      plan["slot_tok"], h1, w_gate, w_up, w_down)


def _moe_plan(eidx, t):
    flat_e = eidx.reshape(2 * t)
    onehot = (flat_e[:, None] == jnp.arange(N_EXPERTS, dtype=I32)[None, :]).astype(I32)
    cs = jnp.cumsum(onehot, axis=0)
    rank = jnp.sum(onehot * cs, axis=1) - 1
    counts = cs[-1]
    padded = (counts + MOE_BLOCK - 1) // MOE_BLOCK * MOE_BLOCK
    pad_end = jnp.cumsum(padded)
    pad_start = pad_end - padded
    dest = (pad_start[flat_e] + rank).astype(I32)
    cap = 2 * t + N_EXPERTS * MOE_BLOCK
    nb = cap // MOE_BLOCK
    block_start = jnp.arange(nb, dtype=I32) * MOE_BLOCK
    block_expert = jnp.minimum(
        jnp.sum((pad_end[None, :] <= block_start[:, None]).astype(I32), axis=1), N_EXPERTS - 1)
    n_used = (pad_end[-1] // MOE_BLOCK).astype(I32).reshape(1)
    tok = jnp.tile(jnp.arange(t, dtype=I32), 2)
    slot_tok = jnp.zeros((cap,), I32).at[dest].set(tok, unique_indices=True)
    blk = jnp.arange(nb, dtype=I32)
    used = blk < n_used[0]
    valid_rows = jnp.where(
        used, jnp.clip(counts[block_expert] - (block_start - pad_start[block_expert]), 0, MOE_BLOCK), 0)
    first = used & jnp.concatenate([jnp.ones((1,), bool), block_expert[1:] != block_expert[:-1]])
    run = jnp.cumsum(first.astype(I32)) - 1
    later = (block_expert[None, :] > block_expert[:, None]) & used[None, :]
    nxt = jnp.min(jnp.where(later, block_expert[None, :], N_EXPERTS), axis=1)
    next_expert = jnp.where(nxt < N_EXPERTS, nxt, -1)
    plan = dict(block_expert=block_expert, run=run.astype(I32), next_expert=next_expert.astype(I32),
                valid_rows=valid_rows.astype(I32), n_used=n_used, slot_tok=slot_tok)
    return dest, plan


def _ple_kernel(dest_ref, h1_ref, h1b_ref, yb_hbm, gt_ref, p_ref, wg_ref, wp_ref, g_ref, b_ref,
                o_ref, ob_ref, gbuf, sem):
    i = pl.program_id(0)
    n = pl.num_programs(0)
    tm = h1_ref.shape[0]
    t = n * tm
    slot = lax.rem(i, 2)
    nxt_base = jnp.minimum(i + 1, n - 1) * tm

    def row_copy(src_row, k, r, s):
        return pltpu.make_async_copy(yb_hbm.at[pl.ds(src_row, 1), :],
                                     gbuf.at[s, pl.ds(k * tm + r, 1), :], sem.at[s])

    def wait_tile(s):
        pltpu.make_async_copy(yb_hbm.at[pl.ds(0, 2 * tm), :], gbuf.at[s], sem.at[s]).wait()

    @pl.when(i == 0)
    def _():
        def body(r, carry):
            for k in range(2):
                row_copy(dest_ref[k * t + r], k, r, 0).start(priority=k)
            return carry
        lax.fori_loop(0, tm, body, 0)

    wait_tile(slot)
    for r in range(tm):
        for k in range(2):
            row_copy(dest_ref[k * t + nxt_base + r], k, r, 1 - slot).start(priority=k)
    ffn = gt_ref[:, 0:1] * gbuf[slot, 0:tm, :] + gt_ref[:, 1:2] * gbuf[slot, tm:2 * tm, :]
    gate = _sigmoid(jnp.dot(h1b_ref[...], wg_ref[...], preferred_element_type=F32))
    pe = jnp.dot(p_ref[...].astype(BF16), wp_ref[...], preferred_element_type=F32)
    u = ALPHA * h1_ref[...] + ffn + gate * pe
    h2 = _layer_norm_rows(u, g_ref[...], b_ref[...])
    o_ref[...] = h2
    ob_ref[...] = h2.astype(BF16)

    @pl.when(i == n - 1)
    def _():
        wait_tile(1 - slot)


def _ple_ln(dest, h1, h1b, yb, gates_t, p, ple_gate_w, ple_w, ln_g, ln_b, tm):
    t, d = h1.shape
    pd = p.shape[1]
    row = lambda w: pl.BlockSpec((tm, w), lambda i, dest: (i, 0))
    const = lambda shape: pl.BlockSpec(shape, lambda i, dest: (0, 0), pipeline_mode=pl.Buffered(1))
    return pl.pallas_call(
        _ple_kernel,
        name="ple_ln",
        grid_spec=pltpu.PrefetchScalarGridSpec(
            num_scalar_prefetch=1,
            grid=(t // tm,),
            in_specs=[row(d), row(d), pl.BlockSpec(memory_space=pl.ANY), row(2), row(pd),
                      const((d, d)), const((pd, d)), const((1, d)), const((1, d))],
            out_specs=[row(d), row(d)],
            scratch_shapes=[pltpu.VMEM((2, 2 * tm, d), F32), pltpu.SemaphoreType.DMA((2,))]),
        out_shape=[jax.ShapeDtypeStruct((t, d), F32), jax.ShapeDtypeStruct((t, d), BF16)],
        compiler_params=pltpu.CompilerParams(
            dimension_semantics=("arbitrary",), vmem_limit_bytes=VMEM_LIMIT),
    )(dest, h1, h1b, yb, gates_t, p, ple_gate_w.astype(BF16), ple_w.astype(BF16),
      ln_g.astype(F32).reshape(1, d), ln_b.astype(F32).reshape(1, d))


def _layer(h, hb, p_i, w_in, layer, conv_w, conv_b, dt_bias, a_log, d_skip, ssm_norm_w,
           idx_k_norm_g, idx_k_norm_b, w_out, ln1_g, ln1_b, router_w, router_b,
           w_gate, w_up, w_down, ple_w, ple_gate_w, ln2_g, ln2_b, bsz, seq):
    t = bsz * seq
    tm = min(1024, t)
    row0 = layer * (OFF_IW + IDX_HEADS)
    za = _proj(hb, w_in, row0, OFF_Z, OFF_DT - OFF_Z, F32, False, tm)
    wide = ATT_WIDTH
    kk = _proj(hb, w_in, row0, OFF_Q + ATT_WIDTH, ATT_WIDTH, BF16, False, tm, tn=wide)
    qt = _proj(hb, w_in, row0, OFF_Q, ATT_WIDTH, BF16, True, tm, tn=wide,
               out_scale=ATT_HEADDIM ** -0.5)
    vt = _proj(hb, w_in, row0, OFF_Q + 2 * ATT_WIDTH, ATT_WIDTH, BF16, True, tm, tn=wide)
    iqt = _proj(hb, w_in, row0, OFF_IQ, IDX_HEADS * IDX_DIM, BF16, True, tm, tn=wide)
    sm, smt = _proj_small(hb, w_in, row0, tm)

    y_ssd = _ssd(za, sm, conv_w, conv_b, dt_bias, a_log, d_skip, ssm_norm_w, bsz, seq)
    y_att = _dsa(qt, iqt, vt, kk, sm, smt, idx_k_norm_g, idx_k_norm_b, bsz, seq)
    h1, h1b, eidx, gates = _outproj_ln_router(
        y_ssd, y_att, w_out, h, ln1_g, ln1_b, router_w, router_b, min(256, t))

    dest, plan = _moe_plan(eidx, t)
    yb = _experts(plan, h1, w_gate, w_up, w_down, layer)
    return _ple_ln(dest, h1, h1b, yb, gates.T, p_i, ple_gate_w, ple_w, ln2_g, ln2_b, min(256, t))


def kernel(x, p, w_in, conv_w, conv_b, dt_bias, a_log, d_skip, ssm_norm_w, idx_k_norm_g,
           idx_k_norm_b, w_out, ln1_g, ln1_b, router_w, router_b, w_gate, w_up, w_down,
           ple_w, ple_gate_w, ln2_g, ln2_b):
    bsz, seq, d = x.shape
    t = bsz * seq
    h = x.reshape(t, d).astype(F32)
    hb = h.astype(BF16)
    w_in_t = jnp.swapaxes(w_in, 1, 2).reshape(-1, d)
    for i in range(w_in.shape[0]):
        h, hb = _layer(h, hb, p[i].reshape(t, -1), w_in_t, i, conv_w[i], conv_b[i], dt_bias[i],
                       a_log[i], d_skip[i], ssm_norm_w[i], idx_k_norm_g[i], idx_k_norm_b[i],
                       w_out[i], ln1_g[i], ln1_b[i], router_w, router_b, w_gate, w_up,
                       w_down, ple_w[i], ple_gate_w[i], ln2_g[i], ln2_b[i], bsz, seq)
    return h.reshape(bsz, seq, d).astype(x.dtype)
```

```python
import functools

import jax
import jax.numpy as jnp
from jax import lax
from jax.experimental import pallas as pl
from jax.experimental.pallas import tpu as pltpu

F32 = jnp.float32
BF16 = jnp.bfloat16
I32 = jnp.int32

SSD_WIDTH = 1024
SSD_HEADDIM = 64
SSD_HEADS = 16
SSD_STATE = 128
SSD_GROUPS = 2
CONV_WIDTH = 4
CHUNK = 128
CONV_DIM = SSD_WIDTH + 2 * SSD_GROUPS * SSD_STATE
ATT_HEADS = 8
ATT_HEADDIM = 128
ATT_WIDTH = ATT_HEADS * ATT_HEADDIM
IDX_HEADS = 16
IDX_DIM = 64
TOPK_MAX = 256
Q_BLOCK = 128
N_EXPERTS = 32
N_EXPERT_GROUPS = 8
EXPERTS_PER_GROUP = 4
D_FF = 512
ALPHA = (2.0 * 2) ** 0.25
LN_EPS = 1e-5

OFF_Z = 0
OFF_XBC = OFF_Z + SSD_WIDTH
OFF_DT = OFF_XBC + CONV_DIM
OFF_Q = OFF_DT + SSD_HEADS
OFF_IQ = OFF_Q + 3 * ATT_WIDTH
OFF_IK = OFF_IQ + IDX_HEADS * IDX_DIM
OFF_IW = OFF_IK + IDX_DIM
SM_DT = 0
SM_IK = SSD_HEADS
SM_IW = SM_IK + IDX_DIM
SM_W = 128

V7X_LANES = 128
VMEM_LIMIT = 56 * 1024 * 1024
FLOAT_LOWEST = -3.0e38
SEARCH_PROBES_PER_CHECK = 3
SEARCH_MAX_ITERS = 200
KEY_CHUNK = 512
SEARCH_CHUNK = 256
DSA_QB = 512
DSA_KV_SPLIT = 4
MOE_BLOCK = 256
MOE_W_SPLIT = 4
GATHER_DMA_PRIORITY = 1
GATHER_GROUP = 16
PROJ_TOKENS = 1024
ROW_TOKENS = 256


def _sigmoid(x):
    return 1.0 / (1.0 + jnp.exp(-x))


def _softplus(x):
    return jnp.maximum(x, 0.0) + jnp.log(1.0 + jnp.exp(-jnp.abs(x)))


def _nt_dot(a, b, **kw):
    return lax.dot_general(a, b, (((1,), (1,)), ((), ())), preferred_element_type=F32, **kw)


def _layer_norm_rows(u, g, b):
    mu = jnp.mean(u, axis=-1, keepdims=True)
    d = u - mu
    var = jnp.mean(d * d, axis=-1, keepdims=True)
    return d * lax.rsqrt(var + LN_EPS) * g + b


PROJ_ROWS = 256


def _proj_kernel(x_ref, wt_ref, o_ref, w_s, *, transposed, out_scale):
    tn, k = wt_ref.shape

    @pl.when(pl.program_id(1) == 0)
    def _():
        if transposed:
            w_s[...] = wt_ref[...].astype(BF16)
        else:
            def stage(r, carry):
                r0 = pl.multiple_of(r * PROJ_ROWS, PROJ_ROWS)
                w_s[pl.ds(r0, PROJ_ROWS), :] = wt_ref[:, pl.ds(r0, PROJ_ROWS)].T.astype(BF16)
                return carry

            lax.fori_loop(0, k // PROJ_ROWS, stage, 0)

    if transposed:
        y = _nt_dot(w_s[...], x_ref[...])
    else:
        y = jnp.dot(x_ref[...], w_s[...], preferred_element_type=F32)
    if out_scale != 1.0:
        y = y * out_scale
    o_ref[...] = y.astype(o_ref.dtype)


def _proj(x, w_in_t, row0, c0, width, out_dtype, transposed, tm, tn=512, out_scale=1.0):
    m, k = x.shape
    assert (row0 + c0) % 8 == 0 and width % tn == 0 and m % tm == 0
    in_specs = [pl.BlockSpec((tm, k), lambda j, i: (i, 0)),
                pl.BlockSpec((pl.Element(tn), pl.Element(k)),
                             lambda j, i: (pl.multiple_of(row0 + c0 + j * tn, 8), 0))]
    if transposed:
        out_spec = pl.BlockSpec((tn, tm), lambda j, i: (j, i))
        out_shape = jax.ShapeDtypeStruct((width, m), out_dtype)
        scratch = pltpu.VMEM((tn, k), BF16)
    else:
        out_spec = pl.BlockSpec((tm, tn), lambda j, i: (i, j))
        out_shape = jax.ShapeDtypeStruct((m, width), out_dtype)
        scratch = pltpu.VMEM((k, tn), BF16)
    return pl.pallas_call(
        functools.partial(_proj_kernel, transposed=transposed, out_scale=out_scale),
        name="in_proj_t" if transposed else "in_proj",
        grid=(width // tn, m // tm),
        in_specs=in_specs,
        out_specs=out_spec,
        out_shape=out_shape,
        scratch_shapes=[scratch],
        compiler_params=pltpu.CompilerParams(
            dimension_semantics=("parallel", "arbitrary"), vmem_limit_bytes=VMEM_LIMIT),
    )(x, w_in_t)


def _proj_small_kernel(x_ref, wa_ref, wb_ref, o_ref, ot_ref, w_s, wt_s):
    @pl.when(pl.program_id(0) == 0)
    def _():
        k = wa_ref.shape[1]
        tail = SM_W - IDX_DIM - IDX_HEADS
        wt = jnp.concatenate(
            [wa_ref[0:SSD_HEADS, :], wb_ref[tail:SM_W, :],
             jnp.zeros((SM_W - SSD_HEADS - IDX_DIM - IDX_HEADS, k), F32)], axis=0)
        wt_s[...] = wt.astype(BF16)
        w_s[...] = wt.T.astype(BF16)

    o_ref[...] = jnp.dot(x_ref[...], w_s[...], preferred_element_type=F32)
    ot_ref[...] = _nt_dot(wt_s[...], x_ref[...])


def _proj_small(x, w_in_t, row0, tm):
    m, k = x.shape
    d_in = OFF_IW + IDX_HEADS
    assert OFF_IW - OFF_IK == IDX_DIM and (row0 + OFF_DT) % 8 == 0 and (row0 + d_in) % 8 == 0
    return pl.pallas_call(
        _proj_small_kernel,
        name="in_proj_small",
        grid=(m // tm,),
        in_specs=[pl.BlockSpec((tm, k), lambda i: (i, 0)),
                  pl.BlockSpec((pl.Element(SM_W), pl.Element(k)), lambda i: (row0 + OFF_DT, 0)),
                  pl.BlockSpec((pl.Element(SM_W), pl.Element(k)),
                               lambda i: (row0 + d_in - SM_W, 0))],
        out_specs=[pl.BlockSpec((tm, SM_W), lambda i: (i, 0)),
                   pl.BlockSpec((SM_W, tm), lambda i: (0, i))],
        out_shape=[jax.ShapeDtypeStruct((m, SM_W), F32), jax.ShapeDtypeStruct((SM_W, m), F32)],
        scratch_shapes=[pltpu.VMEM((k, SM_W), BF16), pltpu.VMEM((SM_W, k), BF16)],
        compiler_params=pltpu.CompilerParams(
            dimension_semantics=("arbitrary",), vmem_limit_bytes=VMEM_LIMIT),
    )(x, w_in_t, w_in_t)


def _ssd_kernel(za_ref, sm_ref, cw_ref, cb_ref, dtb_ref, a_ref, dexp_ref, nw_ref, e_ref,
                o_ref, cbuf, state, ybuf):
    c = pl.program_id(1)
    L = CHUNK

    @pl.when(c == 0)
    def _():
        cbuf[0:8, :] = jnp.zeros((8, CONV_DIM), F32)
        state[...] = jnp.zeros_like(state)

    cbuf[8:8 + L, :] = za_ref[:, SSD_WIDTH:SSD_WIDTH + CONV_DIM]
    acc = cb_ref[...] + cw_ref[0:1, :] * cbuf[5:5 + L, :]
    acc = acc + cw_ref[1:2, :] * cbuf[6:6 + L, :]
    acc = acc + cw_ref[2:3, :] * cbuf[7:7 + L, :]
    acc = acc + cw_ref[3:4, :] * cbuf[8:8 + L, :]
    cbuf[0:8, :] = cbuf[L:L + 8, :]
    xbc = acc * _sigmoid(acc)
    xs = xbc[:, 0:SSD_WIDTH]
    bm = xbc[:, SSD_WIDTH:SSD_WIDTH + SSD_GROUPS * SSD_STATE]
    cm = xbc[:, SSD_WIDTH + SSD_GROUPS * SSD_STATE:CONV_DIM]

    dt = _softplus(sm_ref[...] + dtb_ref[...])
    da = dt * a_ref[...]
    row = lax.broadcasted_iota(I32, (L, L), 0)
    col = lax.broadcasted_iota(I32, (L, L), 1)
    causal = col <= row
    tri = jnp.where(causal, 1.0, 0.0).astype(F32)
    a_cs = jnp.dot(tri, da, preferred_element_type=F32, precision=lax.Precision.HIGHEST)
    a_cs_t = a_cs.T
    expa = jnp.exp(a_cs)
    dte = jnp.exp(a_cs[L - 1:L, :] - a_cs)
    stacked = jnp.concatenate([dt, expa, dte], axis=0)
    st_hi = stacked.astype(BF16)
    st_lo = (stacked - st_hi.astype(F32)).astype(BF16)
    expd = (jnp.dot(st_hi, e_ref[...], preferred_element_type=F32)
            + jnp.dot(st_lo, e_ref[...], preferred_element_type=F32))
    dt_x = expd[0:L, :]
    expa_x = expd[L:2 * L, :]
    dte_x = expd[2 * L:3 * L, :]
    xdt = xs * dt_x
    hpg = SSD_HEADS // SSD_GROUPS
    gw = hpg * SSD_HEADDIM
    for g in range(SSD_GROUPS):
        cm_g = cm[:, g * SSD_STATE:(g + 1) * SSD_STATE].astype(BF16)
        bm_gf = bm[:, g * SSD_STATE:(g + 1) * SSD_STATE]
        bm_g = bm_gf.astype(BF16)
        cb = _nt_dot(cm_g, bm_g)
        for hh in range(hpg):
            h = g * hpg + hh
            seg = a_cs[:, h:h + 1] - a_cs_t[h:h + 1, :]
            dec = jnp.exp(jnp.where(causal, seg, -jnp.inf))
            mh = (cb * dec).astype(BF16)
            xh = xdt[:, h * SSD_HEADDIM:(h + 1) * SSD_HEADDIM].astype(BF16)
            ybuf[:, h * SSD_HEADDIM:(h + 1) * SSD_HEADDIM] = jnp.dot(
                mh, xh, preferred_element_type=F32)
        st_g = state[:, g * gw:(g + 1) * gw]
        y_off = jnp.dot(cm_g, st_g.astype(BF16), preferred_element_type=F32)
        ybuf[:, g * gw:(g + 1) * gw] = (ybuf[:, g * gw:(g + 1) * gw]
                                        + y_off * expa_x[:, g * gw:(g + 1) * gw])
        xw = (xdt[:, g * gw:(g + 1) * gw] * dte_x[:, g * gw:(g + 1) * gw]).astype(BF16)
        upd = jnp.dot(bm_gf.T.astype(BF16), xw, preferred_element_type=F32)
        state[:, g * gw:(g + 1) * gw] = st_g * expa_x[L - 1:L, g * gw:(g + 1) * gw] + upd

    y = ybuf[...] + xs * dexp_ref[...]
    z = za_ref[:, 0:SSD_WIDTH]
    yg = y * (z * _sigmoid(z))
    ms = jnp.mean(yg * yg, axis=-1, keepdims=True)
    o_ref[...] = (yg * lax.rsqrt(ms + LN_EPS) * nw_ref[...]).astype(o_ref.dtype)


def _ssd(za, sm, conv_w, conv_b, dt_bias, a_log, d_skip, ssm_norm_w, bsz, seq):
    nc = seq // CHUNK
    t = bsz * seq
    pad = SM_W - SSD_HEADS
    dtb = jnp.pad(dt_bias.astype(F32), (0, pad)).reshape(1, SM_W)
    a = jnp.pad(-jnp.exp(a_log.astype(F32)), (0, pad)).reshape(1, SM_W)
    dexp = jnp.repeat(d_skip.astype(F32), SSD_HEADDIM).reshape(1, SSD_WIDTH)
    e = (jnp.arange(SSD_WIDTH)[None, :] // SSD_HEADDIM == jnp.arange(SM_W)[:, None]).astype(BF16)
    const = lambda shape: pl.BlockSpec(shape, lambda b, c: (0, 0))
    return pl.pallas_call(
        _ssd_kernel,
        name="ssd",
        grid=(bsz, nc),
        in_specs=[pl.BlockSpec((CHUNK, SSD_WIDTH + CONV_DIM), lambda b, c: (b * nc + c, 0)),
                  pl.BlockSpec((CHUNK, SM_W), lambda b, c: (b * nc + c, 0)),
                  const((CONV_WIDTH, CONV_DIM)), const((1, CONV_DIM)), const((1, SM_W)),
                  const((1, SM_W)), const((1, SSD_WIDTH)), const((1, SSD_WIDTH)),
                  const((SM_W, SSD_WIDTH))],
        out_specs=pl.BlockSpec((CHUNK, SSD_WIDTH), lambda b, c: (b * nc + c, 0)),
        out_shape=jax.ShapeDtypeStruct((t, SSD_WIDTH), BF16),
        scratch_shapes=[pltpu.VMEM((CHUNK + 8, CONV_DIM), F32),
                        pltpu.VMEM((SSD_STATE, SSD_WIDTH), F32),
                        pltpu.VMEM((CHUNK, SSD_WIDTH), F32)],
        compiler_params=pltpu.CompilerParams(
            dimension_semantics=("parallel", "arbitrary"), vmem_limit_bytes=VMEM_LIMIT),
    )(za, sm, conv_w.astype(F32), conv_b.astype(F32).reshape(1, CONV_DIM), dtb, a, dexp,
      ssm_norm_w.astype(F32).reshape(1, SSD_WIDTH), e)


def _dsa_kernel(qt_ref, iqt_ref, *rest, topk):
    ns = DSA_KV_SPLIT
    vt_refs, k_refs = rest[0:ns], rest[ns:2 * ns]
    smb_ref, smt_ref, g_ref, b_ref, o_ref, ikn, keys, acc_s, s_s, p_s, bias_s, tie_s = rest[2 * ns:]
    slab = ATT_WIDTH // ns
    qi = pl.program_id(1)
    QB = o_ref.shape[0]
    KC = KEY_CHUNK
    SC = SEARCH_CHUNK

    @pl.when(qi == 0)
    def _():
        ik = smb_ref[:, SM_IK:SM_IK + IDX_DIM]
        ikn[...] = _layer_norm_rows(ik, g_ref[...], b_ref[...]).astype(BF16)

    nkc = (qi * QB + QB - 1) // KC + 1
    nsc = (qi * QB + QB - 1) // SC + 1
    w = smt_ref[SM_IW:SM_IW + IDX_HEADS, :] * (IDX_DIM ** -0.5 * IDX_HEADS ** -0.5)
    qpos = qi * QB + lax.broadcasted_iota(I32, (KC, QB), 1)
    kiota = lax.broadcasted_iota(I32, (KC, QB), 0)

    def score_chunk(c, carry):
        rmax, rmin = carry
        off = pl.multiple_of(c * KC, KC)
        kc = ikn[pl.ds(off, KC), :]
        acc = jnp.zeros((KC, QB), F32)
        for h in range(IDX_HEADS):
            d = jnp.dot(kc, iqt_ref[h * IDX_DIM:(h + 1) * IDX_DIM, :], preferred_element_type=F32)
            acc = acc + w[h:h + 1, :] * jnp.maximum(d, 0.0)
        vis = off + kiota <= qpos
        sv = jnp.where(vis, acc, -jnp.inf)
        for j in range(KC // SC):
            keys[c * (KC // SC) + j] = sv[j * SC:(j + 1) * SC, :]
        rmax = jnp.maximum(rmax, jnp.max(sv, axis=0, keepdims=True))
        rmin = jnp.minimum(rmin, jnp.min(jnp.where(vis, acc, jnp.inf), axis=0, keepdims=True))
        return rmax, rmin

    rmax, rmin = lax.fori_loop(
        0, nkc, score_chunk,
        (jnp.full((1, QB), -jnp.inf, F32), jnp.full((1, QB), jnp.inf, F32)))

    kf = float(topk)

    def count_ge(mid):
        def count_chunk(c, acc):
            m = jnp.where(keys[c] >= mid, 1.0, 0.0)
            return acc + jnp.sum(m.reshape(SC // 8, 8, QB), axis=0)

        acc = lax.fori_loop(0, nsc, count_chunk, jnp.zeros((8, QB), F32))
        return jnp.sum(acc, axis=0, keepdims=True)

    nvis = (qi * QB + lax.broadcasted_iota(I32, (1, QB), 1) + 1).astype(F32)
    small = nvis <= kf

    def search_cond(st):
        return (st[0] > 0.0) & (st[1] < SEARCH_MAX_ITERS)

    def search_body(st):
        _, it, lo, hi, clo = st
        for _ in range(SEARCH_PROBES_PER_CHECK):
            mid = 0.5 * lo + 0.5 * hi
            splittable = (mid > lo) & (mid < hi)
            c = count_ge(mid)
            up = splittable & (c >= kf)
            down = splittable & (c < kf)
            lo = jnp.where(up, mid, lo)
            clo = jnp.where(up, c, clo)
            hi = jnp.where(down, mid, hi)
        active = jnp.logical_not(small) & splittable & (clo != kf)
        return jnp.sum(jnp.where(active, 1.0, 0.0)), it + 1, lo, hi, clo

    hi0 = rmax + jnp.abs(rmax) * 1e-6 + 1e-37
    st = lax.while_loop(search_cond, search_body,
                        (jnp.float32(1.0), jnp.int32(0), rmin, hi0, nvis))
    thr = jnp.where(small, FLOAT_LOWEST, st[2])

    seq = ikn.shape[0]
    tied = jnp.logical_not(small) & (st[4] > kf)
    tie_s[...] = jnp.full(tie_s.shape, float(seq), F32)
    sub_iota = lax.broadcasted_iota(I32, (SC, QB), 0)

    @pl.when(jnp.sum(jnp.where(tied, 1.0, 0.0)) > 0.0)
    def _():
        def count_where(indicator):
            def count_chunk(c, acc):
                m = indicator(keys[c], (c * SC + sub_iota).astype(F32))
                return acc + jnp.sum(m.reshape(SC // 8, 8, QB), axis=0)

            acc = lax.fori_loop(0, nsc, count_chunk, jnp.zeros((8, QB), F32))
            return jnp.sum(acc, axis=0, keepdims=True)

        need = kf - count_where(lambda s, kidx: jnp.where(s > thr, 1.0, 0.0))

        def index_bisect(it, carry):
            lo_i, hi_i = carry
            mid = jnp.floor((lo_i + hi_i) * 0.5)
            c = count_where(
                lambda s, kidx: jnp.where(s == thr, jnp.where(kidx <= mid, 1.0, 0.0), 0.0))
            ok = c >= need
            return jnp.where(ok, lo_i, mid), jnp.where(ok, mid, hi_i)

        _, hi_i = lax.fori_loop(
            0, seq.bit_length() + 1, index_bisect,
            (jnp.full((1, QB), -1.0, F32), jnp.full((1, QB), seq - 1.0, F32)))
        tie_s[0:1, :] = jnp.where(tied, hi_i, float(seq))

    tie_idx = tie_s[0:1, :]

    acc_s[...] = jnp.zeros_like(acc_s)

    def att_chunk(c, carry):
        ms, ls = carry
        off = pl.multiple_of(c * KC, KC)
        for j in range(KC // SC):
            sc = keys[c * (KC // SC) + j]
            kidx = ((c * (KC // SC) + j) * SC + sub_iota).astype(F32)
            admit_tied = jnp.where(kidx <= tie_idx, 0.0, -1e30)
            bias_s[j * SC:(j + 1) * SC, :] = jnp.where(
                sc > thr, 0.0, jnp.where(sc == thr, admit_tied, -1e30))
        for h in range(ATT_HEADS):
            lo, hi = h * ATT_HEADDIM, (h + 1) * ATT_HEADDIM
            s_s[h] = jnp.dot(k_refs[lo // slab][pl.ds(off, KC), lo % slab:lo % slab + ATT_HEADDIM],
                             qt_ref[lo:hi, :],
                             preferred_element_type=F32)
        new_ms, new_ls, alphas = [], [], []
        for h in range(ATT_HEADS):
            s = s_s[h] + bias_s[...]
            m_new = jnp.maximum(ms[h], jnp.max(s, axis=0, keepdims=True))
            alpha = jnp.exp(ms[h] - m_new)
            p = jnp.exp(s - m_new)
            new_ls.append(alpha * ls[h] + jnp.sum(p, axis=0, keepdims=True))
            new_ms.append(m_new)
            alphas.append(alpha)
            p_s[h] = p.astype(BF16)
        for h in range(ATT_HEADS):
            lo, hi = h * ATT_HEADDIM, (h + 1) * ATT_HEADDIM
            pv = jnp.dot(vt_refs[lo // slab][lo % slab:lo % slab + ATT_HEADDIM, pl.ds(off, KC)],
                         p_s[h],
                         preferred_element_type=F32)
            acc_s[h] = alphas[h] * acc_s[h] + pv
        return tuple(new_ms), tuple(new_ls)

    m0 = tuple(jnp.full((1, QB), -1e30, F32) for _ in range(ATT_HEADS))
    l0 = tuple(jnp.zeros((1, QB), F32) for _ in range(ATT_HEADS))
    _, ls = lax.fori_loop(0, nkc, att_chunk, (m0, l0))
    for h in range(ATT_HEADS):
        o_ref[:, h * ATT_HEADDIM:(h + 1) * ATT_HEADDIM] = (acc_s[h] / ls[h]).T.astype(o_ref.dtype)


def _dsa(qt, iqt, vt, kk, sm, smt, g, b, bsz, seq):
    qb = min(DSA_QB, seq)
    nq = seq // qb
    t = bsz * seq
    topk = min(TOPK_MAX, seq // 4)
    single = pl.Buffered(1)
    ns = DSA_KV_SPLIT
    slab = ATT_WIDTH // ns
    vt_specs = [pl.BlockSpec((slab, seq), lambda bi, qi, q=q: (q, bi), pipeline_mode=single)
                for q in range(ns)]
    k_specs = [pl.BlockSpec((seq, slab), lambda bi, qi, q=q: (bi, q), pipeline_mode=single)
               for q in range(ns)]
    return pl.pallas_call(
        functools.partial(_dsa_kernel, topk=topk),
        name="dsa",
        grid=(bsz, nq),
        in_specs=[pl.BlockSpec((ATT_WIDTH, qb), lambda bi, qi: (0, bi * nq + qi)),
                  pl.BlockSpec((IDX_HEADS * IDX_DIM, qb), lambda bi, qi: (0, bi * nq + qi)),
                  *vt_specs, *k_specs,
                  pl.BlockSpec((seq, SM_W), lambda bi, qi: (bi, 0), pipeline_mode=single),
                  pl.BlockSpec((SM_W, qb), lambda bi, qi: (0, bi * nq + qi)),
                  pl.BlockSpec((1, IDX_DIM), lambda bi, qi: (0, 0)),
                  pl.BlockSpec((1, IDX_DIM), lambda bi, qi: (0, 0))],
        out_specs=pl.BlockSpec((qb, ATT_WIDTH), lambda bi, qi: (bi * nq + qi, 0)),
        out_shape=jax.ShapeDtypeStruct((t, ATT_WIDTH), BF16),
        scratch_shapes=[pltpu.VMEM((seq, IDX_DIM), BF16),
                        pltpu.VMEM((seq // SEARCH_CHUNK, SEARCH_CHUNK, qb), F32),
                        pltpu.VMEM((ATT_HEADS, ATT_HEADDIM, qb), F32),
                        pltpu.VMEM((ATT_HEADS, KEY_CHUNK, qb), F32),
                        pltpu.VMEM((ATT_HEADS, KEY_CHUNK, qb), BF16),
                        pltpu.VMEM((KEY_CHUNK, qb), F32),
                        pltpu.VMEM((8, qb), F32)],
        compiler_params=pltpu.CompilerParams(
            dimension_semantics=("parallel", "arbitrary"), vmem_limit_bytes=VMEM_LIMIT),
    )(qt, iqt, *([vt] * ns), *([kk] * ns), sm, smt, g.astype(F32).reshape(1, IDX_DIM),
      b.astype(F32).reshape(1, IDX_DIM))


def _route_rows(logits_t, rb):
    aff = _sigmoid(logits_t)
    biased = aff + rb
    rb_rows = [biased[r:r + 1, :] for r in range(N_EXPERTS)]
    ra_rows = [aff[r:r + 1, :] for r in range(N_EXPERTS)]
    epg = EXPERTS_PER_GROUP
    gs = []
    for g in range(N_EXPERT_GROUPS):
        a, b, c, d = rb_rows[epg * g:epg * g + epg]
        p, q = jnp.maximum(a, b), jnp.minimum(a, b)
        r, s = jnp.maximum(c, d), jnp.minimum(c, d)
        gs.append(jnp.maximum(p, r) + jnp.maximum(jnp.minimum(p, r), jnp.maximum(q, s)))
    best = gs[0]
    bidx = jnp.zeros_like(best, dtype=I32)
    for g in range(1, N_EXPERT_GROUPS):
        better = gs[g] > best
        best = jnp.where(better, gs[g], best)
        bidx = jnp.where(better, g, bidx)
    vb = [rb_rows[j] for j in range(epg)]
    va = [ra_rows[j] for j in range(epg)]
    for g in range(1, N_EXPERT_GROUPS):
        pick = bidx == g
        vb = [jnp.where(pick, rb_rows[epg * g + j], vb[j]) for j in range(epg)]
        va = [jnp.where(pick, ra_rows[epg * g + j], va[j]) for j in range(epg)]
    t1, a1, i1 = vb[0], va[0], jnp.zeros_like(bidx)
    for j in range(1, epg):
        better = vb[j] > t1
        t1 = jnp.where(better, vb[j], t1)
        a1 = jnp.where(better, va[j], a1)
        i1 = jnp.where(better, j, i1)
    t2 = jnp.full_like(t1, -jnp.inf)
    a2 = jnp.zeros_like(a1)
    i2 = jnp.zeros_like(i1)
    for j in range(epg):
        vj = jnp.where(i1 == j, -jnp.inf, vb[j])
        better = vj > t2
        t2 = jnp.where(better, vj, t2)
        a2 = jnp.where(better, va[j], a2)
        i2 = jnp.where(better, j, i2)
    den = a1 + a2
    return bidx * epg + i1, bidx * epg + i2, a1 / den, a2 / den


def _outproj_kernel(ys_ref, ya_ref, w1_ref, w2_ref, h_ref, g_ref, b_ref, rwh_ref, rwl_ref, rb_ref,
                    h1_ref, h1b_ref, eidx_ref, gate_ref):
    mix = jnp.dot(ys_ref[...], w1_ref[...], preferred_element_type=F32)
    mix = mix + jnp.dot(ya_ref[...], w2_ref[...], preferred_element_type=F32)
    h1 = _layer_norm_rows(ALPHA * h_ref[...] + mix, g_ref[...], b_ref[...])
    h1_ref[...] = h1
    h1_hi = h1.astype(BF16)
    h1b_ref[...] = h1_hi
    h1_lo = (h1 - h1_hi.astype(F32)).astype(BF16)
    logits = jnp.dot(h1_hi, rwh_ref[...], preferred_element_type=F32)
    logits = logits + jnp.dot(h1_hi, rwl_ref[...], preferred_element_type=F32)
    logits = logits + jnp.dot(h1_lo, rwh_ref[...], preferred_element_type=F32)
    logits_t = logits.T[0:N_EXPERTS, :]
    e1, e2, g1, g2 = _route_rows(logits_t, rb_ref[...])
    eidx_ref[0:1, :] = e1
    eidx_ref[1:2, :] = e2
    gate_ref[0:1, :] = g1
    gate_ref[1:2, :] = g2


def _outproj_ln_router(y_ssd, y_att, w_out, h, ln_g, ln_b, router_w, router_b, tm):
    t, d = h.shape
    w = w_out.astype(BF16)
    rw = jnp.pad(router_w.astype(F32), ((0, 0), (0, V7X_LANES - N_EXPERTS)))
    rw_hi = rw.astype(BF16)
    rw_lo = (rw - rw_hi.astype(F32)).astype(BF16)
    const = lambda shape: pl.BlockSpec(shape, lambda i: (0, 0))
    return pl.pallas_call(
        _outproj_kernel,
        name="outproj_ln_router",
        grid=(t // tm,),
        in_specs=[pl.BlockSpec((tm, SSD_WIDTH), lambda i: (i, 0)),
                  pl.BlockSpec((tm, ATT_WIDTH), lambda i: (i, 0)),
                  pl.BlockSpec((SSD_WIDTH, d), lambda i: (0, 0)),
                  pl.BlockSpec((ATT_WIDTH, d), lambda i: (1, 0)),
                  pl.BlockSpec((tm, d), lambda i: (i, 0)),
                  const((1, d)), const((1, d)), const((d, V7X_LANES)), const((d, V7X_LANES)),
                  const((N_EXPERTS, 1))],
        out_specs=[pl.BlockSpec((tm, d), lambda i: (i, 0)),
                   pl.BlockSpec((tm, d), lambda i: (i, 0)),
                   pl.BlockSpec((2, tm), lambda i: (0, i)),
                   pl.BlockSpec((2, tm), lambda i: (0, i))],
        out_shape=[jax.ShapeDtypeStruct((t, d), F32), jax.ShapeDtypeStruct((t, d), BF16),
                   jax.ShapeDtypeStruct((2, t), I32), jax.ShapeDtypeStruct((2, t), F32)],
        compiler_params=pltpu.CompilerParams(
            dimension_semantics=("parallel",), vmem_limit_bytes=VMEM_LIMIT),
    )(y_ssd, y_att, w, w, h, ln_g.astype(F32).reshape(1, d), ln_b.astype(F32).reshape(1, d),
      rw_hi, rw_lo, router_b.astype(F32).reshape(N_EXPERTS, 1))


def _expert_kernel(be_ref, run_ref, nxt_ref, nv_ref, nu_ref, st_ref, h_hbm, wg_hbm, wu_hbm, wd_hbm,
                   o_ref, xbuf, wg_f, wu_f, wd_f, wg_s, wu_s, wd_s, xsem, wsem, *, layer):
    i = pl.program_id(0)
    nu = nu_ref[0]
    slot = lax.rem(i, 2)
    ns = MOE_W_SPLIT

    def row_copy(tok, r, s):
        return pltpu.make_async_copy(h_hbm.at[pl.ds(tok, 1), :], xbuf.at[s, pl.ds(r, 1), :],
                                     xsem.at[s])

    def row_groups(blk):
        return (nv_ref[blk] + GATHER_GROUP - 1) // GATHER_GROUP

    def issue_rows(blk, s):
        def body(gi, carry):
            for u in range(GATHER_GROUP):
                r = gi * GATHER_GROUP + u
                row_copy(st_ref[blk * MOE_BLOCK + r], r, s).start(priority=GATHER_DMA_PRIORITY)
            return carry
        lax.fori_loop(0, row_groups(blk), body, 0)

    def wait_rows(blk, s):
        def body(gi, carry):
            for u in range(GATHER_GROUP):
                row_copy(0, gi * GATHER_GROUP + u, s).wait()
            return carry
        lax.fori_loop(0, row_groups(blk), body, 0)

    def weight_copies(e, ws):
        copies = []
        for src, dst in ((wg_hbm, wg_f), (wu_hbm, wu_f), (wd_hbm, wd_f)):
            rows = dst.shape[1] // ns
            for q in range(ns):
                copies.append(pltpu.make_async_copy(
                    src.at[layer, e, pl.ds(q * rows, rows), :],
                    dst.at[ws, pl.ds(q * rows, rows), :], wsem.at[ws]))
        return copies

    @pl.when(i == 0)
    def _():
        xbuf[...] = jnp.zeros_like(xbuf)
        issue_rows(0, 0)
        for c in weight_copies(be_ref[0], 0):
            c.start()

    @pl.when(i < nu)
    def _():
        wait_rows(i, slot)

        @pl.when(i + 1 < nu)
        def _():
            issue_rows(i + 1, 1 - slot)

        ws = lax.rem(run_ref[i], 2)

        @pl.when((i == 0) | (be_ref[i] != be_ref[jnp.maximum(i - 1, 0)]))
        def _():
            for c in weight_copies(be_ref[i], ws):
                c.wait()
            wg_s[...] = wg_f[ws].astype(BF16)
            wu_s[...] = wu_f[ws].astype(BF16)
            wd_s[...] = wd_f[ws].astype(BF16)

            @pl.when(nxt_ref[i] >= 0)
            def _():
                for c in weight_copies(nxt_ref[i], 1 - ws):
                    c.start()

        x = xbuf[slot].astype(BF16)
        g = jnp.dot(x, wg_s[...], preferred_element_type=F32)
        u = jnp.dot(x, wu_s[...], preferred_element_type=F32)
        a = (g * _sigmoid(g) * u).astype(BF16)
        o_ref[...] = jnp.dot(a, wd_s[...], preferred_element_type=F32)

    @pl.when(i >= nu)
    def _():
        o_ref[...] = jnp.zeros_like(o_ref)


def _experts(plan, h1, w_gate, w_up, w_down, layer):
    t, d = h1.shape
    cap = plan["slot_tok"].shape[0]
    nb = cap // MOE_BLOCK
    hbm = pl.BlockSpec(memory_space=pl.ANY)
    return pl.pallas_call(
        functools.partial(_expert_kernel, layer=layer),
        name="moe_experts",
        grid_spec=pltpu.PrefetchScalarGridSpec(
            num_scalar_prefetch=6,
            grid=(nb,),
            in_specs=[hbm, hbm, hbm, hbm],
            out_specs=pl.BlockSpec((MOE_BLOCK, d), lambda i, *_: (i, 0)),
            scratch_shapes=[pltpu.VMEM((2, MOE_BLOCK, d), F32),
                            pltpu.VMEM((2, d, D_FF), F32), pltpu.VMEM((2, d, D_FF), F32),
                            pltpu.VMEM((2, D_FF, d), F32),
                            pltpu.VMEM((d, D_FF), BF16), pltpu.VMEM((d, D_FF), BF16),
                            pltpu.VMEM((D_FF, d), BF16),
                            pltpu.SemaphoreType.DMA((2,)), pltpu.SemaphoreType.DMA((2,))]),
        out_shape=jax.ShapeDtypeStruct((cap, d), F32),
        compiler_params=pltpu.CompilerParams(
            dimension_semantics=("arbitrary",), vmem_limit_bytes=VMEM_LIMIT),
    )(plan["block_expert"], plan["run"], plan["next_expert"], plan["valid_rows"], plan["n_used"],
      plan["slot_tok"], h1, w_gate, w_up, w_down)


def _moe_plan(eidx, t):
    flat_e = eidx.reshape(2 * t)
    onehot = (flat_e[:, None] == jnp.arange(N_EXPERTS, dtype=I32)[None, :]).astype(I32)
    cs = jnp.cumsum(onehot, axis=0)
    rank = jnp.sum(onehot * cs, axis=1) - 1
    counts = cs[-1]
    padded = (counts + MOE_BLOCK - 1) // MOE_BLOCK * MOE_BLOCK
    pad_end = jnp.cumsum(padded)
    pad_start = pad_end - padded
    dest = (pad_start[flat_e] + rank).astype(I32)
    cap = 2 * t + N_EXPERTS * MOE_BLOCK
    nb = cap // MOE_BLOCK
    block_start = jnp.arange(nb, dtype=I32) * MOE_BLOCK
    block_expert = jnp.minimum(
        jnp.sum((pad_end[None, :] <= block_start[:, None]).astype(I32), axis=1), N_EXPERTS - 1)
    n_used = (pad_end[-1] // MOE_BLOCK).astype(I32).reshape(1)
    tok = jnp.tile(jnp.arange(t, dtype=I32), 2)
    slot_tok = jnp.zeros((cap,), I32).at[dest].set(tok, unique_indices=True)
    blk = jnp.arange(nb, dtype=I32)
    used = blk < n_used[0]
    valid_rows = jnp.where(
        used, jnp.clip(counts[block_expert] - (block_start - pad_start[block_expert]), 0, MOE_BLOCK), 0)
    first = used & jnp.concatenate([jnp.ones((1,), bool), block_expert[1:] != block_expert[:-1]])
    run = jnp.cumsum(first.astype(I32)) - 1
    later = (block_expert[None, :] > block_expert[:, None]) & used[None, :]
    nxt = jnp.min(jnp.where(later, block_expert[None, :], N_EXPERTS), axis=1)
    next_expert = jnp.where(nxt < N_EXPERTS, nxt, -1)
    plan = dict(block_expert=block_expert, run=run.astype(I32), next_expert=next_expert.astype(I32),
                valid_rows=valid_rows.astype(I32), n_used=n_used, slot_tok=slot_tok)
    return dest, plan


def _ple_kernel(dest_ref, h1_ref, h1b_ref, yb_hbm, gt_ref, p_ref, wg_ref, wp_ref, g_ref, b_ref,
                o_ref, ob_ref, gbuf, sem):
    i = pl.program_id(0)
    n = pl.num_programs(0)
    tm = h1_ref.shape[0]
    t = n * tm
    slot = lax.rem(i, 2)
    nxt_base = jnp.minimum(i + 1, n - 1) * tm

    def row_copy(src_row, k, r, s):
        return pltpu.make_async_copy(yb_hbm.at[pl.ds(src_row, 1), :],
                                     gbuf.at[s, pl.ds(k * tm + r, 1), :], sem.at[s])

    def wait_tile(s):
        pltpu.make_async_copy(yb_hbm.at[pl.ds(0, 2 * tm), :], gbuf.at[s], sem.at[s]).wait()

    @pl.when(i == 0)
    def _():
        def body(r, carry):
            for k in range(2):
                row_copy(dest_ref[k * t + r], k, r, 0).start(priority=k)
            return carry
        lax.fori_loop(0, tm, body, 0)

    wait_tile(slot)
    for r in range(tm):
        for k in range(2):
            row_copy(dest_ref[k * t + nxt_base + r], k, r, 1 - slot).start(priority=k)
    ffn = gt_ref[:, 0:1] * gbuf[slot, 0:tm, :] + gt_ref[:, 1:2] * gbuf[slot, tm:2 * tm, :]
    gate = _sigmoid(jnp.dot(h1b_ref[...], wg_ref[...], preferred_element_type=F32))
    pe = jnp.dot(p_ref[...].astype(BF16), wp_ref[...], preferred_element_type=F32)
    u = ALPHA * h1_ref[...] + ffn + gate * pe
    h2 = _layer_norm_rows(u, g_ref[...], b_ref[...])
    o_ref[...] = h2
    ob_ref[...] = h2.astype(BF16)

    @pl.when(i == n - 1)
    def _():
        wait_tile(1 - slot)


def _ple_ln(dest, h1, h1b, yb, gates_t, p, ple_gate_w, ple_w, ln_g, ln_b, tm):
    t, d = h1.shape
    pd = p.shape[1]
    row = lambda w: pl.BlockSpec((tm, w), lambda i, dest: (i, 0))
    const = lambda shape: pl.BlockSpec(shape, lambda i, dest: (0, 0), pipeline_mode=pl.Buffered(1))
    return pl.pallas_call(
        _ple_kernel,
        name="ple_ln",
        grid_spec=pltpu.PrefetchScalarGridSpec(
            num_scalar_prefetch=1,
            grid=(t // tm,),
            in_specs=[row(d), row(d), pl.BlockSpec(memory_space=pl.ANY), row(2), row(pd),
                      const((d, d)), const((pd, d)), const((1, d)), const((1, d))],
            out_specs=[row(d), row(d)],
            scratch_shapes=[pltpu.VMEM((2, 2 * tm, d), F32), pltpu.SemaphoreType.DMA((2,))]),
        out_shape=[jax.ShapeDtypeStruct((t, d), F32), jax.ShapeDtypeStruct((t, d), BF16)],
        compiler_params=pltpu.CompilerParams(
            dimension_semantics=("arbitrary",), vmem_limit_bytes=VMEM_LIMIT),
    )(dest, h1, h1b, yb, gates_t, p, ple_gate_w.astype(BF16), ple_w.astype(BF16),
      ln_g.astype(F32).reshape(1, d), ln_b.astype(F32).reshape(1, d))


def _layer(h, hb, p_i, w_in, layer, conv_w, conv_b, dt_bias, a_log, d_skip, ssm_norm_w,
           idx_k_norm_g, idx_k_norm_b, w_out, ln1_g, ln1_b, router_w, router_b,
           w_gate, w_up, w_down, ple_w, ple_gate_w, ln2_g, ln2_b, bsz, seq):
    t = bsz * seq
    tm = min(PROJ_TOKENS, t)
    row_tm = min(ROW_TOKENS, t)
    row0 = layer * (OFF_IW + IDX_HEADS)
    za = _proj(hb, w_in, row0, OFF_Z, OFF_DT - OFF_Z, F32, False, tm, tn=(OFF_DT - OFF_Z) // 2)
    wide = ATT_WIDTH
    kk = _proj(hb, w_in, row0, OFF_Q + ATT_WIDTH, ATT_WIDTH, BF16, False, tm, tn=wide)
    qt = _proj(hb, w_in, row0, OFF_Q, ATT_WIDTH, BF16, True, tm, tn=wide,
               out_scale=ATT_HEADDIM ** -0.5)
    vt = _proj(hb, w_in, row0, OFF_Q + 2 * ATT_WIDTH, ATT_WIDTH, BF16, True, tm, tn=wide)
    iqt = _proj(hb, w_in, row0, OFF_IQ, IDX_HEADS * IDX_DIM, BF16, True, tm, tn=wide)
    sm, smt = _proj_small(hb, w_in, row0, tm)

    y_ssd = _ssd(za, sm, conv_w, conv_b, dt_bias, a_log, d_skip, ssm_norm_w, bsz, seq)
    y_att = _dsa(qt, iqt, vt, kk, sm, smt, idx_k_norm_g, idx_k_norm_b, bsz, seq)
    h1, h1b, eidx, gates = _outproj_ln_router(
        y_ssd, y_att, w_out, h, ln1_g, ln1_b, router_w, router_b, row_tm)

    dest, plan = _moe_plan(eidx, t)
    yb = _experts(plan, h1, w_gate, w_up, w_down, layer)
    return _ple_ln(dest, h1, h1b, yb, gates.T, p_i, ple_gate_w, ple_w, ln2_g, ln2_b, row_tm)


def kernel(x, p, w_in, conv_w, conv_b, dt_bias, a_log, d_skip, ssm_norm_w, idx_k_norm_g,
           idx_k_norm_b, w_out, ln1_g, ln1_b, router_w, router_b, w_gate, w_up, w_down,
           ple_w, ple_gate_w, ln2_g, ln2_b):
    bsz, seq, d = x.shape
    t = bsz * seq
    h = x.reshape(t, d).astype(F32)
    hb = h.astype(BF16)
    w_in_t = jnp.swapaxes(w_in, 1, 2).reshape(-1, d)
    for i in range(w_in.shape[0]):
        h, hb = _layer(h, hb, p[i].reshape(t, -1), w_in_t, i, conv_w[i], conv_b[i], dt_bias[i],
                       a_log[i], d_skip[i], ssm_norm_w[i], idx_k_norm_g[i], idx_k_norm_b[i],
                       w_out[i], ln1_g[i], ln1_b[i], router_w, router_b, w_gate, w_up,
                       w_down, ple_w[i], ple_gate_w[i], ln2_g[i], ln2_b[i], bsz, seq)
    return h.reshape(bsz, seq, d).astype(x.dtype)
```
